```python
import math
import jax
import jax.numpy as jnp
from jax import lax
import numpy as np

D_MODEL = 2048
BATCH = 2
SEQ = 8192
DEPTH = 4

HEAD_DIM = 128
BLOCK = 128
A_HEADS = 4
A_KV_HEADS = 2
A_GROUP = A_HEADS // A_KV_HEADS
A_WINDOW = 128
ROPE_THETA = 150000.0
B_HEADS = 4
B_KV_HEADS = 1
B_GROUP = B_HEADS // B_KV_HEADS
CMP_LEN = 32
CMP_STRIDE = 16
CMP_HIDDEN = 256
SEL_LEN = 64
SEL_TOPK = 16
B_WINDOW = 512
C_HEADS = 4
C_QK_DIM = 128
C_V_DIM = 256
C_CONV = 4
C_CHUNK = 64
A_WIDTH = A_HEADS * HEAD_DIM
B_WIDTH = B_HEADS * HEAD_DIM
C_WIDTH = C_HEADS * C_V_DIM
MIX_WIDTH = A_WIDTH + B_WIDTH + C_WIDTH
D_FF = 5632
N_EXPERTS = 8
TOP_K = 2
D_FF_EXPERT = 2816
N_DENSE = (DEPTH + 1) // 2
N_MOE = DEPTH // 2
EPS = 1e-6
BIG = 1e9
SPLIT_SIZES = (
    A_WIDTH, A_KV_HEADS * HEAD_DIM, A_KV_HEADS * HEAD_DIM,
    B_WIDTH,
    B_KV_HEADS * HEAD_DIM, B_KV_HEADS * HEAD_DIM,
    B_KV_HEADS * HEAD_DIM, B_KV_HEADS * HEAD_DIM,
    B_KV_HEADS * HEAD_DIM, B_KV_HEADS * HEAD_DIM,
    3 * B_HEADS,
    C_HEADS * C_QK_DIM, C_HEADS * C_QK_DIM, C_WIDTH, C_HEADS, C_HEADS, C_WIDTH,
)
IN_COLS = sum(SPLIT_SIZES)
SPLIT_POINTS = tuple(int(p) for p in np.cumsum(SPLIT_SIZES)[:-1])

kernel_name = 'hybrid_parallel_heads_nsa_mlstm_moe'


def rmsnorm(x, w):
    xf = x.astype(jnp.float32)
    y = xf * lax.rsqrt(jnp.mean(xf * xf, axis=-1, keepdims=True) + EPS)
    return y.astype(x.dtype) * w


def modulate(h, shift, scale):
    return h * (1.0 + scale[:, None, :]) + shift[:, None, :]


def masked_softmax(s, mask, sink=None):
    s = jnp.where(mask, s.astype(jnp.float32), -jnp.inf)
    m = jnp.max(s, axis=-1, keepdims=True)
    if sink is not None:
        m = jnp.maximum(m, sink)
    m = jnp.where(jnp.isfinite(m), m, 0.0)
    p = jnp.exp(s - m)
    den = jnp.sum(p, axis=-1, keepdims=True)
    if sink is not None:
        den = den + jnp.exp(sink - m)
    return p / jnp.maximum(den, 1e-30)


def rope(x, positions):
    dh = x.shape[-1]
    half = dh // 2
    inv = jnp.exp(-math.log(ROPE_THETA) * jnp.arange(half, dtype=jnp.float32) * 2.0 / dh)
    ang = positions.astype(jnp.float32)[..., None] * inv
    ang = ang.reshape(ang.shape[:2] + (1,) * (x.ndim - 3) + (half,))
    cos, sin = jnp.cos(ang), jnp.sin(ang)
    xf = x.astype(jnp.float32)
    x1, x2 = xf[..., :half], xf[..., half:]
    return jnp.concatenate([x1 * cos - x2 * sin, x2 * cos + x1 * sin], axis=-1).astype(x.dtype)


def banded_attention(q, k, v, window, sink=None):
    B, T, Hkv, G, dh = q.shape
    nblk = T // BLOCK
    nprev = -(-(window - 1) // BLOCK)
    span = (nprev + 1) * BLOCK
    pad = nprev * BLOCK
    kpad = jnp.pad(k, ((0, 0), (pad, 0), (0, 0), (0, 0)))
    vpad = jnp.pad(v, ((0, 0), (pad, 0), (0, 0), (0, 0)))
    idx = jnp.arange(nblk)[:, None] * BLOCK + jnp.arange(span)[None, :]
    kb = jnp.take(kpad, idx, axis=1)
    vb = jnp.take(vpad, idx, axis=1)
    qb = q.reshape(B, nblk, BLOCK, Hkv, G, dh)
    s = jnp.einsum('bnqhgd,bnkhd->bnhgqk', qb, kb) * dh ** -0.5
    q_pos = jnp.arange(T).reshape(nblk, BLOCK)
    k_pos = idx - pad
    diff = q_pos[:, :, None] - k_pos[:, None, :]
    mask = (diff >= 0) & (diff < window) & (k_pos[:, None, :] >= 0)
    p = masked_softmax(s, mask[None, :, None, None], sink)
    o = jnp.einsum('bnhgqk,bnkhd->bnqhgd', p.astype(v.dtype), vb)
    return o.reshape(B, T, Hkv, G, dh)


def causal_depthwise_conv(x, w):
    C = x.shape[-1]
    return lax.conv_general_dilated(x, w[:, None, :], window_strides=(1,),
                                    padding=[(w.shape[0] - 1, 0)],
                                    dimension_numbers=('NWC', 'WIO', 'NWC'),
                                    feature_group_count=C)


def compress_blocks(kv, pe, w1, w2):
    B, T, Hkv, dh = kv.shape
    n_cmp = (T - CMP_LEN) // CMP_STRIDE + 1
    idx = jnp.arange(n_cmp)[:, None] * CMP_STRIDE + jnp.arange(CMP_LEN)[None, :]
    blocks = jnp.take(kv, idx, axis=1) + pe[:, None, :]
    blocks = jnp.swapaxes(blocks, 2, 3).reshape(B, n_cmp, Hkv, CMP_LEN * dh)
    return jax.nn.gelu(blocks @ w1) @ w2


def selected_attention(q, k, v, sel_idx):
    B, T, Hkv, G, dh = q.shape
    n_top = sel_idx.shape[-1]
    nblk = T // BLOCK
    kt = jnp.swapaxes(k, 1, 2)
    vt = jnp.swapaxes(v, 1, 2)
    gather = jax.vmap(jax.vmap(lambda a, i: a[i]))
    qs = jnp.moveaxis(q.reshape(B, nblk, BLOCK, Hkv, G, dh), 1, 0)
    idx_s = jnp.moveaxis(sel_idx.reshape(B, Hkv, nblk, BLOCK, n_top), 2, 0)

    def one_block(args):
        qb, ib, blk = args
        tok = (ib[..., None] * SEL_LEN + jnp.arange(SEL_LEN)).reshape(B, Hkv, BLOCK, n_top * SEL_LEN)
        kg = gather(kt, tok)
        vg = gather(vt, tok)
        s = jnp.einsum('bqhgd,bhqkd->bhgqk', qb, kg) * dh ** -0.5
        q_pos = blk * BLOCK + jnp.arange(BLOCK)
        mask = tok <= q_pos[None, None, :, None]
        p = masked_softmax(s, mask[:, :, None])
        return jnp.einsum('bhgqk,bhqkd->bqhgd', p.astype(v.dtype), vg)

    o = lax.map(one_block, (qs, idx_s, jnp.arange(nblk)))
    return jnp.moveaxis(o, 0, 1).reshape(B, T, Hkv, G, dh)


def nsa_mixer(q, kc, vc, ks, vs, kw, vw, gate_logits, pe_k, pe_v, ck_w1, ck_w2, cv_w1, cv_w2):
    B, T, Hkv, G, dh = q.shape
    k_cmp = compress_blocks(kc, pe_k, ck_w1, ck_w2)
    v_cmp = compress_blocks(vc, pe_v, cv_w1, cv_w2)
    n_cmp = k_cmp.shape[1]
    starts = jnp.arange(n_cmp) * CMP_STRIDE
    t = jnp.arange(T)
    cmask = (starts + CMP_LEN - 1)[None, :] <= t[:, None]
    s = jnp.einsum('bthgd,bnhd->bhgtn', q, k_cmp) * dh ** -0.5
    p_cmp = masked_softmax(s, cmask)
    o_cmp = jnp.einsum('bhgtn,bnhd->bthgd', p_cmp.astype(vc.dtype), v_cmp)
    n_sel = T // SEL_LEN
    sel_start = jnp.arange(n_sel) * SEL_LEN
    overlap = jnp.clip(jnp.minimum(starts[:, None] + CMP_LEN, sel_start[None, :] + SEL_LEN)
                       - jnp.maximum(starts[:, None], sel_start[None, :]), 0, None)
    overlap = overlap.astype(jnp.float32) / CMP_LEN
    imp = jnp.einsum('bhgtn,nj->bhtj', p_cmp, overlap)
    j = jnp.arange(n_sel)[None, :]
    cur = (t // SEL_LEN)[:, None]
    forced = (j == 0) | (j == cur) | (j == cur - 1)
    valid = j * SEL_LEN <= t[:, None]
    imp = jnp.where(forced, BIG, jnp.where(valid, imp, -BIG))
    _, sel_idx = lax.top_k(imp, min(SEL_TOPK, n_sel))
    o_sel = selected_attention(q, ks, vs, sel_idx)
    o_win = banded_attention(q, kw, vw, B_WINDOW)
    g = jax.nn.sigmoid(gate_logits.reshape(B, T, Hkv, G, 3))
    o = g[..., 0:1] * o_cmp + g[..., 1:2] * o_sel + g[..., 2:3] * o_win
    return o.reshape(B, T, B_WIDTH)


def mlstm_chunkwise(q, k, v, i_pre, f_pre):
    B, T, H, dqk = q.shape
    dv = v.shape[-1]
    nc = T // C_CHUNK
    f32 = jnp.float32

    def chunks(a):
        a = a.astype(f32).reshape((B, nc, C_CHUNK, H) + a.shape[3:])
        return jnp.swapaxes(jnp.moveaxis(a, 1, 0), 2, 3)

    xs = (chunks(q), chunks(k), chunks(v), chunks(i_pre),
          chunks(jax.nn.log_sigmoid(f_pre.astype(f32))))
    causal = jnp.tril(jnp.ones((C_CHUNK, C_CHUNK), dtype=bool))

    def step(carry, inp):
        Cm, nv, m = carry
        qb, kb, vb, ib, lfb = inp
        b = jnp.cumsum(lfb, axis=-1)
        D = jnp.where(causal, b[..., :, None] - b[..., None, :] + ib[..., None, :], -jnp.inf)
        m_inter = b + m[..., None]
        m_t = jnp.maximum(m_inter, jnp.max(D, axis=-1))
        S = jnp.einsum('bhtd,bhsd->bhts', qb, kb) * jnp.exp(D - m_t[..., None])
        inter = jnp.exp(m_inter - m_t)
        num = jnp.einsum('bhts,bhsv->bhtv', S, vb) + inter[..., None] * jnp.einsum('bhvd,bhtd->bhtv', Cm, qb)
        den = jnp.sum(S, axis=-1) + inter * jnp.einsum('bhd,bhtd->bht', nv, qb)
        h = num / jnp.maximum(jnp.abs(den), jnp.exp(-m_t))[..., None]
        bL = b[..., -1]
        g = bL[..., None] - b + ib
        m_new = jnp.maximum(bL + m, jnp.max(g, axis=-1))
        decay = jnp.exp(g - m_new[..., None])
        keep = jnp.exp(bL + m - m_new)
        C_new = keep[..., None, None] * Cm + jnp.einsum('bhs,bhsv,bhsd->bhvd', decay, vb, kb)
        n_new = keep[..., None] * nv + jnp.einsum('bhs,bhsd->bhd', decay, kb)
        return (C_new, n_new, m_new), h

    init = (jnp.zeros((B, H, dv, dqk), f32), jnp.zeros((B, H, dqk), f32), jnp.zeros((B, H), f32))
    _, hs = lax.scan(step, init, xs)
    return jnp.transpose(hs, (1, 0, 3, 2, 4)).reshape(B, T, H, dv)


def mlstm_mixer(cq, ck, cv, ci, cf, co, conv_w, gate_b, norm_w):
    B, T, _ = cq.shape
    qk = jax.nn.silu(causal_depthwise_conv(jnp.concatenate([cq, ck], axis=-1), conv_w))
    q = qk[..., :C_HEADS * C_QK_DIM].reshape(B, T, C_HEADS, C_QK_DIM)
    k = qk[..., C_HEADS * C_QK_DIM:].reshape(B, T, C_HEADS, C_QK_DIM) * C_QK_DIM ** -0.5
    v = cv.reshape(B, T, C_HEADS, C_V_DIM)
    h = mlstm_chunkwise(q, k, v, ci + gate_b[:C_HEADS], cf + gate_b[C_HEADS:])
    h = rmsnorm(h, norm_w.reshape(C_HEADS, C_V_DIM))
    return h.astype(cv.dtype).reshape(B, T, C_WIDTH) * jax.nn.sigmoid(co)


def hybrid_mixer(h, positions, w_in, gate_b, a_sinks, pe_k, pe_v, ck_w1, ck_w2, cv_w1, cv_w2,
                 conv_w, norm_w, w_out):
    B, T, _ = h.shape
    (aq, ak, av, bq, bkc, bvc, bks, bvs, bkw, bvw, bg,
     cq, ck, cv, ci, cf, co) = jnp.split(h @ w_in, SPLIT_POINTS, axis=-1)
    qa = rope(aq.reshape(B, T, A_KV_HEADS, A_GROUP, HEAD_DIM), positions)
    ka = rope(ak.reshape(B, T, A_KV_HEADS, HEAD_DIM), positions)
    va = av.reshape(B, T, A_KV_HEADS, HEAD_DIM)
    sink = a_sinks.reshape(A_KV_HEADS, A_GROUP)[None, None, :, :, None, None].astype(jnp.float32)
    o_a = banded_attention(qa, ka, va, A_WINDOW, sink).reshape(B, T, A_WIDTH)
    kv_shape = (B, T, B_KV_HEADS, HEAD_DIM)
    o_b = nsa_mixer(bq.reshape(B, T, B_KV_HEADS, B_GROUP, HEAD_DIM),
                    bkc.reshape(kv_shape), bvc.reshape(kv_shape), bks.reshape(kv_shape),
                    bvs.reshape(kv_shape), bkw.reshape(kv_shape), bvw.reshape(kv_shape),
                    bg, pe_k, pe_v, ck_w1, ck_w2, cv_w1, cv_w2)
    o_c = mlstm_mixer(cq, ck, cv, ci, cf, co, conv_w, gate_b, norm_w)
    return jnp.concatenate([o_a, o_b, o_c], axis=-1) @ w_out


def swiglu(h, w1, w3, w2):
    return (jax.nn.silu(h @ w1) * (h @ w3)) @ w2


def moe_swiglu(h, router, w1, w3, w2):
    logits = (h @ router).astype(jnp.float32)
    top_vals, top_idx = lax.top_k(logits, TOP_K)
    top_w = jax.nn.softmax(top_vals, axis=-1)
    comb = jnp.sum(jax.nn.one_hot(top_idx, N_EXPERTS, dtype=jnp.float32) * top_w[..., None], axis=-2)
    y = jnp.zeros_like(h)
    for e in range(N_EXPERTS):
        y = y + comb[..., e:e + 1].astype(h.dtype) * swiglu(h, w1[e], w3[e], w2[e])
    return y


def setup_inputs(seed: int = 0) -> dict:
    key = jax.random.key(seed)
    ks = jax.random.split(key, 32)
    f32 = jnp.float32

    def nrm(k, shape, scale):
        return jax.random.normal(k, shape, f32) * scale

    return {
        'x': nrm(ks[0], (BATCH, SEQ, D_MODEL), 1.0),
        'c': nrm(ks[1], (BATCH, D_MODEL), 1.0),
        'positions': jnp.broadcast_to(jnp.arange(SEQ, dtype=jnp.int32), (BATCH, SEQ)),
        'ada_w': nrm(ks[2], (DEPTH, D_MODEL, 6 * D_MODEL), D_MODEL ** -0.5),
        'ada_b': nrm(ks[3], (DEPTH, 6 * D_MODEL), 0.02),
        'norm_mix_w': 1.0 + nrm(ks[4], (DEPTH, D_MODEL), 0.02),
        'norm_ffn_w': 1.0 + nrm(ks[5], (DEPTH, D_MODEL), 0.02),
        'w_in': nrm(ks[6], (DEPTH, D_MODEL, IN_COLS), D_MODEL ** -0.5),
        'mlstm_gate_b': jnp.concatenate([nrm(ks[7], (DEPTH, C_HEADS), 0.1),
                                         3.0 + nrm(ks[8], (DEPTH, C_HEADS), 0.5)], axis=-1),
        'a_sinks': nrm(ks[9], (DEPTH, A_HEADS), 0.5),
        'nsa_pe_k': nrm(ks[10], (DEPTH, CMP_LEN, HEAD_DIM), 0.02),
        'nsa_pe_v': nrm(ks[11], (DEPTH, CMP_LEN, HEAD_DIM), 0.02),
        'nsa_ck_w1': nrm(ks[12], (DEPTH, CMP_LEN * HEAD_DIM, CMP_HIDDEN), (CMP_LEN * HEAD_DIM) ** -0.5),
        'nsa_ck_w2': nrm(ks[13], (DEPTH, CMP_HIDDEN, HEAD_DIM), CMP_HIDDEN ** -0.5),
        'nsa_cv_w1': nrm(ks[14], (DEPTH, CMP_LEN * HEAD_DIM, CMP_HIDDEN), (CMP_LEN * HEAD_DIM) ** -0.5),
        'nsa_cv_w2': nrm(ks[15], (DEPTH, CMP_HIDDEN, HEAD_DIM), CMP_HIDDEN ** -0.5),
        'mlstm_conv_w': nrm(ks[16], (DEPTH, C_CONV, 2 * C_HEADS * C_QK_DIM), C_CONV ** -0.5),
        'mlstm_norm_w': 1.0 + nrm(ks[17], (DEPTH, C_WIDTH), 0.02),
        'w_out': nrm(ks[18], (DEPTH, MIX_WIDTH, D_MODEL), MIX_WIDTH ** -0.5),
        'ffn_w1': nrm(ks[19], (N_DENSE, D_MODEL, D_FF), D_MODEL ** -0.5),
        'ffn_w3': nrm(ks[20], (N_DENSE, D_MODEL, D_FF), D_MODEL ** -0.5),
        'ffn_w2': nrm(ks[21], (N_DENSE, D_FF, D_MODEL), D_FF ** -0.5),
        'moe_router': nrm(ks[22], (N_MOE, D_MODEL, N_EXPERTS), D_MODEL ** -0.5),
        'moe_w1': nrm(ks[23], (N_MOE, N_EXPERTS, D_MODEL, D_FF_EXPERT), D_MODEL ** -0.5),
        'moe_w3': nrm(ks[24], (N_MOE, N_EXPERTS, D_MODEL, D_FF_EXPERT), D_MODEL ** -0.5),
        'moe_w2': nrm(ks[25], (N_MOE, N_EXPERTS, D_FF_EXPERT, D_MODEL), D_FF_EXPERT ** -0.5),
        'final_norm_w': 1.0 + nrm(ks[26], (D_MODEL,), 0.02),
    }


def reference(x, c, positions, ada_w, ada_b, norm_mix_w, norm_ffn_w, w_in, mlstm_gate_b, a_sinks,
              nsa_pe_k, nsa_pe_v, nsa_ck_w1, nsa_ck_w2, nsa_cv_w1, nsa_cv_w2, mlstm_conv_w,
              mlstm_norm_w, w_out, ffn_w1, ffn_w3, ffn_w2, moe_router, moe_w1, moe_w3, moe_w2,
              final_norm_w):
    c_act = jax.nn.silu(c)
    for l in range(DEPTH):
        mod = c_act @ ada_w[l] + ada_b[l]
        sh1, sc1, g1, sh2, sc2, g2 = jnp.split(mod, 6, axis=-1)
        h = modulate(rmsnorm(x, norm_mix_w[l]), sh1, sc1)
        mix = hybrid_mixer(h, positions, w_in[l], mlstm_gate_b[l], a_sinks[l], nsa_pe_k[l],
                           nsa_pe_v[l], nsa_ck_w1[l], nsa_ck_w2[l], nsa_cv_w1[l], nsa_cv_w2[l],
                           mlstm_conv_w[l], mlstm_norm_w[l], w_out[l])
        x = x + g1[:, None, :] * mix
        h = modulate(rmsnorm(x, norm_ffn_w[l]), sh2, sc2)
        if l % 2 == 0:
            f = swiglu(h, ffn_w1[l // 2], ffn_w3[l // 2], ffn_w2[l // 2])
        else:
            f = moe_swiglu(h, moe_router[l // 2], moe_w1[l // 2], moe_w3[l // 2], moe_w2[l // 2])
        x = x + g2[:, None, :] * f
    return rmsnorm(x, final_norm_w)
```

```python
import functools
import math

import numpy as np
import jax
import jax.numpy as jnp
from jax import lax
from jax.experimental import pallas as pl
from jax.experimental.pallas import tpu as pltpu

F32 = jnp.float32
BF16 = jnp.bfloat16

HEAD_DIM = 128
A_HEADS, A_KV_HEADS, A_WINDOW = 4, 2, 128
ROPE_THETA = 150000.0
B_HEADS = 4
CMP_LEN, CMP_STRIDE = 32, 16
SEL_LEN, SEL_TOPK, B_WINDOW = 64, 16, 512
C_HEADS, C_QK_DIM, C_V_DIM, C_CONV = 4, 128, 256, 4
N_EXPERTS = 8
EPS = 1e-6
BIG = 1e9
NEG = -1e30
SCALE = HEAD_DIM ** -0.5

COL_AQ, COL_AK, COL_AV = 0, 512, 768
COL_CQK, COL_CV, COL_CO = 1024, 2048, 3072
COL_BQ = 4096
COL_BKC, COL_BVC, COL_BKS, COL_BVS, COL_BKW, COL_BVW = 4608, 4736, 4864, 4992, 5120, 5248
COL_SMALL = 5376
LANE_CI, LANE_CF = 12, 16
PROJ_COLS = 5632

VMEM_LIMIT_MB = 56


def _params(n_axes, vmem_mb=VMEM_LIMIT_MB):
    return pltpu.CompilerParams(dimension_semantics=("arbitrary",) * n_axes,
                                vmem_limit_bytes=vmem_mb * 2 ** 20)


def _dot(a, b):
    return jnp.dot(a, b, preferred_element_type=F32)


def _dot_nt(a, b):
    return lax.dot_general(a, b, (((1,), (1,)), ((), ())), preferred_element_type=F32)


def _silu(v):
    return v * jax.nn.sigmoid(v)


def _norm_mod(x, nw, shift, scale):
    ms = jnp.mean(x * x, axis=-1, keepdims=True)
    return (x * lax.rsqrt(ms + EPS) * nw) * (1.0 + scale) + shift


def _ada_kernel(c_ref, w_ref, b_ref, o_ref):
    act = _silu(c_ref[...]).astype(BF16)
    o_ref[...] = _dot(act, w_ref[...].astype(BF16)) + b_ref[...]


def ada_modulation(c, ada_w, ada_b):
    depth, d, n6 = ada_w.shape
    bsz = c.shape[0]
    assert bsz <= 8
    cp = jnp.zeros((8, d), F32).at[:bsz].set(c)
    tn = 512
    out = pl.pallas_call(
        _ada_kernel, grid=(depth, n6 // tn),
        in_specs=[pl.BlockSpec((8, d), lambda l, j: (0, 0)),
                  pl.BlockSpec((None, d, tn), lambda l, j: (l, 0, j)),
                  pl.BlockSpec((None, 1, tn), lambda l, j: (l, 0, j))],
        out_specs=pl.BlockSpec((None, 8, tn), lambda l, j: (l, 0, j)),
        out_shape=jax.ShapeDtypeStruct((depth, 8, n6), F32),
        compiler_params=_params(2), name="ada_modulation")(cp, ada_w, ada_b.reshape(depth, 1, n6))
    return out.reshape(depth, 8, 6, d)


def _rope_table_kernel(pos_ref, cos_ref, sin_ref):
    pos = pos_ref[...].astype(F32)
    lane = lax.broadcasted_iota(jnp.int32, (1, HEAD_DIM), 1)
    half = HEAD_DIM // 2
    inv = jnp.exp((lane & (half - 1)).astype(F32) * (-math.log(ROPE_THETA) * 2.0 / HEAD_DIM))
    ang = pos * inv
    cos_ref[...] = jnp.cos(ang)
    sin_ref[...] = jnp.where(lane < half, -1.0, 1.0) * jnp.sin(ang)


def rope_tables(positions):
    bsz, t = positions.shape
    tt = 512
    spec = pl.BlockSpec((None, tt, HEAD_DIM), lambda b, i: (b, i, 0))
    return pl.pallas_call(
        _rope_table_kernel, grid=(bsz, t // tt),
        in_specs=[pl.BlockSpec((None, tt, 1), lambda b, i: (b, i, 0))],
        out_specs=[spec, spec],
        out_shape=[jax.ShapeDtypeStruct((bsz, t, HEAD_DIM), F32)] * 2,
        compiler_params=_params(2), name="rope_tables")(positions.reshape(bsz, t, 1))


def _proj_kernel(x_ref, nw_ref, mod_ref, w_ref, o_ref, h_scr, *, shift_row, scale_row):
    @pl.when(pl.program_id(1) == 0)
    def _():
        h = _norm_mod(x_ref[...], nw_ref[...], mod_ref[shift_row:shift_row + 1, :],
                      mod_ref[scale_row:scale_row + 1, :])
        h_scr[...] = h.astype(BF16)

    o_ref[...] = _dot(h_scr[...], w_ref[...])


def norm_proj(x, norm_w, mod, w, layer, seq, *, shift_row, scale_row, tm=1024, tn=512):
    n, d = x.shape
    p = w.shape[-1]
    return pl.pallas_call(
        functools.partial(_proj_kernel, shift_row=shift_row, scale_row=scale_row),
        grid=(n // tm, p // tn),
        in_specs=[pl.BlockSpec((tm, d), lambda i, j: (i, 0)),
                  pl.BlockSpec((None, 1, d), lambda i, j: (layer, 0, 0)),
                  pl.BlockSpec((None, None, 6, d), lambda i, j: (layer, (i * tm) // seq, 0, 0)),
                  pl.BlockSpec((None, d, tn), lambda i, j: (layer, 0, j))],
        out_specs=pl.BlockSpec((tm, tn), lambda i, j: (i, j)),
        out_shape=jax.ShapeDtypeStruct((n, p), F32),
        scratch_shapes=[pltpu.VMEM((tm, d), BF16)],
        compiler_params=_params(2), name="norm_proj")(x, norm_w, mod, w)


def _swa_kernel(*refs, tq, window, n_heads, group, rope, sinks):
    it = iter(refs)
    q_ref, kp_ref, kc_ref, vp_ref, vc_ref = (next(it) for _ in range(5))
    if rope:
        cp_ref, sp_ref, cc_ref, sc_ref = (next(it) for _ in range(4))
    if sinks:
        sink_ref = next(it)
    o_ref = next(it)
    i = pl.program_id(1)
    row = lax.broadcasted_iota(jnp.int32, (tq, 2 * tq), 0)
    col = lax.broadcasted_iota(jnp.int32, (tq, 2 * tq), 1)
    diff = row + tq - col + jnp.where(col < tq, jnp.where(i > 0, 0, window), 0)
    mask = jnp.abs(2 * diff - (window - 1)) <= (window - 1)

    def rotate(v, c, s):
        return v * c + pltpu.roll(v, HEAD_DIM // 2, 1) * s

    kv_cache = {}
    for hq in range(n_heads):
        kv = hq // group
        if kv not in kv_cache:
            sl = slice(kv * HEAD_DIM, (kv + 1) * HEAD_DIM)
            kp, kc = kp_ref[:, sl], kc_ref[:, sl]
            if rope:
                kp = rotate(kp, cp_ref[...], sp_ref[...])
                kc = rotate(kc, cc_ref[...], sc_ref[...])
            k = jnp.concatenate([kp, kc], axis=0).astype(BF16)
            v = jnp.concatenate([vp_ref[:, sl], vc_ref[:, sl]], axis=0).astype(BF16)
            kv_cache[kv] = (k, v)
        k, v = kv_cache[kv]
        hs = slice(hq * HEAD_DIM, (hq + 1) * HEAD_DIM)
        q = q_ref[:, hs]
        if rope:
            q = rotate(q, cc_ref[...], sc_ref[...])
        s = _dot_nt((q * SCALE).astype(BF16), k)
        s = jnp.where(mask, s, NEG)
        m = jnp.max(s, axis=-1, keepdims=True)
        if sinks:
            m = jnp.maximum(m, sink_ref[hq])
        m = jnp.where(m < 0.5 * NEG, 0.0, m)
        p = jnp.exp(s - m)
        den = jnp.sum(p, axis=-1, keepdims=True)
        if sinks:
            den = den + jnp.exp(sink_ref[hq] - m)
        o = _dot(p.astype(BF16), v) / jnp.maximum(den, 1e-30)
        o_ref[:, hs] = o.astype(o_ref.dtype)


def sliding_window_attention(proj, *, col_q, col_k, col_v, n_heads, n_kv, window, tq,
                             tables=None, sinks=None, out_dtype=BF16):
    bsz, t, _ = proj.shape
    assert window - 1 <= tq and t % tq == 0
    wq, wkv = n_heads * HEAD_DIM, n_kv * HEAD_DIM
    assert col_q % wq == 0 and col_k % wkv == 0 and col_v % wkv == 0
    cur = lambda b, i: (b, i, 0)
    prev = lambda b, i: (b, jnp.maximum(i - 1, 0), 0)
    kvspec = lambda col, im: pl.BlockSpec(
        (None, tq, wkv), lambda b, i: (b, im(b, i)[1], col // wkv))
    in_specs = [pl.BlockSpec((None, tq, wq), lambda b, i: (b, i, col_q // wq)),
                kvspec(col_k, prev), kvspec(col_k, cur), kvspec(col_v, prev), kvspec(col_v, cur)]
    args = [proj] * 5
    if tables is not None:
        cos, sin = tables
        tp = pl.BlockSpec((None, tq, HEAD_DIM), prev)
        tc = pl.BlockSpec((None, tq, HEAD_DIM), cur)
        in_specs += [tp, tp, tc, tc]
        args += [cos, sin, cos, sin]
    if sinks is not None:
        in_specs.append(pl.BlockSpec(memory_space=pltpu.SMEM))
        args.append(sinks)
    return pl.pallas_call(
        functools.partial(_swa_kernel, tq=tq, window=window, n_heads=n_heads, group=n_heads // n_kv,
                          rope=tables is not None, sinks=sinks is not None),
        grid=(bsz, t // tq), in_specs=in_specs,
        out_specs=pl.BlockSpec((None, tq, wq), cur),
        out_shape=jax.ShapeDtypeStruct((bsz, t, wq), out_dtype),
        compiler_params=_params(2), name="sliding_window_attention")(*args)


def _compress_kernel(r_ref, pe_ref, w1_ref, w2_ref, o_ref):
    n16, half = r_ref.shape
    r = r_ref[...].astype(BF16)
    w1 = w1_ref[...].astype(BF16)
    first = _dot(r, w1[:half])
    second = _dot(r, w1[half:])
    pe = _dot(pe_ref[...].astype(BF16), w1)[0:1]
    pre = first + pltpu.roll(second, n16 - 1, 0) + pe
    hid = jax.nn.gelu(pre, approximate=True)
    o_ref[...] = _dot(hid.astype(BF16), w2_ref[...].astype(BF16))


def compress_blocks(chunks, pe, w1, w2, layer):
    bsz, n16, half = chunks.shape
    hid = w1.shape[-1]
    pe8 = jnp.broadcast_to(pe[layer].reshape(1, 2 * half), (8, 2 * half))
    return pl.pallas_call(
        _compress_kernel, grid=(bsz,),
        in_specs=[pl.BlockSpec((None, n16, half), lambda b: (b, 0, 0)),
                  pl.BlockSpec((8, 2 * half), lambda b: (0, 0)),
                  pl.BlockSpec((None, 2 * half, hid), lambda b: (layer, 0, 0)),
                  pl.BlockSpec((None, hid, HEAD_DIM), lambda b: (layer, 0, 0))],
        out_specs=pl.BlockSpec((None, n16, HEAD_DIM), lambda b: (b, 0, 0)),
        out_shape=jax.ShapeDtypeStruct((bsz, n16, HEAD_DIM), F32),
        compiler_params=_params(1), name="compress_blocks")(chunks, pe8, w1, w2)


def _heads_transposed(q, tq):
    n = q.shape[1] // HEAD_DIM
    return jnp.concatenate([(q[:, g * HEAD_DIM:(g + 1) * HEAD_DIM] * SCALE).T for g in range(n)],
                           axis=1).astype(BF16)


def _cmp_select_kernel(q_ref, kc_ref, vc_ref, ov_ref, ocmp_ref, bias_ref, *, tq, n_sel, topk):
    i = pl.program_id(1)
    nc = kc_ref.shape[0]
    hq = B_HEADS * tq
    qt = _heads_transposed(q_ref[...], tq)
    st = _dot(kc_ref[...].astype(BF16), qt)
    n_row = lax.broadcasted_iota(jnp.int32, (nc, hq), 0)
    t_col = i * tq + (lax.broadcasted_iota(jnp.int32, (nc, hq), 1) & (tq - 1))
    st = jnp.where(n_row * CMP_STRIDE + (CMP_LEN - 1) <= t_col, st, NEG)
    m = jnp.max(st, axis=0, keepdims=True)
    m = jnp.where(m < 0.5 * NEG, 0.0, m)
    p = jnp.exp(st - m)
    p = p / jnp.maximum(jnp.sum(p, axis=0, keepdims=True), 1e-30)
    pb = p.astype(BF16)
    ot = _dot(vc_ref[...].T.astype(BF16), pb)
    for g in range(B_HEADS):
        ocmp_ref[:, g * HEAD_DIM:(g + 1) * HEAD_DIM] = ot[:, g * tq:(g + 1) * tq].T
    imp4 = _dot(ov_ref[...], pb)
    imp = imp4[:, 0:tq]
    for g in range(1, B_HEADS):
        imp = imp + imp4[:, g * tq:(g + 1) * tq]
    nsp = imp.shape[0]
    j = lax.broadcasted_iota(jnp.int32, (nsp, tq), 0)
    t = i * tq + lax.broadcasted_iota(jnp.int32, (nsp, tq), 1)
    cur = t >> 6
    imp = jnp.where((j << 6) <= t, imp, -BIG)
    for forced in (0, cur, cur - 1):
        imp = jnp.where(j == forced, BIG, imp)
    imp = jnp.where(j < n_sel, imp, -3e38)
    sel = jnp.zeros((nsp, tq), F32)
    for _ in range(topk):
        mx = jnp.max(imp, axis=0, keepdims=True)
        idx = jnp.min(jnp.where(imp == mx, j, nsp), axis=0, keepdims=True)
        hit = j == idx
        sel = jnp.where(hit, 1.0, sel)
        imp = jnp.where(hit, -3e38, imp)
    bias_ref[...] = jnp.where(sel > 0.0, 0.0, jnp.where(j < n_sel, -BIG, 0.0))


def _overlap_matrix_t(t):
    n16, n_sel = t // CMP_STRIDE, t // SEL_LEN
    nsp = -(-n_sel // 128) * 128
    starts = np.arange(n16) * CMP_STRIDE
    sel_start = np.arange(n_sel) * SEL_LEN
    ov = np.clip(np.minimum(starts[:, None] + CMP_LEN, sel_start[None, :] + SEL_LEN)
                 - np.maximum(starts[:, None], sel_start[None, :]), 0, None).astype(np.float32) / CMP_LEN
    out = np.zeros((nsp, n16), np.float32)
    out[:n_sel] = ov.T
    return jnp.asarray(out, BF16)


def compressed_attention_select(proj, k_cmp, v_cmp, *, tq=128):
    bsz, t, _ = proj.shape
    n16, n_sel = t // CMP_STRIDE, t // SEL_LEN
    assert SEL_LEN == 64 and tq == 128
    ov = _overlap_matrix_t(t)
    nsp = ov.shape[0]
    wq = B_HEADS * HEAD_DIM
    return pl.pallas_call(
        functools.partial(_cmp_select_kernel, tq=tq, n_sel=n_sel, topk=min(SEL_TOPK, n_sel)),
        grid=(bsz, t // tq),
        in_specs=[pl.BlockSpec((None, tq, wq), lambda b, i: (b, i, COL_BQ // wq)),
                  pl.BlockSpec((None, n16, HEAD_DIM), lambda b, i: (b, 0, 0)),
                  pl.BlockSpec((None, n16, HEAD_DIM), lambda b, i: (b, 0, 0)),
                  pl.BlockSpec((nsp, n16), lambda b, i: (0, 0))],
        out_specs=[pl.BlockSpec((None, tq, wq), lambda b, i: (b, i, 0)),
                   pl.BlockSpec((None, nsp, tq), lambda b, i: (b, 0, i))],
        out_shape=[jax.ShapeDtypeStruct((bsz, t, wq), F32),
                   jax.ShapeDtypeStruct((bsz, nsp, t), F32)],
        compiler_params=_params(2), name="compressed_attention_select")(proj, k_cmp, v_cmp, ov)


def _selected_kernel(q_ref, bias_ref, ks_ref, vs_ref, o_ref, kaug_scr, vt_scr, m_scr, l_scr, acc_scr,
                     *, tq, kb):
    i = pl.program_id(1)
    t_all = ks_ref.shape[0]
    nsp = bias_ref.shape[0]
    hq = B_HEADS * tq

    @pl.when(i == 0)
    def _():
        def prep(c, carry):
            r0 = pl.multiple_of(c * kb, kb)
            kaug_scr[pl.ds(r0, kb), 0:HEAD_DIM] = ks_ref[pl.ds(r0, kb), :].astype(BF16)
            key = r0 + lax.broadcasted_iota(jnp.int32, (kb, nsp), 0)
            blk = lax.broadcasted_iota(jnp.int32, (kb, nsp), 1)
            kaug_scr[pl.ds(r0, kb), HEAD_DIM:] = jnp.where((key >> 6) == blk, 1.0, 0.0).astype(BF16)
            vt_scr[c] = vs_ref[pl.ds(r0, kb), :].T.astype(BF16)
            return carry
        lax.fori_loop(0, t_all // kb, prep, 0)

    bias = bias_ref[...]
    qa = jnp.concatenate([_heads_transposed(q_ref[...], tq),
                          jnp.concatenate([bias] * B_HEADS, axis=1).astype(BF16)], axis=0)
    m_scr[...] = jnp.full(m_scr.shape, NEG, F32)
    l_scr[...] = jnp.zeros(l_scr.shape, F32)
    acc_scr[...] = jnp.zeros(acc_scr.shape, F32)

    def tile(kt, causal):
        r0 = pl.multiple_of(kt * kb, kb)
        st = _dot(kaug_scr[pl.ds(r0, kb), :], qa)
        if causal:
            key = r0 + lax.broadcasted_iota(jnp.int32, (kb, hq), 0)
            tpos = i * tq + (lax.broadcasted_iota(jnp.int32, (kb, hq), 1) & (tq - 1))
            st = jnp.where(key <= tpos, st, NEG)
        m_old = m_scr[...]
        m_new = jnp.maximum(m_old, jnp.max(st, axis=0, keepdims=True))
        alpha = jnp.exp(m_old - m_new)
        p = jnp.exp(st - m_new)
        l_scr[...] = alpha * l_scr[...] + jnp.sum(p, axis=0, keepdims=True)
        acc_scr[...] = alpha * acc_scr[...] + _dot(vt_scr[kt], p.astype(BF16))
        m_scr[...] = m_new

    n_full = (i * tq) // kb

    def body(kt, carry):
        tile(kt, False)
        return carry
    lax.fori_loop(0, n_full, body, 0)
    tile(n_full, True)

    ot = acc_scr[...] / jnp.maximum(l_scr[...], 1e-30)
    for g in range(B_HEADS):
        o_ref[:, g * HEAD_DIM:(g + 1) * HEAD_DIM] = ot[:, g * tq:(g + 1) * tq].T


def selected_attention(proj, bias_t, *, tq=128, kb=256):
    bsz, t, _ = proj.shape
    nsp = bias_t.shape[1]
    wq = B_HEADS * HEAD_DIM
    assert kb % tq == 0 and t % kb == 0
    return pl.pallas_call(
        functools.partial(_selected_kernel, tq=tq, kb=kb),
        grid=(bsz, t // tq),
        in_specs=[pl.BlockSpec((None, tq, wq), lambda b, i: (b, i, COL_BQ // wq)),
                  pl.BlockSpec((None, nsp, tq), lambda b, i: (b, 0, i)),
                  pl.BlockSpec((None, t, HEAD_DIM), lambda b, i: (b, 0, COL_BKS // HEAD_DIM)),
                  pl.BlockSpec((None, t, HEAD_DIM), lambda b, i: (b, 0, COL_BVS // HEAD_DIM))],
        out_specs=pl.BlockSpec((None, tq, wq), lambda b, i: (b, i, 0)),
        out_shape=jax.ShapeDtypeStruct((bsz, t, wq), F32),
        scratch_shapes=[pltpu.VMEM((t, HEAD_DIM + nsp), BF16),
                        pltpu.VMEM((t // kb, HEAD_DIM, kb), BF16),
                        pltpu.VMEM((1, B_HEADS * tq), F32),
                        pltpu.VMEM((1, B_HEADS * tq), F32),
                        pltpu.VMEM((HEAD_DIM, B_HEADS * tq), F32)],
        compiler_params=_params(2), name="selected_attention")(proj, bias_t, proj, proj)


def _combine_kernel(g_ref, cmp_ref, sel_ref, win_ref, o_ref):
    g = jax.nn.sigmoid(g_ref[...])
    for h in range(B_HEADS):
        hs = slice(h * HEAD_DIM, (h + 1) * HEAD_DIM)
        o = (g[:, 3 * h:3 * h + 1] * cmp_ref[:, hs] + g[:, 3 * h + 1:3 * h + 2] * sel_ref[:, hs]
             + g[:, 3 * h + 2:3 * h + 3] * win_ref[:, hs])
        o_ref[:, hs] = o.astype(o_ref.dtype)


def combine_branches(proj, o_cmp, o_sel, o_win, *, tq=512):
    bsz, t, _ = proj.shape
    wq = B_HEADS * HEAD_DIM
    spec = pl.BlockSpec((None, tq, wq), lambda b, i: (b, i, 0))
    return pl.pallas_call(
        _combine_kernel, grid=(bsz, t // tq),
        in_specs=[pl.BlockSpec((None, tq, 128), lambda b, i: (b, i, COL_SMALL // 128)), spec, spec, spec],
        out_specs=spec, out_shape=jax.ShapeDtypeStruct((bsz, t, wq), BF16),
        compiler_params=_params(2), name="combine_branches")(proj, o_cmp, o_sel, o_win)


def _mlstm_kernel(gb_ref, qk_ref, v_ref, og_ref, small_ref, convw_ref, normw_ref, o_ref,
                  hist_scr, c_scr, n_scr, m_scr, *, chunk):
    L = chunk
    dqk, dv = C_QK_DIM, C_V_DIM

    @pl.when(pl.program_id(1) == 0)
    def _():
        hist_scr[...] = jnp.zeros(hist_scr.shape, F32)
        c_scr[...] = jnp.zeros(c_scr.shape, F32)
        n_scr[...] = jnp.zeros(n_scr.shape, F32)
        m_scr[...] = jnp.zeros(m_scr.shape, F32)

    x = qk_ref[...]
    xe = jnp.concatenate([hist_scr[...], x], axis=0)
    w = convw_ref[...]
    y = w[0:1] * xe[8 - 3:8 - 3 + L]
    for tap in range(1, C_CONV):
        y = y + w[tap:tap + 1] * xe[8 - 3 + tap:8 - 3 + tap + L]
    hist_scr[...] = x[L - 8:L]
    qk = _silu(y)

    lane = lax.broadcasted_iota(jnp.int32, (1, 128), 1)
    gbias = jnp.zeros((1, 128), F32)
    for idx in range(2 * C_HEADS):
        gbias = jnp.where(lane == LANE_CI + idx, gb_ref[idx], gbias)
    pre = small_ref[...] + gbias
    logf = jnp.minimum(pre, 0.0) - jnp.log1p(jnp.exp(-jnp.abs(pre)))
    row = lax.broadcasted_iota(jnp.int32, (L, L), 0)
    col = lax.broadcasted_iota(jnp.int32, (L, L), 1)
    causal = row >= col
    tri = jnp.where(causal, 1.0, 0.0)
    hp = lax.Precision.HIGHEST
    b_cols = jnp.dot(tri, logf, precision=hp, preferred_element_type=F32)
    b_rows = lax.dot_general(logf.T, tri, (((1,), (1,)), ((), ())), precision=hp,
                             preferred_element_type=F32)
    pre_t = pre.T

    for h in range(C_HEADS):
        q = qk[:, h * dqk:(h + 1) * dqk]
        k = qk[:, C_HEADS * dqk + h * dqk:C_HEADS * dqk + (h + 1) * dqk] * (dqk ** -0.5)
        v = v_ref[:, h * dv:(h + 1) * dv].astype(BF16)
        b_col = b_cols[:, LANE_CF + h:LANE_CF + h + 1]
        i_col = pre[:, LANE_CI + h:LANE_CI + h + 1]
        b_row = b_rows[LANE_CF + h:LANE_CF + h + 1, :]
        i_row = pre_t[LANE_CI + h:LANE_CI + h + 1, :]
        m_prev = m_scr[h:h + 1, 0:1]
        ct = c_scr[h]
        n_row = n_scr[h:h + 1, :]

        d = jnp.where(causal, b_col - b_row + i_row, NEG)
        m_inter = b_col + m_prev
        m_t = jnp.maximum(m_inter, jnp.max(d, axis=-1, keepdims=True))
        qb = q.astype(BF16)
        s = _dot_nt(qb, k.astype(BF16)) * jnp.exp(d - m_t)
        inter = jnp.exp(m_inter - m_t)
        num = _dot(s.astype(BF16), v) + inter * _dot(qb, ct.astype(BF16))
        den = jnp.sum(s, axis=-1, keepdims=True) + inter * jnp.sum(q * n_row, axis=-1, keepdims=True)
        hh = num / jnp.maximum(jnp.abs(den), jnp.exp(-m_t))

        b_last = b_col[L - 1:L, :]
        g = b_last - b_col + i_col
        m_new = jnp.maximum(b_last + m_prev, jnp.max(g, axis=0, keepdims=True))
        kd = k * jnp.exp(g - m_new)
        keep = jnp.exp(b_last + m_prev - m_new)
        c_scr[h] = keep * ct + _dot(kd.T.astype(BF16), v)
        n_scr[h:h + 1, :] = keep * n_row + jnp.sum(kd, axis=0, keepdims=True)
        m_scr[h:h + 1, :] = jnp.broadcast_to(m_new, (1, 128))

        vs = slice(h * dv, (h + 1) * dv)
        hn = hh * lax.rsqrt(jnp.mean(hh * hh, axis=-1, keepdims=True) + EPS) * normw_ref[:, vs]
        o_ref[:, vs] = (hn * jax.nn.sigmoid(og_ref[:, vs])).astype(o_ref.dtype)


def mlstm_mixer(proj, gate_b, conv_w, norm_w, layer, *, chunk=128):
    bsz, t, _ = proj.shape
    wqk, wv = 2 * C_HEADS * C_QK_DIM, C_HEADS * C_V_DIM
    assert wqk == wv == 1024 and t % chunk == 0
    blk = lambda col: pl.BlockSpec((None, chunk, 1024), lambda b, i: (b, i, col // 1024))
    return pl.pallas_call(
        functools.partial(_mlstm_kernel, chunk=chunk),
        grid=(bsz, t // chunk),
        in_specs=[pl.BlockSpec(memory_space=pltpu.SMEM),
                  blk(COL_CQK), blk(COL_CV), blk(COL_CO),
                  pl.BlockSpec((None, chunk, 128), lambda b, i: (b, i, COL_SMALL // 128)),
                  pl.BlockSpec((None, C_CONV, wqk), lambda b, i: (layer, 0, 0)),
                  pl.BlockSpec((None, 1, wv), lambda b, i: (layer, 0, 0))],
        out_specs=pl.BlockSpec((None, chunk, wv), lambda b, i: (b, i, 0)),
        out_shape=jax.ShapeDtypeStruct((bsz, t, wv), BF16),
        scratch_shapes=[pltpu.VMEM((8, wqk), F32),
                        pltpu.VMEM((C_HEADS, C_QK_DIM, C_V_DIM), F32),
                        pltpu.VMEM((8, C_QK_DIM), F32),
                        pltpu.VMEM((8, 128), F32)],
        compiler_params=_params(2), name="mlstm_mixer")(
            gate_b[layer], proj, proj, proj, proj, conv_w, norm_w)


def _outproj_kernel(oa_ref, ob_ref, oc_ref, w_ref, x_ref, mod_ref, o_ref, *, gate_row):
    wa, wb = oa_ref.shape[1], ob_ref.shape[1]
    acc = _dot(oa_ref[...], w_ref[0:wa, :])
    acc = acc + _dot(ob_ref[...], w_ref[wa:wa + wb, :])
    acc = acc + _dot(oc_ref[...], w_ref[wa + wb:, :])
    o_ref[...] = x_ref[...] + mod_ref[gate_row:gate_row + 1, :] * acc


def out_proj_residual(o_a, o_b, o_c, w_out, x, mod, layer, seq, *, tm=1024, tn=512):
    n, d = x.shape
    wa, wb, wc = o_a.shape[1], o_b.shape[1], o_c.shape[1]
    return pl.pallas_call(
        functools.partial(_outproj_kernel, gate_row=2),
        grid=(n // tm, d // tn),
        in_specs=[pl.BlockSpec((tm, wa), lambda i, j: (i, 0)),
                  pl.BlockSpec((tm, wb), lambda i, j: (i, 0)),
                  pl.BlockSpec((tm, wc), lambda i, j: (i, 0)),
                  pl.BlockSpec((None, wa + wb + wc, tn), lambda i, j: (layer, 0, j)),
                  pl.BlockSpec((tm, tn), lambda i, j: (i, j)),
                  pl.BlockSpec((None, None, 6, tn), lambda i, j: (layer, (i * tm) // seq, 0, j))],
        out_specs=pl.BlockSpec((tm, tn), lambda i, j: (i, j)),
        out_shape=jax.ShapeDtypeStruct((n, d), F32),
        compiler_params=_params(2), name="out_proj_residual")(o_a, o_b, o_c, w_out, x, mod)


def _ffn_kernel(x_ref, nw_ref, mod_ref, w1_ref, w3_ref, w2_ref, o_ref, h_scr, acc_scr):
    f = pl.program_id(1)

    @pl.when(f == 0)
    def _():
        h = _norm_mod(x_ref[...], nw_ref[...], mod_ref[3:4, :], mod_ref[4:5, :])
        h_scr[...] = h.astype(BF16)
        acc_scr[...] = jnp.zeros(acc_scr.shape, F32)

    h = h_scr[...]
    g = _silu(_dot(h, w1_ref[...])) * _dot(h, w3_ref[...])
    acc_scr[...] += _dot(g.astype(BF16), w2_ref[...])

    @pl.when(f == pl.num_programs(1) - 1)
    def _():
        o_ref[...] = x_ref[...] + mod_ref[5:6, :] * acc_scr[...]


def ffn_residual(x, norm_w, mod, w1, w3, w2, layer, idx, seq, *, tm=512, tf=512):
    n, d = x.shape
    dff = w1.shape[-1]
    return pl.pallas_call(
        _ffn_kernel, grid=(n // tm, dff // tf),
        in_specs=[pl.BlockSpec((tm, d), lambda i, f: (i, 0)),
                  pl.BlockSpec((None, 1, d), lambda i, f: (layer, 0, 0)),
                  pl.BlockSpec((None, None, 6, d), lambda i, f: (layer, (i * tm) // seq, 0, 0)),
                  pl.BlockSpec((None, d, tf), lambda i, f: (idx, 0, f)),
                  pl.BlockSpec((None, d, tf), lambda i, f: (idx, 0, f)),
                  pl.BlockSpec((None, tf, d), lambda i, f: (idx, f, 0))],
        out_specs=pl.BlockSpec((tm, d), lambda i, f: (i, 0)),
        out_shape=jax.ShapeDtypeStruct((n, d), F32),
        scratch_shapes=[pltpu.VMEM((tm, d), BF16), pltpu.VMEM((tm, d), F32)],
        compiler_params=_params(2), name="ffn_residual")(x, norm_w, mod, w1, w3, w2)


def _moe_kernel(x_ref, nw_ref, mod_ref, r_ref, w1_ref, w3_ref, w2_ref, o_ref, h_scr, comb_scr, acc_scr):
    e = pl.program_id(1)
    f = pl.program_id(2)
    tm = x_ref.shape[0]
    lane = lax.broadcasted_iota(jnp.int32, (tm, 128), 1)

    @pl.when((e == 0) & (f == 0))
    def _():
        h = _norm_mod(x_ref[...], nw_ref[...], mod_ref[3:4, :], mod_ref[4:5, :])
        h_scr[...] = h.astype(BF16)
        acc_scr[...] = jnp.zeros(acc_scr.shape, F32)
        logits = jnp.dot(h, r_ref[...], precision=lax.Precision.HIGHEST, preferred_element_type=F32)
        logits = jnp.where(lane < N_EXPERTS, logits, NEG)
        v1 = jnp.max(logits, axis=-1, keepdims=True)
        i1 = jnp.min(jnp.where(logits == v1, lane, 128), axis=-1, keepdims=True)
        rest = jnp.where(lane == i1, NEG, logits)
        v2 = jnp.max(rest, axis=-1, keepdims=True)
        i2 = jnp.min(jnp.where(rest == v2, lane, 128), axis=-1, keepdims=True)
        e2 = jnp.exp(v2 - v1)
        comb_scr[...] = (jnp.where(lane == i1, 1.0 / (1.0 + e2), 0.0)
                         + jnp.where(lane == i2, e2 / (1.0 + e2), 0.0))

    h = h_scr[...]
    ce = jnp.sum(jnp.where(lane == e, comb_scr[...], 0.0), axis=-1, keepdims=True)
    g = _silu(_dot(h, w1_ref[...])) * _dot(h, w3_ref[...]) * ce
    acc_scr[...] += _dot(g.astype(BF16), w2_ref[...])

    @pl.when((e == pl.num_programs(1) - 1) & (f == pl.num_programs(2) - 1))
    def _():
        o_ref[...] = x_ref[...] + mod_ref[5:6, :] * acc_scr[...]


def moe_residual(x, norm_w, mod, router, w1, w3, w2, layer, idx, seq, *, tm=512, tf=256):
    n, d = x.shape
    n_exp, _, dff = w1.shape[1:]
    return pl.pallas_call(
        _moe_kernel, grid=(n // tm, n_exp, dff // tf),
        in_specs=[pl.BlockSpec((tm, d), lambda i, e, f: (i, 0)),
                  pl.BlockSpec((None, 1, d), lambda i, e, f: (layer, 0, 0)),
                  pl.BlockSpec((None, None, 6, d), lambda i, e, f: (layer, (i * tm) // seq, 0, 0)),
                  pl.BlockSpec((None, d, 128), lambda i, e, f: (idx, 0, 0)),
                  pl.BlockSpec((None, None, d, tf), lambda i, e, f: (idx, e, 0, f)),
                  pl.BlockSpec((None, None, d, tf), lambda i, e, f: (idx, e, 0, f)),
                  pl.BlockSpec((None, None, tf, d), lambda i, e, f: (idx, e, f, 0))],
        out_specs=pl.BlockSpec((tm, d), lambda i, e, f: (i, 0)),
        out_shape=jax.ShapeDtypeStruct((n, d), F32),
        scratch_shapes=[pltpu.VMEM((tm, d), BF16), pltpu.VMEM((tm, 128), F32), pltpu.VMEM((tm, d), F32)],
        compiler_params=_params(3), name="moe_residual")(x, norm_w, mod, router, w1, w3, w2)


def _final_norm_kernel(x_ref, w_ref, o_ref):
    x = x_ref[...]
    o_ref[...] = x * lax.rsqrt(jnp.mean(x * x, axis=-1, keepdims=True) + EPS) * w_ref[...]


def final_norm(x, w, *, tm=512):
    n, d = x.shape
    return pl.pallas_call(
        _final_norm_kernel, grid=(n // tm,),
        in_specs=[pl.BlockSpec((tm, d), lambda i: (i, 0)), pl.BlockSpec((1, d), lambda i: (0, 0))],
        out_specs=pl.BlockSpec((tm, d), lambda i: (i, 0)),
        out_shape=jax.ShapeDtypeStruct((n, d), F32),
        compiler_params=_params(1), name="final_norm")(x, w.reshape(1, d))


def _reorder_w_in(w_in):
    sizes = (512, 256, 256, 512, 128, 128, 128, 128, 128, 128, 12, 512, 512, 1024, 4, 4, 1024)
    (aq, ak, av, bq, bkc, bvc, bks, bvs, bkw, bvw, bg, cq, ck, cv, ci, cf, co) = jnp.split(
        w_in, np.cumsum(sizes)[:-1].tolist(), axis=-1)
    pad = lambda n: jnp.zeros(w_in.shape[:-1] + (n,), w_in.dtype)
    out = jnp.concatenate([aq, ak, av, cq, ck, cv, co, bq, bkc, bvc, bks, bvs, bkw, bvw,
                           bg, ci, cf, pad(128 - 20), pad(PROJ_COLS - COL_SMALL - 128)], axis=-1)
    assert out.shape[-1] == PROJ_COLS
    return out.astype(BF16)


def hybrid_mixer(proj, tables, layer, a_sinks, mlstm_gate_b, nsa_pe_k, nsa_pe_v, nsa_ck_w1, nsa_ck_w2,
                 nsa_cv_w1, nsa_cv_w2, mlstm_conv_w, mlstm_norm_w):
    bsz, t, _ = proj.shape
    o_a = sliding_window_attention(proj, col_q=COL_AQ, col_k=COL_AK, col_v=COL_AV, n_heads=A_HEADS,
                                   n_kv=A_KV_HEADS, window=A_WINDOW, tq=256, tables=tables,
                                   sinks=a_sinks[layer])
    chunks = lambda col: proj[:, :, col:col + HEAD_DIM].reshape(bsz, t // CMP_STRIDE, CMP_STRIDE * HEAD_DIM)
    k_cmp = compress_blocks(chunks(COL_BKC), nsa_pe_k, nsa_ck_w1, nsa_ck_w2, layer)
    v_cmp = compress_blocks(chunks(COL_BVC), nsa_pe_v, nsa_cv_w1, nsa_cv_w2, layer)
    o_cmp, bias_t = compressed_attention_select(proj, k_cmp, v_cmp)
    o_sel = selected_attention(proj, bias_t)
    o_win = sliding_window_attention(proj, col_q=COL_BQ, col_k=COL_BKW, col_v=COL_BVW, n_heads=B_HEADS,
                                     n_kv=1, window=B_WINDOW, tq=512, out_dtype=F32)
    o_b = combine_branches(proj, o_cmp, o_sel, o_win)
    o_c = mlstm_mixer(proj, mlstm_gate_b, mlstm_conv_w, mlstm_norm_w, layer)
    return o_a, o_b, o_c


def kernel(x, c, positions, ada_w, ada_b, norm_mix_w, norm_ffn_w, w_in, mlstm_gate_b, a_sinks, nsa_pe_k,
           nsa_pe_v, nsa_ck_w1, nsa_ck_w2, nsa_cv_w1, nsa_cv_w2, mlstm_conv_w, mlstm_norm_w, w_out, ffn_w1,
           ffn_w3, ffn_w2, moe_router, moe_w1, moe_w3, moe_w2, final_norm_w):
    bsz, t, d = x.shape
    depth = ada_w.shape[0]
    mod = ada_modulation(c, ada_w, ada_b)
    tables = rope_tables(positions)
    w_in_p = _reorder_w_in(w_in)
    w_out_b = w_out.astype(BF16)
    ffn_b = [w.astype(BF16) for w in (ffn_w1, ffn_w3, ffn_w2)]
    moe_b = [w.astype(BF16) for w in (moe_w1, moe_w3, moe_w2)]
    router_p = jnp.pad(moe_router, ((0, 0), (0, 0), (0, 128 - N_EXPERTS)))
    norm_mix = norm_mix_w.reshape(depth, 1, d)
    norm_ffn = norm_ffn_w.reshape(depth, 1, d)
    conv_w = mlstm_conv_w
    norm_c = mlstm_norm_w.reshape(depth, 1, -1)

    xf = x.reshape(bsz * t, d)
    for layer in range(depth):
        proj = norm_proj(xf, norm_mix, mod, w_in_p, layer, t, shift_row=0, scale_row=1)
        o_a, o_b, o_c = hybrid_mixer(proj.reshape(bsz, t, PROJ_COLS), tables, layer, a_sinks, mlstm_gate_b,
                                     nsa_pe_k, nsa_pe_v, nsa_ck_w1, nsa_ck_w2, nsa_cv_w1, nsa_cv_w2,
                                     conv_w, norm_c)
        flat = lambda a: a.reshape(bsz * t, a.shape[-1])
        xf = out_proj_residual(flat(o_a), flat(o_b), flat(o_c), w_out_b, xf, mod, layer, t)
        if layer % 2 == 0:
            xf = ffn_residual(xf, norm_ffn, mod, *ffn_b, layer, layer // 2, t)
        else:
            xf = moe_residual(xf, norm_ffn, mod, router_p, *moe_b, layer, layer // 2, t)
    return final_norm(xf, final_norm_w).reshape(bsz, t, d)
```

```python
import functools
import math

import numpy as np
import jax
import jax.numpy as jnp
from jax import lax
from jax.experimental import pallas as pl
from jax.experimental.pallas import tpu as pltpu

F32 = jnp.float32
BF16 = jnp.bfloat16

HEAD_DIM = 128
A_HEADS, A_KV_HEADS, A_WINDOW = 4, 2, 128
ROPE_THETA = 150000.0
B_HEADS = 4
CMP_LEN, CMP_STRIDE = 32, 16
SEL_LEN, SEL_TOPK, B_WINDOW = 64, 16, 512
C_HEADS, C_QK_DIM, C_V_DIM, C_CONV = 4, 128, 256, 4
N_EXPERTS = 8
EPS = 1e-6
BIG = 1e9
NEG = -1e30
SCALE = HEAD_DIM ** -0.5

COL_AQ, COL_AK, COL_AV = 0, 512, 768
COL_CQK, COL_CV, COL_CO = 1024, 2048, 3072
COL_BQ = 4096
COL_BKC, COL_BVC, COL_BKS, COL_BVS, COL_BKW, COL_BVW = 4608, 4736, 4864, 4992, 5120, 5248
COL_SMALL = 5376
LANE_CI, LANE_CF = 12, 16
PROJ_COLS = 5632

VMEM_LIMIT_MB = 56


def _params(n_axes, vmem_mb=VMEM_LIMIT_MB):
    return pltpu.CompilerParams(dimension_semantics=("arbitrary",) * n_axes,
                                vmem_limit_bytes=vmem_mb * 2 ** 20)


def _dot(a, b):
    return jnp.dot(a, b, preferred_element_type=F32)


def _dot_nt(a, b):
    return lax.dot_general(a, b, (((1,), (1,)), ((), ())), preferred_element_type=F32)


def _silu(v):
    return v * jax.nn.sigmoid(v)


def _norm_mod(x, nw, shift, scale):
    ms = jnp.mean(x * x, axis=-1, keepdims=True)
    return (x * lax.rsqrt(ms + EPS) * nw) * (1.0 + scale) + shift


def _ada_kernel(c_ref, w_ref, b_ref, o_ref):
    act = _silu(c_ref[...]).astype(BF16)
    o_ref[...] = _dot(act, w_ref[...].astype(BF16)) + b_ref[...]


def ada_modulation(c, ada_w, ada_b):
    depth, d, n6 = ada_w.shape
    bsz = c.shape[0]
    assert bsz <= 8
    cp = jnp.zeros((8, d), F32).at[:bsz].set(c)
    tn = 512
    out = pl.pallas_call(
        _ada_kernel, grid=(depth, n6 // tn),
        in_specs=[pl.BlockSpec((8, d), lambda l, j: (0, 0)),
                  pl.BlockSpec((None, d, tn), lambda l, j: (l, 0, j)),
                  pl.BlockSpec((None, 1, tn), lambda l, j: (l, 0, j))],
        out_specs=pl.BlockSpec((None, 8, tn), lambda l, j: (l, 0, j)),
        out_shape=jax.ShapeDtypeStruct((depth, 8, n6), F32),
        compiler_params=_params(2), name="ada_modulation")(cp, ada_w, ada_b.reshape(depth, 1, n6))
    return out.reshape(depth, 8, 6, d)


def _rope_table_kernel(pos_ref, cos_ref, sin_ref):
    pos = pos_ref[...].astype(F32)
    lane = lax.broadcasted_iota(jnp.int32, (1, HEAD_DIM), 1)
    half = HEAD_DIM // 2
    inv = jnp.exp((lane & (half - 1)).astype(F32) * (-math.log(ROPE_THETA) * 2.0 / HEAD_DIM))
    ang = pos * inv
    cos_ref[...] = jnp.cos(ang)
    sin_ref[...] = jnp.where(lane < half, -1.0, 1.0) * jnp.sin(ang)


def rope_tables(positions):
    bsz, t = positions.shape
    tt = 512
    spec = pl.BlockSpec((None, tt, HEAD_DIM), lambda b, i: (b, i, 0))
    return pl.pallas_call(
        _rope_table_kernel, grid=(bsz, t // tt),
        in_specs=[pl.BlockSpec((None, tt, 1), lambda b, i: (b, i, 0))],
        out_specs=[spec, spec],
        out_shape=[jax.ShapeDtypeStruct((bsz, t, HEAD_DIM), F32)] * 2,
        compiler_params=_params(2), name="rope_tables")(positions.reshape(bsz, t, 1))


def _proj_kernel(x_ref, nw_ref, mod_ref, w_ref, o_ref, h_scr, *, shift_row, scale_row):
    @pl.when(pl.program_id(1) == 0)
    def _():
        h = _norm_mod(x_ref[...], nw_ref[...], mod_ref[shift_row:shift_row + 1, :],
                      mod_ref[scale_row:scale_row + 1, :])
        h_scr[...] = h.astype(BF16)

    o_ref[...] = _dot(h_scr[...], w_ref[...])


def norm_proj(x, norm_w, mod, w, layer, seq, *, shift_row, scale_row, tm=1024, tn=512):
    n, d = x.shape
    p = w.shape[-1]
    return pl.pallas_call(
        functools.partial(_proj_kernel, shift_row=shift_row, scale_row=scale_row),
        grid=(n // tm, p // tn),
        in_specs=[pl.BlockSpec((tm, d), lambda i, j: (i, 0)),
                  pl.BlockSpec((None, 1, d), lambda i, j: (layer, 0, 0)),
                  pl.BlockSpec((None, None, 6, d), lambda i, j: (layer, (i * tm) // seq, 0, 0)),
                  pl.BlockSpec((None, d, tn), lambda i, j: (layer, 0, j))],
        out_specs=pl.BlockSpec((tm, tn), lambda i, j: (i, j)),
        out_shape=jax.ShapeDtypeStruct((n, p), F32),
        scratch_shapes=[pltpu.VMEM((tm, d), BF16)],
        compiler_params=_params(2), name="norm_proj")(x, norm_w, mod, w)


def _swa_kernel(*refs, tq, window, n_heads, group, rope, sinks):
    it = iter(refs)
    q_ref, kp_ref, kc_ref, vp_ref, vc_ref = (next(it) for _ in range(5))
    if rope:
        cp_ref, sp_ref, cc_ref, sc_ref = (next(it) for _ in range(4))
    if sinks:
        sink_ref = next(it)
    o_ref = next(it)
    i = pl.program_id(1)
    row = lax.broadcasted_iota(jnp.int32, (tq, 2 * tq), 0)
    col = lax.broadcasted_iota(jnp.int32, (tq, 2 * tq), 1)
    diff = row + tq - col + jnp.where(col < tq, jnp.where(i > 0, 0, window), 0)
    mask = jnp.abs(2 * diff - (window - 1)) <= (window - 1)

    def rotate(v, c, s):
        return v * c + pltpu.roll(v, HEAD_DIM // 2, 1) * s

    kv_cache = {}
    for hq in range(n_heads):
        kv = hq // group
        if kv not in kv_cache:
            sl = slice(kv * HEAD_DIM, (kv + 1) * HEAD_DIM)
            kp, kc = kp_ref[:, sl], kc_ref[:, sl]
            if rope:
                kp = rotate(kp, cp_ref[...], sp_ref[...])
                kc = rotate(kc, cc_ref[...], sc_ref[...])
            k = jnp.concatenate([kp, kc], axis=0).astype(BF16)
            v = jnp.concatenate([vp_ref[:, sl], vc_ref[:, sl]], axis=0).astype(BF16)
            kv_cache[kv] = (k, v)
        k, v = kv_cache[kv]
        hs = slice(hq * HEAD_DIM, (hq + 1) * HEAD_DIM)
        q = q_ref[:, hs]
        if rope:
            q = rotate(q, cc_ref[...], sc_ref[...])
        s = _dot_nt((q * SCALE).astype(BF16), k)
        s = jnp.where(mask, s, NEG)
        m = jnp.max(s, axis=-1, keepdims=True)
        if sinks:
            m = jnp.maximum(m, sink_ref[hq])
        m = jnp.where(m < 0.5 * NEG, 0.0, m)
        p = jnp.exp(s - m)
        den = jnp.sum(p, axis=-1, keepdims=True)
        if sinks:
            den = den + jnp.exp(sink_ref[hq] - m)
        o = _dot(p.astype(BF16), v) / jnp.maximum(den, 1e-30)
        o_ref[:, hs] = o.astype(o_ref.dtype)


def sliding_window_attention(proj, *, col_q, col_k, col_v, n_heads, n_kv, window, tq,
                             tables=None, sinks=None, out_dtype=BF16):
    bsz, t, _ = proj.shape
    assert window - 1 <= tq and t % tq == 0
    wq, wkv = n_heads * HEAD_DIM, n_kv * HEAD_DIM
    assert col_q % wq == 0 and col_k % wkv == 0 and col_v % wkv == 0
    cur = lambda b, i: (b, i, 0)
    prev = lambda b, i: (b, jnp.maximum(i - 1, 0), 0)
    kvspec = lambda col, im: pl.BlockSpec(
        (None, tq, wkv), lambda b, i: (b, im(b, i)[1], col // wkv))
    in_specs = [pl.BlockSpec((None, tq, wq), lambda b, i: (b, i, col_q // wq)),
                kvspec(col_k, prev), kvspec(col_k, cur), kvspec(col_v, prev), kvspec(col_v, cur)]
    args = [proj] * 5
    if tables is not None:
        cos, sin = tables
        tp = pl.BlockSpec((None, tq, HEAD_DIM), prev)
        tc = pl.BlockSpec((None, tq, HEAD_DIM), cur)
        in_specs += [tp, tp, tc, tc]
        args += [cos, sin, cos, sin]
    if sinks is not None:
        in_specs.append(pl.BlockSpec(memory_space=pltpu.SMEM))
        args.append(sinks)
    return pl.pallas_call(
        functools.partial(_swa_kernel, tq=tq, window=window, n_heads=n_heads, group=n_heads // n_kv,
                          rope=tables is not None, sinks=sinks is not None),
        grid=(bsz, t // tq), in_specs=in_specs,
        out_specs=pl.BlockSpec((None, tq, wq), cur),
        out_shape=jax.ShapeDtypeStruct((bsz, t, wq), out_dtype),
        compiler_params=_params(2), name="sliding_window_attention")(*args)


def _compress_kernel(r_ref, pe_ref, w1_ref, w2_ref, o_ref):
    n16, half = r_ref.shape
    r = r_ref[...].astype(BF16)
    w1 = w1_ref[...].astype(BF16)
    first = _dot(r, w1[:half])
    second = _dot(r, w1[half:])
    pe = _dot(pe_ref[...].astype(BF16), w1)[0:1]
    pre = first + pltpu.roll(second, n16 - 1, 0) + pe
    hid = jax.nn.gelu(pre, approximate=True)
    o_ref[...] = _dot(hid.astype(BF16), w2_ref[...].astype(BF16))


def compress_blocks(chunks, pe, w1, w2, layer):
    bsz, n16, half = chunks.shape
    hid = w1.shape[-1]
    pe8 = jnp.broadcast_to(pe[layer].reshape(1, 2 * half), (8, 2 * half))
    return pl.pallas_call(
        _compress_kernel, grid=(bsz,),
        in_specs=[pl.BlockSpec((None, n16, half), lambda b: (b, 0, 0)),
                  pl.BlockSpec((8, 2 * half), lambda b: (0, 0)),
                  pl.BlockSpec((None, 2 * half, hid), lambda b: (layer, 0, 0)),
                  pl.BlockSpec((None, hid, HEAD_DIM), lambda b: (layer, 0, 0))],
        out_specs=pl.BlockSpec((None, n16, HEAD_DIM), lambda b: (b, 0, 0)),
        out_shape=jax.ShapeDtypeStruct((bsz, n16, HEAD_DIM), F32),
        compiler_params=_params(1), name="compress_blocks")(chunks, pe8, w1, w2)


def _heads_transposed(q, tq):
    n = q.shape[1] // HEAD_DIM
    return jnp.concatenate([(q[:, g * HEAD_DIM:(g + 1) * HEAD_DIM] * SCALE).T for g in range(n)],
                           axis=1).astype(BF16)


def _cmp_select_kernel(q_ref, kc_ref, vc_ref, ov_ref, ocmp_ref, bias_ref, *, tq, n_sel, topk):
    i = pl.program_id(1)
    nc = kc_ref.shape[0]
    hq = B_HEADS * tq
    qt = _heads_transposed(q_ref[...], tq)
    st = _dot(kc_ref[...].astype(BF16), qt)
    n_row = lax.broadcasted_iota(jnp.int32, (nc, hq), 0)
    t_col = i * tq + (lax.broadcasted_iota(jnp.int32, (nc, hq), 1) & (tq - 1))
    st = jnp.where(n_row * CMP_STRIDE + (CMP_LEN - 1) <= t_col, st, NEG)
    m = jnp.max(st, axis=0, keepdims=True)
    m = jnp.where(m < 0.5 * NEG, 0.0, m)
    p = jnp.exp(st - m)
    p = p / jnp.maximum(jnp.sum(p, axis=0, keepdims=True), 1e-30)
    pb = p.astype(BF16)
    ot = _dot(vc_ref[...].T.astype(BF16), pb)
    for g in range(B_HEADS):
        ocmp_ref[:, g * HEAD_DIM:(g + 1) * HEAD_DIM] = ot[:, g * tq:(g + 1) * tq].T
    imp4 = _dot(ov_ref[...], pb)
    imp = imp4[:, 0:tq]
    for g in range(1, B_HEADS):
        imp = imp + imp4[:, g * tq:(g + 1) * tq]
    nsp = imp.shape[0]
    j = lax.broadcasted_iota(jnp.int32, (nsp, tq), 0)
    t = i * tq + lax.broadcasted_iota(jnp.int32, (nsp, tq), 1)
    cur = t >> 6
    imp = jnp.where((j << 6) <= t, imp, -BIG)
    for forced in (0, cur, cur - 1):
        imp = jnp.where(j == forced, BIG, imp)
    imp = jnp.where(j < n_sel, imp, -3e38)
    sel = jnp.zeros((nsp, tq), F32)
    for _ in range(topk):
        mx = jnp.max(imp, axis=0, keepdims=True)
        idx = jnp.min(jnp.where(imp == mx, j, nsp), axis=0, keepdims=True)
        hit = j == idx
        sel = jnp.where(hit, 1.0, sel)
        imp = jnp.where(hit, -3e38, imp)
    bias_ref[...] = jnp.where(sel > 0.0, 0.0, jnp.where(j < n_sel, -BIG, 0.0))


def _overlap_matrix_t(t):
    n16, n_sel = t // CMP_STRIDE, t // SEL_LEN
    nsp = -(-n_sel // 128) * 128
    starts = np.arange(n16) * CMP_STRIDE
    sel_start = np.arange(n_sel) * SEL_LEN
    ov = np.clip(np.minimum(starts[:, None] + CMP_LEN, sel_start[None, :] + SEL_LEN)
                 - np.maximum(starts[:, None], sel_start[None, :]), 0, None).astype(np.float32) / CMP_LEN
    out = np.zeros((nsp, n16), np.float32)
    out[:n_sel] = ov.T
    return jnp.asarray(out, BF16)


def compressed_attention_select(proj, k_cmp, v_cmp, *, tq=128):
    bsz, t, _ = proj.shape
    n16, n_sel = t // CMP_STRIDE, t // SEL_LEN
    assert SEL_LEN == 64 and tq == 128
    ov = _overlap_matrix_t(t)
    nsp = ov.shape[0]
    wq = B_HEADS * HEAD_DIM
    return pl.pallas_call(
        functools.partial(_cmp_select_kernel, tq=tq, n_sel=n_sel, topk=min(SEL_TOPK, n_sel)),
        grid=(bsz, t // tq),
        in_specs=[pl.BlockSpec((None, tq, wq), lambda b, i: (b, i, COL_BQ // wq)),
                  pl.BlockSpec((None, n16, HEAD_DIM), lambda b, i: (b, 0, 0)),
                  pl.BlockSpec((None, n16, HEAD_DIM), lambda b, i: (b, 0, 0)),
                  pl.BlockSpec((nsp, n16), lambda b, i: (0, 0))],
        out_specs=[pl.BlockSpec((None, tq, wq), lambda b, i: (b, i, 0)),
                   pl.BlockSpec((None, nsp, tq), lambda b, i: (b, 0, i))],
        out_shape=[jax.ShapeDtypeStruct((bsz, t, wq), F32),
                   jax.ShapeDtypeStruct((bsz, nsp, t), F32)],
        compiler_params=_params(2), name="compressed_attention_select")(proj, k_cmp, v_cmp, ov)


def _selected_kernel(q_ref, bias_ref, ks_ref, vs_ref, o_ref, kaug_scr, vt_scr, qa_scr, s0_scr, s1_scr,
                     m_scr, l_scr, acc_scr, *, tq, kb):
    i = pl.program_id(1)
    t_all = ks_ref.shape[0]
    nsp = bias_ref.shape[0]
    hq = B_HEADS * tq

    @pl.when(i == 0)
    def _():
        def prep(c, carry):
            r0 = pl.multiple_of(c * kb, kb)
            kaug_scr[pl.ds(r0, kb), 0:HEAD_DIM] = ks_ref[pl.ds(r0, kb), :].astype(BF16)
            key = r0 + lax.broadcasted_iota(jnp.int32, (kb, nsp), 0)
            blk = lax.broadcasted_iota(jnp.int32, (kb, nsp), 1)
            kaug_scr[pl.ds(r0, kb), HEAD_DIM:] = jnp.where((key >> 6) == blk, 1.0, 0.0).astype(BF16)
            vt_scr[c] = vs_ref[pl.ds(r0, kb), :].T.astype(BF16)
            return carry
        lax.fori_loop(0, t_all // kb, prep, 0)

    bias = bias_ref[...]
    qa_scr[...] = jnp.concatenate([_heads_transposed(q_ref[...], tq),
                                   jnp.concatenate([bias] * B_HEADS, axis=1).astype(BF16)], axis=0)
    m_scr[...] = jnp.full(m_scr.shape, NEG, F32)
    l_scr[...] = jnp.zeros(l_scr.shape, F32)
    acc_scr[...] = jnp.zeros(acc_scr.shape, F32)

    def scores(kt, s_out):
        r0 = pl.multiple_of(kt * kb, kb)
        s_out[...] = _dot(kaug_scr[pl.ds(r0, kb), :], qa_scr[...])

    def consume(kt, s_in, causal):
        st = s_in[...]
        if causal:
            key = kt * kb + lax.broadcasted_iota(jnp.int32, (kb, hq), 0)
            tpos = i * tq + (lax.broadcasted_iota(jnp.int32, (kb, hq), 1) & (tq - 1))
            st = jnp.where(key <= tpos, st, NEG)
        m_old = m_scr[...]
        m_new = jnp.maximum(m_old, jnp.max(st, axis=0, keepdims=True))
        alpha = jnp.exp(m_old - m_new)
        p = jnp.exp(st - m_new)
        l_scr[...] = alpha * l_scr[...] + jnp.sum(p, axis=0, keepdims=True)
        acc_scr[...] = alpha * acc_scr[...] + _dot(vt_scr[kt], p.astype(BF16))
        m_scr[...] = m_new

    def step(kt, s_in, s_out):
        scores(kt + 1, s_out)
        consume(kt, s_in, False)

    def finish(kt, s_in):
        consume(kt, s_in, True)
        ot = acc_scr[...] / jnp.maximum(l_scr[...], 1e-30)
        for g in range(B_HEADS):
            o_ref[:, g * HEAD_DIM:(g + 1) * HEAD_DIM] = ot[:, g * tq:(g + 1) * tq].T

    n_full = (i * tq) // kb
    scores(0, s0_scr)

    def pair(j, carry):
        step(2 * j, s0_scr, s1_scr)
        step(2 * j + 1, s1_scr, s0_scr)
        return carry
    lax.fori_loop(0, n_full // 2, pair, 0)

    @pl.when(n_full % 2 == 1)
    def _():
        step(n_full - 1, s0_scr, s1_scr)
        finish(n_full, s1_scr)

    @pl.when(n_full % 2 == 0)
    def _():
        finish(n_full, s0_scr)


def selected_attention(proj, bias_t, *, tq=128, kb=512):
    bsz, t, _ = proj.shape
    nsp = bias_t.shape[1]
    wq = B_HEADS * HEAD_DIM
    assert kb % tq == 0 and t % kb == 0
    return pl.pallas_call(
        functools.partial(_selected_kernel, tq=tq, kb=kb),
        grid=(bsz, t // tq),
        in_specs=[pl.BlockSpec((None, tq, wq), lambda b, i: (b, i, COL_BQ // wq)),
                  pl.BlockSpec((None, nsp, tq), lambda b, i: (b, 0, i)),
                  pl.BlockSpec((None, t, HEAD_DIM), lambda b, i: (b, 0, COL_BKS // HEAD_DIM)),
                  pl.BlockSpec((None, t, HEAD_DIM), lambda b, i: (b, 0, COL_BVS // HEAD_DIM))],
        out_specs=pl.BlockSpec((None, tq, wq), lambda b, i: (b, i, 0)),
        out_shape=jax.ShapeDtypeStruct((bsz, t, wq), F32),
        scratch_shapes=[pltpu.VMEM((t, HEAD_DIM + nsp), BF16),
                        pltpu.VMEM((t // kb, HEAD_DIM, kb), BF16),
                        pltpu.VMEM((HEAD_DIM + nsp, B_HEADS * tq), BF16),
                        pltpu.VMEM((kb, B_HEADS * tq), F32),
                        pltpu.VMEM((kb, B_HEADS * tq), F32),
                        pltpu.VMEM((1, B_HEADS * tq), F32),
                        pltpu.VMEM((1, B_HEADS * tq), F32),
                        pltpu.VMEM((HEAD_DIM, B_HEADS * tq), F32)],
        compiler_params=_params(2), name="selected_attention")(proj, bias_t, proj, proj)


def _combine_kernel(g_ref, cmp_ref, sel_ref, win_ref, o_ref):
    g = jax.nn.sigmoid(g_ref[...])
    for h in range(B_HEADS):
        hs = slice(h * HEAD_DIM, (h + 1) * HEAD_DIM)
        o = (g[:, 3 * h:3 * h + 1] * cmp_ref[:, hs] + g[:, 3 * h + 1:3 * h + 2] * sel_ref[:, hs]
             + g[:, 3 * h + 2:3 * h + 3] * win_ref[:, hs])
        o_ref[:, hs] = o.astype(o_ref.dtype)


def combine_branches(proj, o_cmp, o_sel, o_win, *, tq=512):
    bsz, t, _ = proj.shape
    wq = B_HEADS * HEAD_DIM
    spec = pl.BlockSpec((None, tq, wq), lambda b, i: (b, i, 0))
    return pl.pallas_call(
        _combine_kernel, grid=(bsz, t // tq),
        in_specs=[pl.BlockSpec((None, tq, 128), lambda b, i: (b, i, COL_SMALL // 128)), spec, spec, spec],
        out_specs=spec, out_shape=jax.ShapeDtypeStruct((bsz, t, wq), BF16),
        compiler_params=_params(2), name="combine_branches")(proj, o_cmp, o_sel, o_win)


def _mlstm_kernel(gb_ref, qk_ref, v_ref, og_ref, small_ref, convw_ref, normw_ref, o_ref,
                  hist_scr, c_scr, n_scr, m_scr, *, chunk):
    L = chunk
    dqk, dv = C_QK_DIM, C_V_DIM

    @pl.when(pl.program_id(1) == 0)
    def _():
        hist_scr[...] = jnp.zeros(hist_scr.shape, F32)
        c_scr[...] = jnp.zeros(c_scr.shape, F32)
        n_scr[...] = jnp.zeros(n_scr.shape, F32)
        m_scr[...] = jnp.zeros(m_scr.shape, F32)

    x = qk_ref[...]
    xe = jnp.concatenate([hist_scr[...], x], axis=0)
    w = convw_ref[...]
    y = w[0:1] * xe[8 - 3:8 - 3 + L]
    for tap in range(1, C_CONV):
        y = y + w[tap:tap + 1] * xe[8 - 3 + tap:8 - 3 + tap + L]
    hist_scr[...] = x[L - 8:L]
    qk = _silu(y)

    lane = lax.broadcasted_iota(jnp.int32, (1, 128), 1)
    gbias = jnp.zeros((1, 128), F32)
    for idx in range(2 * C_HEADS):
        gbias = jnp.where(lane == LANE_CI + idx, gb_ref[idx], gbias)
    pre = small_ref[...] + gbias
    logf = jnp.minimum(pre, 0.0) - jnp.log1p(jnp.exp(-jnp.abs(pre)))
    row = lax.broadcasted_iota(jnp.int32, (L, L), 0)
    col = lax.broadcasted_iota(jnp.int32, (L, L), 1)
    causal = row >= col
    tri = jnp.where(causal, 1.0, 0.0)
    hp = lax.Precision.HIGHEST
    b_cols = jnp.dot(tri, logf, precision=hp, preferred_element_type=F32)
    b_rows = lax.dot_general(logf.T, tri, (((1,), (1,)), ((), ())), precision=hp,
                             preferred_element_type=F32)
    pre_t = pre.T

    for h in range(C_HEADS):
        q = qk[:, h * dqk:(h + 1) * dqk]
        k = qk[:, C_HEADS * dqk + h * dqk:C_HEADS * dqk + (h + 1) * dqk] * (dqk ** -0.5)
        v = v_ref[:, h * dv:(h + 1) * dv].astype(BF16)
        b_col = b_cols[:, LANE_CF + h:LANE_CF + h + 1]
        i_col = pre[:, LANE_CI + h:LANE_CI + h + 1]
        b_row = b_rows[LANE_CF + h:LANE_CF + h + 1, :]
        i_row = pre_t[LANE_CI + h:LANE_CI + h + 1, :]
        m_prev = m_scr[h:h + 1, 0:1]
        ct = c_scr[h]
        n_row = n_scr[h:h + 1, :]

        d = jnp.where(causal, b_col - b_row + i_row, NEG)
        m_inter = b_col + m_prev
        m_t = jnp.maximum(m_inter, jnp.max(d, axis=-1, keepdims=True))
        qb = q.astype(BF16)
        s = _dot_nt(qb, k.astype(BF16)) * jnp.exp(d - m_t)
        inter = jnp.exp(m_inter - m_t)
        num = _dot(s.astype(BF16), v) + inter * _dot(qb, ct.astype(BF16))
        den = jnp.sum(s, axis=-1, keepdims=True) + inter * jnp.sum(q * n_row, axis=-1, keepdims=True)
        hh = num / jnp.maximum(jnp.abs(den), jnp.exp(-m_t))

        b_last = b_col[L - 1:L, :]
        g = b_last - b_col + i_col
        m_new = jnp.maximum(b_last + m_prev, jnp.max(g, axis=0, keepdims=True))
        kd = k * jnp.exp(g - m_new)
        keep = jnp.exp(b_last + m_prev - m_new)
        c_scr[h] = keep * ct + _dot(kd.T.astype(BF16), v)
        n_scr[h:h + 1, :] = keep * n_row + jnp.sum(kd, axis=0, keepdims=True)
        m_scr[h:h + 1, :] = jnp.broadcast_to(m_new, (1, 128))

        vs = slice(h * dv, (h + 1) * dv)
        hn = hh * lax.rsqrt(jnp.mean(hh * hh, axis=-1, keepdims=True) + EPS) * normw_ref[:, vs]
        o_ref[:, vs] = (hn * jax.nn.sigmoid(og_ref[:, vs])).astype(o_ref.dtype)


def mlstm_mixer(proj, gate_b, conv_w, norm_w, layer, *, chunk=128):
    bsz, t, _ = proj.shape
    wqk, wv = 2 * C_HEADS * C_QK_DIM, C_HEADS * C_V_DIM
    assert wqk == wv == 1024 and t % chunk == 0
    blk = lambda col: pl.BlockSpec((None, chunk, 1024), lambda b, i: (b, i, col // 1024))
    return pl.pallas_call(
        functools.partial(_mlstm_kernel, chunk=chunk),
        grid=(bsz, t // chunk),
        in_specs=[pl.BlockSpec(memory_space=pltpu.SMEM),
                  blk(COL_CQK), blk(COL_CV), blk(COL_CO),
                  pl.BlockSpec((None, chunk, 128), lambda b, i: (b, i, COL_SMALL // 128)),
                  pl.BlockSpec((None, C_CONV, wqk), lambda b, i: (layer, 0, 0)),
                  pl.BlockSpec((None, 1, wv), lambda b, i: (layer, 0, 0))],
        out_specs=pl.BlockSpec((None, chunk, wv), lambda b, i: (b, i, 0)),
        out_shape=jax.ShapeDtypeStruct((bsz, t, wv), BF16),
        scratch_shapes=[pltpu.VMEM((8, wqk), F32),
                        pltpu.VMEM((C_HEADS, C_QK_DIM, C_V_DIM), F32),
                        pltpu.VMEM((8, C_QK_DIM), F32),
                        pltpu.VMEM((8, 128), F32)],
        compiler_params=_params(2), name="mlstm_mixer")(
            gate_b[layer], proj, proj, proj, proj, conv_w, norm_w)


def _outproj_kernel(oa_ref, ob_ref, oc_ref, w_ref, x_ref, mod_ref, o_ref, *, gate_row):
    wa, wb = oa_ref.shape[1], ob_ref.shape[1]
    acc = _dot(oa_ref[...], w_ref[0:wa, :])
    acc = acc + _dot(ob_ref[...], w_ref[wa:wa + wb, :])
    acc = acc + _dot(oc_ref[...], w_ref[wa + wb:, :])
    o_ref[...] = x_ref[...] + mod_ref[gate_row:gate_row + 1, :] * acc


def out_proj_residual(o_a, o_b, o_c, w_out, x, mod, layer, seq, *, tm=1024, tn=512):
    n, d = x.shape
    wa, wb, wc = o_a.shape[1], o_b.shape[1], o_c.shape[1]
    return pl.pallas_call(
        functools.partial(_outproj_kernel, gate_row=2),
        grid=(n // tm, d // tn),
        in_specs=[pl.BlockSpec((tm, wa), lambda i, j: (i, 0)),
                  pl.BlockSpec((tm, wb), lambda i, j: (i, 0)),
                  pl.BlockSpec((tm, wc), lambda i, j: (i, 0)),
                  pl.BlockSpec((None, wa + wb + wc, tn), lambda i, j: (layer, 0, j)),
                  pl.BlockSpec((tm, tn), lambda i, j: (i, j)),
                  pl.BlockSpec((None, None, 6, tn), lambda i, j: (layer, (i * tm) // seq, 0, j))],
        out_specs=pl.BlockSpec((tm, tn), lambda i, j: (i, j)),
        out_shape=jax.ShapeDtypeStruct((n, d), F32),
        compiler_params=_params(2), name="out_proj_residual")(o_a, o_b, o_c, w_out, x, mod)


def _ffn_kernel(x_ref, nw_ref, mod_ref, w1_ref, w3_ref, w2_ref, o_ref, h_scr, acc_scr):
    f = pl.program_id(1)

    @pl.when(f == 0)
    def _():
        h = _norm_mod(x_ref[...], nw_ref[...], mod_ref[3:4, :], mod_ref[4:5, :])
        h_scr[...] = h.astype(BF16)
        acc_scr[...] = jnp.zeros(acc_scr.shape, F32)

    h = h_scr[...]
    g = _silu(_dot(h, w1_ref[...])) * _dot(h, w3_ref[...])
    acc_scr[...] += _dot(g.astype(BF16), w2_ref[...])

    @pl.when(f == pl.num_programs(1) - 1)
    def _():
        o_ref[...] = x_ref[...] + mod_ref[5:6, :] * acc_scr[...]


def ffn_residual(x, norm_w, mod, w1, w3, w2, layer, idx, seq, *, tm=512, tf=512):
    n, d = x.shape
    dff = w1.shape[-1]
    return pl.pallas_call(
        _ffn_kernel, grid=(n // tm, dff // tf),
        in_specs=[pl.BlockSpec((tm, d), lambda i, f: (i, 0)),
                  pl.BlockSpec((None, 1, d), lambda i, f: (layer, 0, 0)),
                  pl.BlockSpec((None, None, 6, d), lambda i, f: (layer, (i * tm) // seq, 0, 0)),
                  pl.BlockSpec((None, d, tf), lambda i, f: (idx, 0, f)),
                  pl.BlockSpec((None, d, tf), lambda i, f: (idx, 0, f)),
                  pl.BlockSpec((None, tf, d), lambda i, f: (idx, f, 0))],
        out_specs=pl.BlockSpec((tm, d), lambda i, f: (i, 0)),
        out_shape=jax.ShapeDtypeStruct((n, d), F32),
        scratch_shapes=[pltpu.VMEM((tm, d), BF16), pltpu.VMEM((tm, d), F32)],
        compiler_params=_params(2), name="ffn_residual")(x, norm_w, mod, w1, w3, w2)


MOE_TILE = 512


def _router_kernel(x_ref, nw_ref, mod_ref, r_ref, h_ref, route_ref):
    tm = x_ref.shape[0]
    lane = lax.broadcasted_iota(jnp.int32, (tm, 128), 1)
    h = _norm_mod(x_ref[...], nw_ref[...], mod_ref[3:4, :], mod_ref[4:5, :])
    h_ref[...] = h
    logits = jnp.dot(h, r_ref[...], precision=lax.Precision.HIGHEST, preferred_element_type=F32)
    logits = jnp.where(lane < N_EXPERTS, logits, NEG)
    v1 = jnp.max(logits, axis=-1, keepdims=True)
    i1 = jnp.min(jnp.where(logits == v1, lane, 128), axis=-1, keepdims=True)
    rest = jnp.where(lane == i1, NEG, logits)
    v2 = jnp.max(rest, axis=-1, keepdims=True)
    i2 = jnp.min(jnp.where(rest == v2, lane, 128), axis=-1, keepdims=True)
    e2 = jnp.exp(v2 - v1)
    route = jnp.where(lane == 0, i1.astype(F32), jnp.where(lane == 1, i2.astype(F32), 0.0))
    route = jnp.where(lane == 2, 1.0 / (1.0 + e2), jnp.where(lane == 3, e2 / (1.0 + e2), route))
    route_ref[...] = route


def moe_router(x, norm_w, mod, router, layer, idx, seq, *, tm=512):
    n, d = x.shape
    return pl.pallas_call(
        _router_kernel, grid=(n // tm,),
        in_specs=[pl.BlockSpec((tm, d), lambda i: (i, 0)),
                  pl.BlockSpec((None, 1, d), lambda i: (layer, 0, 0)),
                  pl.BlockSpec((None, None, 6, d), lambda i: (layer, (i * tm) // seq, 0, 0)),
                  pl.BlockSpec((None, d, 128), lambda i: (idx, 0, 0))],
        out_specs=[pl.BlockSpec((tm, d), lambda i: (i, 0)), pl.BlockSpec((tm, 128), lambda i: (i, 0))],
        out_shape=[jax.ShapeDtypeStruct((n, d), F32), jax.ShapeDtypeStruct((n, 128), F32)],
        compiler_params=_params(1), name="moe_router")(x, norm_w, mod, router)


def _moe_plan(route, n_exp, tile):
    n = route.shape[0]
    ef = route[:, :2].astype(jnp.int32).reshape(-1)
    onehot = (ef[:, None] == jnp.arange(n_exp, dtype=jnp.int32)[None, :]).astype(jnp.int32)
    csum = jnp.cumsum(onehot, axis=0)
    rank = jnp.sum(onehot * (csum - 1), axis=-1)
    counts = csum[-1]
    padded = ((counts + tile - 1) // tile) * tile
    ends = jnp.cumsum(padded)
    dest = (ends - padded)[ef] + rank
    rows = 2 * n + n_exp * tile
    src = jnp.zeros((rows,), jnp.int32).at[dest].set(jnp.arange(2 * n, dtype=jnp.int32) // 2)
    tile_start = jnp.arange(rows // tile, dtype=jnp.int32) * tile
    tile_valid = (tile_start < ends[-1]).astype(jnp.int32)
    tile_expert = jnp.minimum(jnp.searchsorted(ends, tile_start, side="right"), n_exp - 1).astype(jnp.int32)
    last = jnp.maximum(ends[-1] // tile - 1, 0)
    tile_expert = jnp.where(tile_valid > 0, tile_expert, tile_expert[last])
    return dest, src, tile_expert, tile_valid


def _row_copy(src_hbm, row, dst_vmem, slot, sem):
    return pltpu.make_async_copy(src_hbm.at[pl.ds(row, 1), :], dst_vmem.at[pl.ds(slot, 1), :], sem)


def _gather_kernel(src_ref, h_hbm, o_ref, buf, sem):
    tg = buf.shape[0]
    base = pl.program_id(0) * tg

    def issue(r, carry):
        _row_copy(h_hbm, src_ref[base + r], buf, r, sem).start()
        return carry
    lax.fori_loop(0, tg, issue, 0, unroll=8)

    def drain(r, carry):
        _row_copy(h_hbm, 0, buf, r, sem).wait()
        return carry
    lax.fori_loop(0, tg, drain, 0, unroll=8)
    o_ref[...] = buf[...].astype(o_ref.dtype)


def moe_gather(h, src, *, tg=512):
    n, d = h.shape
    rows = src.shape[0]
    return pl.pallas_call(
        _gather_kernel,
        grid_spec=pltpu.PrefetchScalarGridSpec(
            num_scalar_prefetch=1, grid=(rows // tg,),
            in_specs=[pl.BlockSpec(memory_space=pl.ANY)],
            out_specs=pl.BlockSpec((tg, d), lambda i, src: (i, 0)),
            scratch_shapes=[pltpu.VMEM((tg, d), F32), pltpu.SemaphoreType.DMA(())]),
        out_shape=jax.ShapeDtypeStruct((rows, d), BF16),
        compiler_params=_params(1), name="moe_gather")(src, h)


def _experts_kernel(te_ref, tv_ref, xs_ref, w1_ref, w3_ref, w2_ref, ys_ref, acc_scr):
    i = pl.program_id(0)
    f = pl.program_id(1)
    last = pl.num_programs(1) - 1
    valid = tv_ref[i] > 0

    @pl.when(valid)
    def _():
        @pl.when(f == 0)
        def _():
            acc_scr[...] = jnp.zeros(acc_scr.shape, F32)
        h = xs_ref[...]
        g = _silu(_dot(h, w1_ref[...])) * _dot(h, w3_ref[...])
        acc_scr[...] += _dot(g.astype(BF16), w2_ref[...])

        @pl.when(f == last)
        def _():
            ys_ref[...] = acc_scr[...]

    @pl.when(jnp.logical_not(valid) & (f == last))
    def _():
        ys_ref[...] = jnp.zeros(ys_ref.shape, F32)


def moe_experts(xs, tile_expert, tile_valid, w1, w3, w2, idx, *, tf=256):
    rows, d = xs.shape
    dff = w1.shape[-1]
    tm = MOE_TILE
    nf = dff // tf
    fcol = lambda i, f, te, tv: jnp.where(tv[i] > 0, f, nf - 1)
    return pl.pallas_call(
        _experts_kernel,
        grid_spec=pltpu.PrefetchScalarGridSpec(
            num_scalar_prefetch=2, grid=(rows // tm, nf),
            in_specs=[pl.BlockSpec((tm, d), lambda i, f, te, tv: (i, 0)),
                      pl.BlockSpec((None, None, d, tf), lambda i, f, te, tv: (idx, te[i], 0, fcol(i, f, te, tv))),
                      pl.BlockSpec((None, None, d, tf), lambda i, f, te, tv: (idx, te[i], 0, fcol(i, f, te, tv))),
                      pl.BlockSpec((None, None, tf, d), lambda i, f, te, tv: (idx, te[i], fcol(i, f, te, tv), 0))],
            out_specs=pl.BlockSpec((tm, d), lambda i, f, te, tv: (i, 0)),
            scratch_shapes=[pltpu.VMEM((tm, d), F32)]),
        out_shape=jax.ShapeDtypeStruct((rows, d), F32),
        compiler_params=_params(2), name="moe_experts")(tile_expert, tile_valid, xs, w1, w3, w2)


def _moe_combine_kernel(dest_ref, ys_hbm, x_ref, route_ref, mod_ref, o_ref, buf0, buf1, sem):
    tc = x_ref.shape[0]
    base = 2 * pl.program_id(0) * tc

    def issue(t, carry):
        _row_copy(ys_hbm, dest_ref[base + 2 * t], buf0, t, sem).start()
        _row_copy(ys_hbm, dest_ref[base + 2 * t + 1], buf1, t, sem).start()
        return carry
    lax.fori_loop(0, tc, issue, 0, unroll=4)

    def drain(t, carry):
        _row_copy(ys_hbm, 0, buf0, t, sem).wait()
        _row_copy(ys_hbm, 0, buf1, t, sem).wait()
        return carry
    lax.fori_loop(0, tc, drain, 0, unroll=4)
    route = route_ref[...]
    y = route[:, 2:3] * buf0[...] + route[:, 3:4] * buf1[...]
    o_ref[...] = x_ref[...] + mod_ref[5:6, :] * y


def moe_combine(ys, dest, x, route, mod, layer, seq, *, tc=256):
    n, d = x.shape
    return pl.pallas_call(
        _moe_combine_kernel,
        grid_spec=pltpu.PrefetchScalarGridSpec(
            num_scalar_prefetch=1, grid=(n // tc,),
            in_specs=[pl.BlockSpec(memory_space=pl.ANY),
                      pl.BlockSpec((tc, d), lambda i, dest: (i, 0)),
                      pl.BlockSpec((tc, 128), lambda i, dest: (i, 0)),
                      pl.BlockSpec((None, None, 6, d), lambda i, dest: (layer, (i * tc) // seq, 0, 0))],
            out_specs=pl.BlockSpec((tc, d), lambda i, dest: (i, 0)),
            scratch_shapes=[pltpu.VMEM((tc, d), F32), pltpu.VMEM((tc, d), F32), pltpu.SemaphoreType.DMA(())]),
        out_shape=jax.ShapeDtypeStruct((n, d), F32),
        compiler_params=_params(1), name="moe_combine")(dest, ys, x, route, mod)


def moe_residual(x, norm_w, mod, router, w1, w3, w2, layer, idx, seq):
    h, route = moe_router(x, norm_w, mod, router, layer, idx, seq)
    dest, src, tile_expert, tile_valid = _moe_plan(route, w1.shape[1], MOE_TILE)
    xs = moe_gather(h, src)
    ys = moe_experts(xs, tile_expert, tile_valid, w1, w3, w2, idx)
    return moe_combine(ys, dest, x, route, mod, layer, seq)


def _final_norm_kernel(x_ref, w_ref, o_ref):
    x = x_ref[...]
    o_ref[...] = x * lax.rsqrt(jnp.mean(x * x, axis=-1, keepdims=True) + EPS) * w_ref[...]


def final_norm(x, w, *, tm=512):
    n, d = x.shape
    return pl.pallas_call(
        _final_norm_kernel, grid=(n // tm,),
        in_specs=[pl.BlockSpec((tm, d), lambda i: (i, 0)), pl.BlockSpec((1, d), lambda i: (0, 0))],
        out_specs=pl.BlockSpec((tm, d), lambda i: (i, 0)),
        out_shape=jax.ShapeDtypeStruct((n, d), F32),
        compiler_params=_params(1), name="final_norm")(x, w.reshape(1, d))


def _reorder_w_in(w_in):
    sizes = (512, 256, 256, 512, 128, 128, 128, 128, 128, 128, 12, 512, 512, 1024, 4, 4, 1024)
    (aq, ak, av, bq, bkc, bvc, bks, bvs, bkw, bvw, bg, cq, ck, cv, ci, cf, co) = jnp.split(
        w_in, np.cumsum(sizes)[:-1].tolist(), axis=-1)
    pad = lambda n: jnp.zeros(w_in.shape[:-1] + (n,), w_in.dtype)
    out = jnp.concatenate([aq, ak, av, cq, ck, cv, co, bq, bkc, bvc, bks, bvs, bkw, bvw,
                           bg, ci, cf, pad(128 - 20), pad(PROJ_COLS - COL_SMALL - 128)], axis=-1)
    assert out.shape[-1] == PROJ_COLS
    return out.astype(BF16)


def hybrid_mixer(proj, tables, layer, a_sinks, mlstm_gate_b, nsa_pe_k, nsa_pe_v, nsa_ck_w1, nsa_ck_w2,
                 nsa_cv_w1, nsa_cv_w2, mlstm_conv_w, mlstm_norm_w):
    bsz, t, _ = proj.shape
    o_a = sliding_window_attention(proj, col_q=COL_AQ, col_k=COL_AK, col_v=COL_AV, n_heads=A_HEADS,
                                   n_kv=A_KV_HEADS, window=A_WINDOW, tq=256, tables=tables,
                                   sinks=a_sinks[layer])
    chunks = lambda col: proj[:, :, col:col + HEAD_DIM].reshape(bsz, t // CMP_STRIDE, CMP_STRIDE * HEAD_DIM)
    k_cmp = compress_blocks(chunks(COL_BKC), nsa_pe_k, nsa_ck_w1, nsa_ck_w2, layer)
    v_cmp = compress_blocks(chunks(COL_BVC), nsa_pe_v, nsa_cv_w1, nsa_cv_w2, layer)
    o_cmp, bias_t = compressed_attention_select(proj, k_cmp, v_cmp)
    o_sel = selected_attention(proj, bias_t)
    o_win = sliding_window_attention(proj, col_q=COL_BQ, col_k=COL_BKW, col_v=COL_BVW, n_heads=B_HEADS,
                                     n_kv=1, window=B_WINDOW, tq=512, out_dtype=F32)
    o_b = combine_branches(proj, o_cmp, o_sel, o_win)
    o_c = mlstm_mixer(proj, mlstm_gate_b, mlstm_conv_w, mlstm_norm_w, layer)
    return o_a, o_b, o_c


def kernel(x, c, positions, ada_w, ada_b, norm_mix_w, norm_ffn_w, w_in, mlstm_gate_b, a_sinks, nsa_pe_k,
           nsa_pe_v, nsa_ck_w1, nsa_ck_w2, nsa_cv_w1, nsa_cv_w2, mlstm_conv_w, mlstm_norm_w, w_out, ffn_w1,
           ffn_w3, ffn_w2, moe_router, moe_w1, moe_w3, moe_w2, final_norm_w):
    bsz, t, d = x.shape
    depth = ada_w.shape[0]
    mod = ada_modulation(c, ada_w, ada_b)
    tables = rope_tables(positions)
    w_in_p = _reorder_w_in(w_in)
    w_out_b = w_out.astype(BF16)
    ffn_b = [w.astype(BF16) for w in (ffn_w1, ffn_w3, ffn_w2)]
    moe_b = [w.astype(BF16) for w in (moe_w1, moe_w3, moe_w2)]
    router_p = jnp.pad(moe_router, ((0, 0), (0, 0), (0, 128 - N_EXPERTS)))
    norm_mix = norm_mix_w.reshape(depth, 1, d)
    norm_ffn = norm_ffn_w.reshape(depth, 1, d)
    conv_w = mlstm_conv_w
    norm_c = mlstm_norm_w.reshape(depth, 1, -1)

    xf = x.reshape(bsz * t, d)
    for layer in range(depth):
        proj = norm_proj(xf, norm_mix, mod, w_in_p, layer, t, shift_row=0, scale_row=1)
        o_a, o_b, o_c = hybrid_mixer(proj.reshape(bsz, t, PROJ_COLS), tables, layer, a_sinks, mlstm_gate_b,
                                     nsa_pe_k, nsa_pe_v, nsa_ck_w1, nsa_ck_w2, nsa_cv_w1, nsa_cv_w2,
                                     conv_w, norm_c)
        flat = lambda a: a.reshape(bsz * t, a.shape[-1])
        xf = out_proj_residual(flat(o_a), flat(o_b), flat(o_c), w_out_b, xf, mod, layer, t)
        if layer % 2 == 0:
            xf = ffn_residual(xf, norm_ffn, mod, *ffn_b, layer, layer // 2, t)
        else:
            xf = moe_residual(xf, norm_ffn, mod, router_p, *moe_b, layer, layer // 2, t)
    return final_norm(xf, final_norm_w).reshape(bsz, t, d)
```

```python
import functools
import math

import numpy as np
import jax
import jax.numpy as jnp
from jax import lax
from jax.experimental import pallas as pl
from jax.experimental.pallas import tpu as pltpu

F32 = jnp.float32
BF16 = jnp.bfloat16

HEAD_DIM = 128
A_HEADS, A_KV_HEADS, A_WINDOW = 4, 2, 128
ROPE_THETA = 150000.0
B_HEADS = 4
CMP_LEN, CMP_STRIDE = 32, 16
SEL_LEN, SEL_TOPK, B_WINDOW = 64, 16, 512
C_HEADS, C_QK_DIM, C_V_DIM, C_CONV = 4, 128, 256, 4
N_EXPERTS = 8
EPS = 1e-6
BIG = 1e9
NEG = -1e30
SCALE = HEAD_DIM ** -0.5

COL_AQ, COL_AK, COL_AV = 0, 512, 768
COL_CQK, COL_CV, COL_CO = 1024, 2048, 3072
COL_BQ = 4096
COL_BKC, COL_BVC, COL_BKS, COL_BVS, COL_BKW, COL_BVW = 4608, 4736, 4864, 4992, 5120, 5248
COL_SMALL = 5376
LANE_CI, LANE_CF = 12, 16
PROJ_COLS = 5632

VMEM_LIMIT_MB = 56
BF16_SUBLANES = 16


def _params(n_axes, vmem_mb=VMEM_LIMIT_MB):
    return pltpu.CompilerParams(dimension_semantics=("arbitrary",) * n_axes,
                                vmem_limit_bytes=vmem_mb * 2 ** 20)


def _dot(a, b):
    return jnp.dot(a, b, preferred_element_type=F32)


def _dot_nt(a, b):
    return lax.dot_general(a, b, (((1,), (1,)), ((), ())), preferred_element_type=F32)


def _silu(v):
    return v * jax.nn.sigmoid(v)


def _norm_mod(x, nw, shift, scale):
    ms = jnp.mean(x * x, axis=-1, keepdims=True)
    return (x * lax.rsqrt(ms + EPS) * nw) * (1.0 + scale) + shift


def _ada_kernel(c_ref, w_ref, b_ref, o_ref):
    act = _silu(c_ref[...]).astype(BF16)
    o_ref[...] = _dot(act, w_ref[...].astype(BF16)) + b_ref[...]


def ada_modulation(c, ada_w, ada_b):
    depth, d, n6 = ada_w.shape
    bsz = c.shape[0]
    assert bsz <= 8
    cp = jnp.zeros((8, d), F32).at[:bsz].set(c)
    tn = 512
    out = pl.pallas_call(
        _ada_kernel, grid=(depth, n6 // tn),
        in_specs=[pl.BlockSpec((8, d), lambda l, j: (0, 0)),
                  pl.BlockSpec((None, d, tn), lambda l, j: (l, 0, j)),
                  pl.BlockSpec((None, 1, tn), lambda l, j: (l, 0, j))],
        out_specs=pl.BlockSpec((None, 8, tn), lambda l, j: (l, 0, j)),
        out_shape=jax.ShapeDtypeStruct((depth, 8, n6), F32),
        compiler_params=_params(2), name="ada_modulation")(cp, ada_w, ada_b.reshape(depth, 1, n6))
    return out.reshape(depth, 8, 6, d)


def _rope_table_kernel(pos_ref, cos_ref, sin_ref):
    pos = pos_ref[...].astype(F32)
    lane = lax.broadcasted_iota(jnp.int32, (1, HEAD_DIM), 1)
    half = HEAD_DIM // 2
    inv = jnp.exp((lane & (half - 1)).astype(F32) * (-math.log(ROPE_THETA) * 2.0 / HEAD_DIM))
    ang = pos * inv
    cos_ref[...] = jnp.cos(ang)
    sin_ref[...] = jnp.where(lane < half, -1.0, 1.0) * jnp.sin(ang)


def rope_tables(positions):
    bsz, t = positions.shape
    tt = 512
    spec = pl.BlockSpec((None, tt, HEAD_DIM), lambda b, i: (b, i, 0))
    return pl.pallas_call(
        _rope_table_kernel, grid=(bsz, t // tt),
        in_specs=[pl.BlockSpec((None, tt, 1), lambda b, i: (b, i, 0))],
        out_specs=[spec, spec],
        out_shape=[jax.ShapeDtypeStruct((bsz, t, HEAD_DIM), F32)] * 2,
        compiler_params=_params(2), name="rope_tables")(positions.reshape(bsz, t, 1))


def _proj_kernel(x_ref, nw_ref, mod_ref, w_ref, o_ref, h_scr, *, shift_row, scale_row):
    @pl.when(pl.program_id(1) == 0)
    def _():
        h = _norm_mod(x_ref[...], nw_ref[...], mod_ref[shift_row:shift_row + 1, :],
                      mod_ref[scale_row:scale_row + 1, :])
        h_scr[...] = h.astype(BF16)

    o_ref[...] = _dot(h_scr[...], w_ref[...])


def norm_proj(x, norm_w, mod, w, layer, seq, *, shift_row, scale_row, tm=1024, tn=1408):
    n, d = x.shape
    p = w.shape[-1]
    assert seq % tm == 0 and p % tn == 0
    return pl.pallas_call(
        functools.partial(_proj_kernel, shift_row=shift_row, scale_row=scale_row),
        grid=(n // tm, p // tn),
        in_specs=[pl.BlockSpec((tm, d), lambda i, j: (i, 0)),
                  pl.BlockSpec((None, 1, d), lambda i, j: (layer, 0, 0)),
                  pl.BlockSpec((None, None, 6, d), lambda i, j: (layer, (i * tm) // seq, 0, 0)),
                  pl.BlockSpec((None, d, tn), lambda i, j: (layer, 0, j))],
        out_specs=pl.BlockSpec((tm, tn), lambda i, j: (i, j)),
        out_shape=jax.ShapeDtypeStruct((n, p), F32),
        scratch_shapes=[pltpu.VMEM((tm, d), BF16)],
        compiler_params=_params(2), name="norm_proj")(x, norm_w, mod, w)


def _swa_kernel(*refs, tq, window, n_heads, group, rope, sinks):
    it = iter(refs)
    q_ref, kp_ref, kc_ref, vp_ref, vc_ref = (next(it) for _ in range(5))
    if rope:
        cp_ref, sp_ref, cc_ref, sc_ref = (next(it) for _ in range(4))
    if sinks:
        sink_ref = next(it)
    o_ref = next(it)
    i = pl.program_id(1)
    row = lax.broadcasted_iota(jnp.int32, (tq, 2 * tq), 0)
    col = lax.broadcasted_iota(jnp.int32, (tq, 2 * tq), 1)
    diff = row + tq - col + jnp.where(col < tq, jnp.where(i > 0, 0, window), 0)
    mask = jnp.abs(2 * diff - (window - 1)) <= (window - 1)

    def rotate(v, c, s):
        return v * c + pltpu.roll(v, HEAD_DIM // 2, 1) * s

    kv_cache = {}
    for hq in range(n_heads):
        kv = hq // group
        if kv not in kv_cache:
            sl = slice(kv * HEAD_DIM, (kv + 1) * HEAD_DIM)
            kp, kc = kp_ref[:, sl], kc_ref[:, sl]
            if rope:
                kp = rotate(kp, cp_ref[...], sp_ref[...])
                kc = rotate(kc, cc_ref[...], sc_ref[...])
            k = jnp.concatenate([kp, kc], axis=0).astype(BF16)
            v = jnp.concatenate([vp_ref[:, sl], vc_ref[:, sl]], axis=0).astype(BF16)
            kv_cache[kv] = (k, v)
        k, v = kv_cache[kv]
        hs = slice(hq * HEAD_DIM, (hq + 1) * HEAD_DIM)
        q = q_ref[:, hs]
        if rope:
            q = rotate(q, cc_ref[...], sc_ref[...])
        s = _dot_nt((q * SCALE).astype(BF16), k)
        s = jnp.where(mask, s, NEG)
        m = jnp.max(s, axis=-1, keepdims=True)
        if sinks:
            m = jnp.maximum(m, sink_ref[hq])
        m = jnp.where(m < 0.5 * NEG, 0.0, m)
        p = jnp.exp(s - m)
        den = jnp.sum(p, axis=-1, keepdims=True)
        if sinks:
            den = den + jnp.exp(sink_ref[hq] - m)
        o = _dot(p.astype(BF16), v) / jnp.maximum(den, 1e-30)
        o_ref[:, hs] = o.astype(o_ref.dtype)


def sliding_window_attention(proj, *, col_q, col_k, col_v, n_heads, n_kv, window, tq,
                             tables=None, sinks=None, out_dtype=BF16):
    bsz, t, _ = proj.shape
    assert window - 1 <= tq and t % tq == 0
    wq, wkv = n_heads * HEAD_DIM, n_kv * HEAD_DIM
    assert col_q % wq == 0 and col_k % wkv == 0 and col_v % wkv == 0
    cur = lambda b, i: (b, i, 0)
    prev = lambda b, i: (b, jnp.maximum(i - 1, 0), 0)
    kvspec = lambda col, im: pl.BlockSpec(
        (None, tq, wkv), lambda b, i: (b, im(b, i)[1], col // wkv))
    in_specs = [pl.BlockSpec((None, tq, wq), lambda b, i: (b, i, col_q // wq)),
                kvspec(col_k, prev), kvspec(col_k, cur), kvspec(col_v, prev), kvspec(col_v, cur)]
    args = [proj] * 5
    if tables is not None:
        cos, sin = tables
        tp = pl.BlockSpec((None, tq, HEAD_DIM), prev)
        tc = pl.BlockSpec((None, tq, HEAD_DIM), cur)
        in_specs += [tp, tp, tc, tc]
        args += [cos, sin, cos, sin]
    if sinks is not None:
        in_specs.append(pl.BlockSpec(memory_space=pltpu.SMEM))
        args.append(sinks)
    return pl.pallas_call(
        functools.partial(_swa_kernel, tq=tq, window=window, n_heads=n_heads, group=n_heads // n_kv,
                          rope=tables is not None, sinks=sinks is not None),
        grid=(bsz, t // tq), in_specs=in_specs,
        out_specs=pl.BlockSpec((None, tq, wq), cur),
        out_shape=jax.ShapeDtypeStruct((bsz, t, wq), out_dtype),
        compiler_params=_params(2), name="sliding_window_attention")(*args)


def _compress_kernel(r_ref, pe_ref, w1_ref, w2_ref, o_ref):
    n16, half = r_ref.shape
    r = r_ref[...].astype(BF16)
    w1 = w1_ref[...].astype(BF16)
    first = _dot(r, w1[:half])
    second = _dot(r, w1[half:])
    pe = _dot(pe_ref[...].astype(BF16), w1)[0:1]
    pre = first + pltpu.roll(second, n16 - 1, 0) + pe
    hid = jax.nn.gelu(pre, approximate=True)
    o_ref[...] = _dot(hid.astype(BF16), w2_ref[...].astype(BF16))


def compress_blocks(chunks, pe, w1, w2, layer):
    bsz, n16, half = chunks.shape
    hid = w1.shape[-1]
    pe8 = jnp.broadcast_to(pe[layer].reshape(1, 2 * half), (8, 2 * half))
    return pl.pallas_call(
        _compress_kernel, grid=(bsz,),
        in_specs=[pl.BlockSpec((None, n16, half), lambda b: (b, 0, 0)),
                  pl.BlockSpec((8, 2 * half), lambda b: (0, 0)),
                  pl.BlockSpec((None, 2 * half, hid), lambda b: (layer, 0, 0)),
                  pl.BlockSpec((None, hid, HEAD_DIM), lambda b: (layer, 0, 0))],
        out_specs=pl.BlockSpec((None, n16, HEAD_DIM), lambda b: (b, 0, 0)),
        out_shape=jax.ShapeDtypeStruct((bsz, n16, HEAD_DIM), F32),
        compiler_params=_params(1), name="compress_blocks")(chunks, pe8, w1, w2)


def _heads_transposed(q, tq, scale=SCALE):
    n = q.shape[1] // HEAD_DIM
    return jnp.concatenate([(q[:, g * HEAD_DIM:(g + 1) * HEAD_DIM] * scale).T for g in range(n)],
                           axis=1).astype(BF16)


def _cmp_select_kernel(q_ref, kc_ref, vc_ref, ov_ref, ocmp_ref, bias_ref, *, tq, n_sel, topk):
    i = pl.program_id(1)
    nc = kc_ref.shape[0]
    hq = B_HEADS * tq
    qt = _heads_transposed(q_ref[...], tq)
    st = _dot(kc_ref[...].astype(BF16), qt)
    n_row = lax.broadcasted_iota(jnp.int32, (nc, hq), 0)
    t_col = i * tq + (lax.broadcasted_iota(jnp.int32, (nc, hq), 1) & (tq - 1))
    st = jnp.where(n_row * CMP_STRIDE + (CMP_LEN - 1) <= t_col, st, NEG)
    m = jnp.max(st, axis=0, keepdims=True)
    m = jnp.where(m < 0.5 * NEG, 0.0, m)
    p = jnp.exp(st - m)
    p = p / jnp.maximum(jnp.sum(p, axis=0, keepdims=True), 1e-30)
    pb = p.astype(BF16)
    ot = _dot(vc_ref[...].T.astype(BF16), pb)
    for g in range(B_HEADS):
        ocmp_ref[:, g * HEAD_DIM:(g + 1) * HEAD_DIM] = ot[:, g * tq:(g + 1) * tq].T
    imp4 = _dot(ov_ref[...], pb)
    imp = imp4[:, 0:tq]
    for g in range(1, B_HEADS):
        imp = imp + imp4[:, g * tq:(g + 1) * tq]
    nsp = imp.shape[0]
    j = lax.broadcasted_iota(jnp.int32, (nsp, tq), 0)
    t = i * tq + lax.broadcasted_iota(jnp.int32, (nsp, tq), 1)
    cur = t >> 6
    imp = jnp.where((j << 6) <= t, imp, -BIG)
    for forced in (0, cur, cur - 1):
        imp = jnp.where(j == forced, BIG, imp)
    imp = jnp.where(j < n_sel, imp, -3e38)
    sel = jnp.zeros((nsp, tq), F32)
    for _ in range(topk):
        mx = jnp.max(imp, axis=0, keepdims=True)
        idx = jnp.min(jnp.where(imp == mx, j, nsp), axis=0, keepdims=True)
        hit = j == idx
        sel = jnp.where(hit, 1.0, sel)
        imp = jnp.where(hit, -3e38, imp)
    bias_ref[...] = jnp.where(sel > 0.0, 0.0, jnp.where(j < n_sel, -BIG, 0.0))


def _overlap_matrix_t(t):
    n16, n_sel = t // CMP_STRIDE, t // SEL_LEN
    nsp = -(-n_sel // 128) * 128
    starts = np.arange(n16) * CMP_STRIDE
    sel_start = np.arange(n_sel) * SEL_LEN
    ov = np.clip(np.minimum(starts[:, None] + CMP_LEN, sel_start[None, :] + SEL_LEN)
                 - np.maximum(starts[:, None], sel_start[None, :]), 0, None).astype(np.float32) / CMP_LEN
    out = np.zeros((nsp, n16), np.float32)
    out[:n_sel] = ov.T
    return jnp.asarray(out, BF16)


def compressed_attention_select(proj, k_cmp, v_cmp, *, tq=128):
    bsz, t, _ = proj.shape
    n16, n_sel = t // CMP_STRIDE, t // SEL_LEN
    assert SEL_LEN == 64 and tq == 128
    ov = _overlap_matrix_t(t)
    nsp = ov.shape[0]
    wq = B_HEADS * HEAD_DIM
    return pl.pallas_call(
        functools.partial(_cmp_select_kernel, tq=tq, n_sel=n_sel, topk=min(SEL_TOPK, n_sel)),
        grid=(bsz, t // tq),
        in_specs=[pl.BlockSpec((None, tq, wq), lambda b, i: (b, i, COL_BQ // wq)),
                  pl.BlockSpec((None, n16, HEAD_DIM), lambda b, i: (b, 0, 0)),
                  pl.BlockSpec((None, n16, HEAD_DIM), lambda b, i: (b, 0, 0)),
                  pl.BlockSpec((nsp, n16), lambda b, i: (0, 0))],
        out_specs=[pl.BlockSpec((None, tq, wq), lambda b, i: (b, i, 0)),
                   pl.BlockSpec((None, nsp, tq), lambda b, i: (b, 0, i))],
        out_shape=[jax.ShapeDtypeStruct((bsz, t, wq), F32),
                   jax.ShapeDtypeStruct((bsz, nsp, t), F32)],
        compiler_params=_params(2), name="compressed_attention_select")(proj, k_cmp, v_cmp, ov)


def _selected_kernel(q_ref, bias_ref, ks_ref, vs_ref, o_ref, kaug_scr, vt_scr, qa_scr, s0_scr, s1_scr,
                     m_scr, acc_scr, *, tq, kb):
    i = pl.program_id(1)
    t_all = ks_ref.shape[0]
    nsp = bias_ref.shape[0]
    hq = B_HEADS * tq
    ones_rows = vt_scr.shape[1] - HEAD_DIM

    @pl.when(i == 0)
    def _():
        def prep(c, carry):
            r0 = pl.multiple_of(c * kb, kb)
            kaug_scr[pl.ds(r0, kb), 0:HEAD_DIM] = ks_ref[pl.ds(r0, kb), :].astype(BF16)
            key = r0 + lax.broadcasted_iota(jnp.int32, (kb, nsp), 0)
            blk = lax.broadcasted_iota(jnp.int32, (kb, nsp), 1)
            kaug_scr[pl.ds(r0, kb), HEAD_DIM:] = jnp.where((key >> 6) == blk, 1.0, 0.0).astype(BF16)
            vt_scr[c] = jnp.concatenate([vs_ref[pl.ds(r0, kb), :].T, jnp.ones((ones_rows, kb), F32)],
                                        axis=0).astype(BF16)
            return carry
        lax.fori_loop(0, t_all // kb, prep, 0)

    bias = bias_ref[...]
    qa_scr[...] = jnp.concatenate([_heads_transposed(q_ref[...], tq, SCALE * math.log2(math.e)),
                                   jnp.concatenate([bias] * B_HEADS, axis=1).astype(BF16)], axis=0)
    m_scr[...] = jnp.full(m_scr.shape, NEG, F32)
    acc_scr[...] = jnp.zeros(acc_scr.shape, F32)

    def scores(kt, s_out):
        r0 = pl.multiple_of(kt * kb, kb)
        s_out[...] = _dot(kaug_scr[pl.ds(r0, kb), :], qa_scr[...])

    def consume(kt, s_in, causal):
        st = s_in[...]
        if causal:
            key = kt * kb + lax.broadcasted_iota(jnp.int32, (kb, hq), 0)
            tpos = i * tq + (lax.broadcasted_iota(jnp.int32, (kb, hq), 1) & (tq - 1))
            st = jnp.where(key <= tpos, st, NEG)
        m_old = m_scr[...]
        m_new = jnp.maximum(m_old, jnp.max(st, axis=0, keepdims=True))
        p = jnp.exp2((st - m_new).astype(BF16))
        acc_scr[...] = jnp.exp2(m_old - m_new) * acc_scr[...] + _dot(vt_scr[kt], p)
        m_scr[...] = m_new

    def step(kt, s_in, s_out):
        scores(kt + 1, s_out)
        consume(kt, s_in, False)

    def finish(kt, s_in):
        consume(kt, s_in, True)
        ot = acc_scr[0:HEAD_DIM, :] / jnp.maximum(acc_scr[HEAD_DIM:HEAD_DIM + 1, :], 1e-30)
        for g in range(B_HEADS):
            o_ref[:, g * HEAD_DIM:(g + 1) * HEAD_DIM] = ot[:, g * tq:(g + 1) * tq].T

    n_full = (i * tq) // kb
    scores(0, s0_scr)

    def pair(j, carry):
        step(2 * j, s0_scr, s1_scr)
        step(2 * j + 1, s1_scr, s0_scr)
        return carry
    lax.fori_loop(0, n_full // 2, pair, 0)

    @pl.when(n_full % 2 == 1)
    def _():
        step(n_full - 1, s0_scr, s1_scr)
        finish(n_full, s1_scr)

    @pl.when(n_full % 2 == 0)
    def _():
        finish(n_full, s0_scr)


def selected_attention(proj, bias_t, *, tq=128, kb=512):
    bsz, t, _ = proj.shape
    nsp = bias_t.shape[1]
    wq = B_HEADS * HEAD_DIM
    assert kb % tq == 0 and t % kb == 0
    return pl.pallas_call(
        functools.partial(_selected_kernel, tq=tq, kb=kb),
        grid=(bsz, t // tq),
        in_specs=[pl.BlockSpec((None, tq, wq), lambda b, i: (b, i, COL_BQ // wq)),
                  pl.BlockSpec((None, nsp, tq), lambda b, i: (b, 0, i)),
                  pl.BlockSpec((None, t, HEAD_DIM), lambda b, i: (b, 0, COL_BKS // HEAD_DIM)),
                  pl.BlockSpec((None, t, HEAD_DIM), lambda b, i: (b, 0, COL_BVS // HEAD_DIM))],
        out_specs=pl.BlockSpec((None, tq, wq), lambda b, i: (b, i, 0)),
        out_shape=jax.ShapeDtypeStruct((bsz, t, wq), F32),
        scratch_shapes=[pltpu.VMEM((t, HEAD_DIM + nsp), BF16),
                        pltpu.VMEM((t // kb, HEAD_DIM + BF16_SUBLANES, kb), BF16),
                        pltpu.VMEM((HEAD_DIM + nsp, B_HEADS * tq), BF16),
                        pltpu.VMEM((kb, B_HEADS * tq), F32),
                        pltpu.VMEM((kb, B_HEADS * tq), F32),
                        pltpu.VMEM((1, B_HEADS * tq), F32),
                        pltpu.VMEM((HEAD_DIM + BF16_SUBLANES, B_HEADS * tq), F32)],
        compiler_params=_params(2), name="selected_attention")(proj, bias_t, proj, proj)


def _combine_kernel(g_ref, cmp_ref, sel_ref, win_ref, o_ref):
    g = jax.nn.sigmoid(g_ref[...])
    for h in range(B_HEADS):
        hs = slice(h * HEAD_DIM, (h + 1) * HEAD_DIM)
        o = (g[:, 3 * h:3 * h + 1] * cmp_ref[:, hs] + g[:, 3 * h + 1:3 * h + 2] * sel_ref[:, hs]
             + g[:, 3 * h + 2:3 * h + 3] * win_ref[:, hs])
        o_ref[:, hs] = o.astype(o_ref.dtype)


def combine_branches(proj, o_cmp, o_sel, o_win, *, tq=512):
    bsz, t, _ = proj.shape
    wq = B_HEADS * HEAD_DIM
    spec = pl.BlockSpec((None, tq, wq), lambda b, i: (b, i, 0))
    return pl.pallas_call(
        _combine_kernel, grid=(bsz, t // tq),
        in_specs=[pl.BlockSpec((None, tq, 128), lambda b, i: (b, i, COL_SMALL // 128)), spec, spec, spec],
        out_specs=spec, out_shape=jax.ShapeDtypeStruct((bsz, t, wq), BF16),
        compiler_params=_params(2), name="combine_branches")(proj, o_cmp, o_sel, o_win)


def _mlstm_kernel(gb_ref, qk_ref, v_ref, og_ref, small_ref, convw_ref, normw_ref, o_ref,
                  hist_scr, c_scr, n_scr, m_scr, *, chunk):
    L = chunk
    dqk, dv = C_QK_DIM, C_V_DIM

    @pl.when(pl.program_id(1) == 0)
    def _():
        hist_scr[...] = jnp.zeros(hist_scr.shape, F32)
        c_scr[...] = jnp.zeros(c_scr.shape, F32)
        n_scr[...] = jnp.zeros(n_scr.shape, F32)
        m_scr[...] = jnp.zeros(m_scr.shape, F32)

    x = qk_ref[...]
    xe = jnp.concatenate([hist_scr[...], x], axis=0)
    w = convw_ref[...]
    y = w[0:1] * xe[8 - 3:8 - 3 + L]
    for tap in range(1, C_CONV):
        y = y + w[tap:tap + 1] * xe[8 - 3 + tap:8 - 3 + tap + L]
    hist_scr[...] = x[L - 8:L]
    qk = _silu(y)

    lane = lax.broadcasted_iota(jnp.int32, (1, 128), 1)
    gbias = jnp.zeros((1, 128), F32)
    for idx in range(2 * C_HEADS):
        gbias = jnp.where(lane == LANE_CI + idx, gb_ref[idx], gbias)
    pre = small_ref[...] + gbias
    logf = jnp.minimum(pre, 0.0) - jnp.log1p(jnp.exp(-jnp.abs(pre)))
    row = lax.broadcasted_iota(jnp.int32, (L, L), 0)
    col = lax.broadcasted_iota(jnp.int32, (L, L), 1)
    causal = row >= col
    tri = jnp.where(causal, 1.0, 0.0)
    hp = lax.Precision.HIGHEST
    b_cols = jnp.dot(tri, logf, precision=hp, preferred_element_type=F32)
    b_rows = lax.dot_general(logf.T, tri, (((1,), (1,)), ((), ())), precision=hp,
                             preferred_element_type=F32)
    pre_t = pre.T

    for h in range(C_HEADS):
        q = qk[:, h * dqk:(h + 1) * dqk]
        k = qk[:, C_HEADS * dqk + h * dqk:C_HEADS * dqk + (h + 1) * dqk] * (dqk ** -0.5)
        v = v_ref[:, h * dv:(h + 1) * dv].astype(BF16)
        b_col = b_cols[:, LANE_CF + h:LANE_CF + h + 1]
        i_col = pre[:, LANE_CI + h:LANE_CI + h + 1]
        b_row = b_rows[LANE_CF + h:LANE_CF + h + 1, :]
        i_row = pre_t[LANE_CI + h:LANE_CI + h + 1, :]
        m_prev = m_scr[h:h + 1, 0:1]
        ct = c_scr[h]
        n_row = n_scr[h:h + 1, :]

        d = jnp.where(causal, b_col - b_row + i_row, NEG)
        m_inter = b_col + m_prev
        m_t = jnp.maximum(m_inter, jnp.max(d, axis=-1, keepdims=True))
        qb = q.astype(BF16)
        s = _dot_nt(qb, k.astype(BF16)) * jnp.exp(d - m_t)
        inter = jnp.exp(m_inter - m_t)
        num = _dot(s.astype(BF16), v) + inter * _dot(qb, ct.astype(BF16))
        den = jnp.sum(s, axis=-1, keepdims=True) + inter * jnp.sum(q * n_row, axis=-1, keepdims=True)
        hh = num / jnp.maximum(jnp.abs(den), jnp.exp(-m_t))

        b_last = b_col[L - 1:L, :]
        g = b_last - b_col + i_col
        m_new = jnp.maximum(b_last + m_prev, jnp.max(g, axis=0, keepdims=True))
        kd = k * jnp.exp(g - m_new)
        keep = jnp.exp(b_last + m_prev - m_new)
        c_scr[h] = keep * ct + _dot(kd.T.astype(BF16), v)
        n_scr[h:h + 1, :] = keep * n_row + jnp.sum(kd, axis=0, keepdims=True)
        m_scr[h:h + 1, :] = jnp.broadcast_to(m_new, (1, 128))

        vs = slice(h * dv, (h + 1) * dv)
        hn = hh * lax.rsqrt(jnp.mean(hh * hh, axis=-1, keepdims=True) + EPS) * normw_ref[:, vs]
        o_ref[:, vs] = (hn * jax.nn.sigmoid(og_ref[:, vs])).astype(o_ref.dtype)


def mlstm_mixer(proj, gate_b, conv_w, norm_w, layer, *, chunk=128):
    bsz, t, _ = proj.shape
    wqk, wv = 2 * C_HEADS * C_QK_DIM, C_HEADS * C_V_DIM
    assert wqk == wv == 1024 and t % chunk == 0
    blk = lambda col: pl.BlockSpec((None, chunk, 1024), lambda b, i: (b, i, col // 1024))
    return pl.pallas_call(
        functools.partial(_mlstm_kernel, chunk=chunk),
        grid=(bsz, t // chunk),
        in_specs=[pl.BlockSpec(memory_space=pltpu.SMEM),
                  blk(COL_CQK), blk(COL_CV), blk(COL_CO),
                  pl.BlockSpec((None, chunk, 128), lambda b, i: (b, i, COL_SMALL // 128)),
                  pl.BlockSpec((None, C_CONV, wqk), lambda b, i: (layer, 0, 0)),
                  pl.BlockSpec((None, 1, wv), lambda b, i: (layer, 0, 0))],
        out_specs=pl.BlockSpec((None, chunk, wv), lambda b, i: (b, i, 0)),
        out_shape=jax.ShapeDtypeStruct((bsz, t, wv), BF16),
        scratch_shapes=[pltpu.VMEM((8, wqk), F32),
                        pltpu.VMEM((C_HEADS, C_QK_DIM, C_V_DIM), F32),
                        pltpu.VMEM((8, C_QK_DIM), F32),
                        pltpu.VMEM((8, 128), F32)],
        compiler_params=_params(2), name="mlstm_mixer")(
            gate_b[layer], proj, proj, proj, proj, conv_w, norm_w)


def _outproj_kernel(oa_ref, ob_ref, oc_ref, w_ref, x_ref, mod_ref, o_ref, *, gate_row):
    wa, wb = oa_ref.shape[1], ob_ref.shape[1]
    acc = _dot(oa_ref[...], w_ref[0:wa, :])
    acc = acc + _dot(ob_ref[...], w_ref[wa:wa + wb, :])
    acc = acc + _dot(oc_ref[...], w_ref[wa + wb:, :])
    o_ref[...] = x_ref[...] + mod_ref[gate_row:gate_row + 1, :] * acc


def out_proj_residual(o_a, o_b, o_c, w_out, x, mod, layer, seq, *, tm=1024, tn=1024):
    n, d = x.shape
    assert seq % tm == 0 and d % tn == 0
    wa, wb, wc = o_a.shape[1], o_b.shape[1], o_c.shape[1]
    return pl.pallas_call(
        functools.partial(_outproj_kernel, gate_row=2),
        grid=(n // tm, d // tn),
        in_specs=[pl.BlockSpec((tm, wa), lambda i, j: (i, 0)),
                  pl.BlockSpec((tm, wb), lambda i, j: (i, 0)),
                  pl.BlockSpec((tm, wc), lambda i, j: (i, 0)),
                  pl.BlockSpec((None, wa + wb + wc, tn), lambda i, j: (layer, 0, j)),
                  pl.BlockSpec((tm, tn), lambda i, j: (i, j)),
                  pl.BlockSpec((None, None, 6, tn), lambda i, j: (layer, (i * tm) // seq, 0, j))],
        out_specs=pl.BlockSpec((tm, tn), lambda i, j: (i, j)),
        out_shape=jax.ShapeDtypeStruct((n, d), F32),
        compiler_params=_params(2), name="out_proj_residual")(o_a, o_b, o_c, w_out, x, mod)


def _ffn_kernel(x_ref, nw_ref, mod_ref, w1_ref, w3_ref, w2_ref, o_ref, h_scr, acc_scr):
    f = pl.program_id(1)

    @pl.when(f == 0)
    def _():
        h = _norm_mod(x_ref[...], nw_ref[...], mod_ref[3:4, :], mod_ref[4:5, :])
        h_scr[...] = h.astype(BF16)
        acc_scr[...] = jnp.zeros(acc_scr.shape, F32)

    h = h_scr[...]
    g = _silu(_dot(h, w1_ref[...])) * _dot(h, w3_ref[...])
    acc_scr[...] += _dot(g.astype(BF16), w2_ref[...])

    @pl.when(f == pl.num_programs(1) - 1)
    def _():
        o_ref[...] = x_ref[...] + mod_ref[5:6, :] * acc_scr[...]


def ffn_residual(x, norm_w, mod, w1, w3, w2, layer, idx, seq, *, tm=512, tf=512):
    n, d = x.shape
    dff = w1.shape[-1]
    return pl.pallas_call(
        _ffn_kernel, grid=(n // tm, dff // tf),
        in_specs=[pl.BlockSpec((tm, d), lambda i, f: (i, 0)),
                  pl.BlockSpec((None, 1, d), lambda i, f: (layer, 0, 0)),
                  pl.BlockSpec((None, None, 6, d), lambda i, f: (layer, (i * tm) // seq, 0, 0)),
                  pl.BlockSpec((None, d, tf), lambda i, f: (idx, 0, f)),
                  pl.BlockSpec((None, d, tf), lambda i, f: (idx, 0, f)),
                  pl.BlockSpec((None, tf, d), lambda i, f: (idx, f, 0))],
        out_specs=pl.BlockSpec((tm, d), lambda i, f: (i, 0)),
        out_shape=jax.ShapeDtypeStruct((n, d), F32),
        scratch_shapes=[pltpu.VMEM((tm, d), BF16), pltpu.VMEM((tm, d), F32)],
        compiler_params=_params(2), name="ffn_residual")(x, norm_w, mod, w1, w3, w2)


MOE_TILE = 512


def _router_kernel(x_ref, nw_ref, mod_ref, r_ref, h_ref, route_ref):
    tm = x_ref.shape[0]
    lane = lax.broadcasted_iota(jnp.int32, (tm, 128), 1)
    h = _norm_mod(x_ref[...], nw_ref[...], mod_ref[3:4, :], mod_ref[4:5, :])
    h_ref[...] = h
    logits = jnp.dot(h, r_ref[...], precision=lax.Precision.HIGHEST, preferred_element_type=F32)
    logits = jnp.where(lane < N_EXPERTS, logits, NEG)
    v1 = jnp.max(logits, axis=-1, keepdims=True)
    i1 = jnp.min(jnp.where(logits == v1, lane, 128), axis=-1, keepdims=True)
    rest = jnp.where(lane == i1, NEG, logits)
    v2 = jnp.max(rest, axis=-1, keepdims=True)
    i2 = jnp.min(jnp.where(rest == v2, lane, 128), axis=-1, keepdims=True)
    e2 = jnp.exp(v2 - v1)
    route = jnp.where(lane == 0, i1.astype(F32), jnp.where(lane == 1, i2.astype(F32), 0.0))
    route = jnp.where(lane == 2, 1.0 / (1.0 + e2), jnp.where(lane == 3, e2 / (1.0 + e2), route))
    route_ref[...] = route


def moe_router(x, norm_w, mod, router, layer, idx, seq, *, tm=512):
    n, d = x.shape
    return pl.pallas_call(
        _router_kernel, grid=(n // tm,),
        in_specs=[pl.BlockSpec((tm, d), lambda i: (i, 0)),
                  pl.BlockSpec((None, 1, d), lambda i: (layer, 0, 0)),
                  pl.BlockSpec((None, None, 6, d), lambda i: (layer, (i * tm) // seq, 0, 0)),
                  pl.BlockSpec((None, d, 128), lambda i: (idx, 0, 0))],
        out_specs=[pl.BlockSpec((tm, d), lambda i: (i, 0)), pl.BlockSpec((tm, 128), lambda i: (i, 0))],
        out_shape=[jax.ShapeDtypeStruct((n, d), F32), jax.ShapeDtypeStruct((n, 128), F32)],
        compiler_params=_params(1), name="moe_router")(x, norm_w, mod, router)


def _moe_plan(route, n_exp, tile):
    n = route.shape[0]
    ef = route[:, :2].astype(jnp.int32).reshape(-1)
    onehot = (ef[:, None] == jnp.arange(n_exp, dtype=jnp.int32)[None, :]).astype(jnp.int32)
    csum = jnp.cumsum(onehot, axis=0)
    rank = jnp.sum(onehot * (csum - 1), axis=-1)
    counts = csum[-1]
    padded = ((counts + tile - 1) // tile) * tile
    ends = jnp.cumsum(padded)
    dest = (ends - padded)[ef] + rank
    rows = 2 * n + n_exp * tile
    src = jnp.zeros((rows,), jnp.int32).at[dest].set(jnp.arange(2 * n, dtype=jnp.int32) // 2)
    tile_start = jnp.arange(rows // tile, dtype=jnp.int32) * tile
    tile_valid = (tile_start < ends[-1]).astype(jnp.int32)
    tile_expert = jnp.sum((ends[None, :] <= tile_start[:, None]).astype(jnp.int32), axis=1)
    tile_expert = jnp.minimum(tile_expert, n_exp - 1)
    last = jnp.maximum(ends[-1] // tile - 1, 0)
    tile_expert = jnp.where(tile_valid > 0, tile_expert, tile_expert[last])
    return dest, src, tile_expert, tile_valid


def _row_copy(src_hbm, row, dst_vmem, slot, sem):
    return pltpu.make_async_copy(src_hbm.at[pl.ds(row, 1), :], dst_vmem.at[pl.ds(slot, 1), :], sem)


def _experts_kernel(te_ref, tv_ref, src_ref, h_hbm, w1_ref, w3_ref, w2_ref, ys_ref,
                    rows_scr, xb_scr, acc_scr, sems):
    i = pl.program_id(0)
    f = pl.program_id(1)
    last = pl.num_programs(1) - 1
    tm = xb_scr.shape[0]
    valid = tv_ref[i] > 0
    slot = i % 2

    def issue(tile, sl):
        def body(r, carry):
            _row_copy(h_hbm, src_ref[tile * tm + r], rows_scr.at[sl], r, sems.at[sl]).start()
            return carry
        lax.fori_loop(0, tm, body, 0, unroll=8)

    def drain(sl):
        def body(r, carry):
            _row_copy(h_hbm, 0, rows_scr.at[sl], r, sems.at[sl]).wait()
            return carry
        lax.fori_loop(0, tm, body, 0, unroll=8)

    @pl.when(f == 0)
    def _():
        @pl.when(i == 0)
        def _():
            issue(0, 0)

        @pl.when(i + 1 < pl.num_programs(0))
        def _():
            issue(i + 1, 1 - slot)
        drain(slot)
        xb_scr[...] = rows_scr[slot].astype(BF16)
        acc_scr[...] = jnp.zeros(acc_scr.shape, F32)

    @pl.when(valid)
    def _():
        h = xb_scr[...]
        g = _silu(_dot(h, w1_ref[...])) * _dot(h, w3_ref[...])
        acc_scr[...] += _dot(g.astype(BF16), w2_ref[...])

    @pl.when(f == last)
    def _():
        ys_ref[...] = acc_scr[...]


def moe_experts(h, src, tile_expert, tile_valid, w1, w3, w2, idx, *, tf=256):
    n, d = h.shape
    rows = src.shape[0]
    dff = w1.shape[-1]
    tm = MOE_TILE
    nf = dff // tf
    fcol = lambda i, f, tv: jnp.where(tv[i] > 0, f, nf - 1)
    return pl.pallas_call(
        _experts_kernel,
        grid_spec=pltpu.PrefetchScalarGridSpec(
            num_scalar_prefetch=3, grid=(rows // tm, nf),
            in_specs=[pl.BlockSpec(memory_space=pl.ANY),
                      pl.BlockSpec((None, None, d, tf), lambda i, f, te, tv, src: (idx, te[i], 0, fcol(i, f, tv))),
                      pl.BlockSpec((None, None, d, tf), lambda i, f, te, tv, src: (idx, te[i], 0, fcol(i, f, tv))),
                      pl.BlockSpec((None, None, tf, d), lambda i, f, te, tv, src: (idx, te[i], fcol(i, f, tv), 0))],
            out_specs=pl.BlockSpec((tm, d), lambda i, f, te, tv, src: (i, 0)),
            scratch_shapes=[pltpu.VMEM((2, tm, d), F32), pltpu.VMEM((tm, d), BF16), pltpu.VMEM((tm, d), F32),
                            pltpu.SemaphoreType.DMA((2,))]),
        out_shape=jax.ShapeDtypeStruct((rows, d), F32),
        compiler_params=_params(2), name="moe_experts")(tile_expert, tile_valid, src, h, w1, w3, w2)


def _moe_combine_kernel(dest_ref, ys_hbm, x_ref, route_ref, mod_ref, o_ref, buf0, buf1, sems):
    tc = x_ref.shape[0]
    i = pl.program_id(0)
    slot = i % 2

    def issue(tile, sl):
        def body(t, carry):
            a = 2 * (tile * tc + t)
            _row_copy(ys_hbm, dest_ref[a], buf0.at[sl], t, sems.at[sl]).start()
            _row_copy(ys_hbm, dest_ref[a + 1], buf1.at[sl], t, sems.at[sl]).start()
            return carry
        lax.fori_loop(0, tc, body, 0, unroll=4)

    def drain(sl):
        def body(t, carry):
            _row_copy(ys_hbm, 0, buf0.at[sl], t, sems.at[sl]).wait()
            _row_copy(ys_hbm, 0, buf1.at[sl], t, sems.at[sl]).wait()
            return carry
        lax.fori_loop(0, tc, body, 0, unroll=4)

    @pl.when(i == 0)
    def _():
        issue(0, 0)

    @pl.when(i + 1 < pl.num_programs(0))
    def _():
        issue(i + 1, 1 - slot)
    drain(slot)
    route = route_ref[...]
    y = route[:, 2:3] * buf0[slot] + route[:, 3:4] * buf1[slot]
    o_ref[...] = x_ref[...] + mod_ref[5:6, :] * y


def moe_combine(ys, dest, x, route, mod, layer, seq, *, tc=256):
    n, d = x.shape
    return pl.pallas_call(
        _moe_combine_kernel,
        grid_spec=pltpu.PrefetchScalarGridSpec(
            num_scalar_prefetch=1, grid=(n // tc,),
            in_specs=[pl.BlockSpec(memory_space=pl.ANY),
                      pl.BlockSpec((tc, d), lambda i, dest: (i, 0)),
                      pl.BlockSpec((tc, 128), lambda i, dest: (i, 0)),
                      pl.BlockSpec((None, None, 6, d), lambda i, dest: (layer, (i * tc) // seq, 0, 0))],
            out_specs=pl.BlockSpec((tc, d), lambda i, dest: (i, 0)),
            scratch_shapes=[pltpu.VMEM((2, tc, d), F32), pltpu.VMEM((2, tc, d), F32),
                            pltpu.SemaphoreType.DMA((2,))]),
        out_shape=jax.ShapeDtypeStruct((n, d), F32),
        compiler_params=_params(1), name="moe_combine")(dest, ys, x, route, mod)


def moe_residual(x, norm_w, mod, router, w1, w3, w2, layer, idx, seq):
    h, route = moe_router(x, norm_w, mod, router, layer, idx, seq)
    dest, src, tile_expert, tile_valid = _moe_plan(route, w1.shape[1], MOE_TILE)
    ys = moe_experts(h, src, tile_expert, tile_valid, w1, w3, w2, idx)
    return moe_combine(ys, dest, x, route, mod, layer, seq)


def _final_norm_kernel(x_ref, w_ref, o_ref):
    x = x_ref[...]
    o_ref[...] = x * lax.rsqrt(jnp.mean(x * x, axis=-1, keepdims=True) + EPS) * w_ref[...]


def final_norm(x, w, *, tm=512):
    n, d = x.shape
    return pl.pallas_call(
        _final_norm_kernel, grid=(n // tm,),
        in_specs=[pl.BlockSpec((tm, d), lambda i: (i, 0)), pl.BlockSpec((1, d), lambda i: (0, 0))],
        out_specs=pl.BlockSpec((tm, d), lambda i: (i, 0)),
        out_shape=jax.ShapeDtypeStruct((n, d), F32),
        compiler_params=_params(1), name="final_norm")(x, w.reshape(1, d))


def _reorder_w_in(w_in):
    sizes = (512, 256, 256, 512, 128, 128, 128, 128, 128, 128, 12, 512, 512, 1024, 4, 4, 1024)
    (aq, ak, av, bq, bkc, bvc, bks, bvs, bkw, bvw, bg, cq, ck, cv, ci, cf, co) = jnp.split(
        w_in, np.cumsum(sizes)[:-1].tolist(), axis=-1)
    pad = lambda n: jnp.zeros(w_in.shape[:-1] + (n,), w_in.dtype)
    out = jnp.concatenate([aq, ak, av, cq, ck, cv, co, bq, bkc, bvc, bks, bvs, bkw, bvw,
                           bg, ci, cf, pad(128 - 20), pad(PROJ_COLS - COL_SMALL - 128)], axis=-1)
    assert out.shape[-1] == PROJ_COLS
    return out.astype(BF16)


def hybrid_mixer(proj, tables, layer, a_sinks, mlstm_gate_b, nsa_pe_k, nsa_pe_v, nsa_ck_w1, nsa_ck_w2,
                 nsa_cv_w1, nsa_cv_w2, mlstm_conv_w, mlstm_norm_w):
    bsz, t, _ = proj.shape
    o_a = sliding_window_attention(proj, col_q=COL_AQ, col_k=COL_AK, col_v=COL_AV, n_heads=A_HEADS,
                                   n_kv=A_KV_HEADS, window=A_WINDOW, tq=256, tables=tables,
                                   sinks=a_sinks[layer])
    chunks = lambda col: proj[:, :, col:col + HEAD_DIM].reshape(bsz, t // CMP_STRIDE, CMP_STRIDE * HEAD_DIM)
    k_cmp = compress_blocks(chunks(COL_BKC), nsa_pe_k, nsa_ck_w1, nsa_ck_w2, layer)
    v_cmp = compress_blocks(chunks(COL_BVC), nsa_pe_v, nsa_cv_w1, nsa_cv_w2, layer)
    o_cmp, bias_t = compressed_attention_select(proj, k_cmp, v_cmp)
    o_sel = selected_attention(proj, bias_t)
    o_win = sliding_window_attention(proj, col_q=COL_BQ, col_k=COL_BKW, col_v=COL_BVW, n_heads=B_HEADS,
                                     n_kv=1, window=B_WINDOW, tq=512, out_dtype=F32)
    o_b = combine_branches(proj, o_cmp, o_sel, o_win)
    o_c = mlstm_mixer(proj, mlstm_gate_b, mlstm_conv_w, mlstm_norm_w, layer)
    return o_a, o_b, o_c


def kernel(x, c, positions, ada_w, ada_b, norm_mix_w, norm_ffn_w, w_in, mlstm_gate_b, a_sinks, nsa_pe_k,
           nsa_pe_v, nsa_ck_w1, nsa_ck_w2, nsa_cv_w1, nsa_cv_w2, mlstm_conv_w, mlstm_norm_w, w_out, ffn_w1,
           ffn_w3, ffn_w2, moe_router, moe_w1, moe_w3, moe_w2, final_norm_w):
    bsz, t, d = x.shape
    depth = ada_w.shape[0]
    mod = ada_modulation(c, ada_w, ada_b)
    tables = rope_tables(positions)
    w_in_p = _reorder_w_in(w_in)
    w_out_b = w_out.astype(BF16)
    ffn_b = [w.astype(BF16) for w in (ffn_w1, ffn_w3, ffn_w2)]
    moe_b = [w.astype(BF16) for w in (moe_w1, moe_w3, moe_w2)]
    router_p = jnp.pad(moe_router, ((0, 0), (0, 0), (0, 128 - N_EXPERTS)))
    norm_mix = norm_mix_w.reshape(depth, 1, d)
    norm_ffn = norm_ffn_w.reshape(depth, 1, d)
    conv_w = mlstm_conv_w
    norm_c = mlstm_norm_w.reshape(depth, 1, -1)

    xf = x.reshape(bsz * t, d)
    for layer in range(depth):
        proj = norm_proj(xf, norm_mix, mod, w_in_p, layer, t, shift_row=0, scale_row=1)
        o_a, o_b, o_c = hybrid_mixer(proj.reshape(bsz, t, PROJ_COLS), tables, layer, a_sinks, mlstm_gate_b,
                                     nsa_pe_k, nsa_pe_v, nsa_ck_w1, nsa_ck_w2, nsa_cv_w1, nsa_cv_w2,
                                     conv_w, norm_c)
        flat = lambda a: a.reshape(bsz * t, a.shape[-1])
        xf = out_proj_residual(flat(o_a), flat(o_b), flat(o_c), w_out_b, xf, mod, layer, t)
        if layer % 2 == 0:
            xf = ffn_residual(xf, norm_ffn, mod, *ffn_b, layer, layer // 2, t)
        else:
            xf = moe_residual(xf, norm_ffn, mod, router_p, *moe_b, layer, layer // 2, t)
    return final_norm(xf, final_norm_w).reshape(bsz, t, d)
```

```python
import functools
import math

import numpy as np
import jax
import jax.numpy as jnp
from jax import lax
from jax.experimental import pallas as pl
from jax.experimental.pallas import tpu as pltpu

F32 = jnp.float32
BF16 = jnp.bfloat16

HEAD_DIM = 128
A_HEADS, A_KV_HEADS, A_WINDOW = 4, 2, 128
ROPE_THETA = 150000.0
B_HEADS = 4
CMP_LEN, CMP_STRIDE = 32, 16
SEL_LEN, SEL_TOPK, B_WINDOW = 64, 16, 512
C_HEADS, C_QK_DIM, C_V_DIM, C_CONV = 4, 128, 256, 4
N_EXPERTS = 8
EPS = 1e-6
BIG = 1e9
NEG = -1e30
SCALE = HEAD_DIM ** -0.5

COL_AQ, COL_AK, COL_AV = 0, 512, 768
COL_CQK, COL_CV, COL_CO = 1024, 2048, 3072
COL_BQ = 4096
COL_BKC, COL_BVC, COL_BKS, COL_BVS, COL_BKW, COL_BVW = 4608, 4736, 4864, 4992, 5120, 5248
COL_SMALL = 5376
LANE_CI, LANE_CF = 12, 16
PROJ_COLS = 5632

VMEM_LIMIT_MB = 56
BF16_SUBLANES = 16


def _params(n_axes, vmem_mb=VMEM_LIMIT_MB):
    return pltpu.CompilerParams(dimension_semantics=("arbitrary",) * n_axes,
                                vmem_limit_bytes=vmem_mb * 2 ** 20)


def _dot(a, b):
    return jnp.dot(a, b, preferred_element_type=F32)


def _dot_nt(a, b):
    return lax.dot_general(a, b, (((1,), (1,)), ((), ())), preferred_element_type=F32)


def _silu(v):
    return v * jax.nn.sigmoid(v)


def _norm_mod(x, nw, shift, scale):
    ms = jnp.mean(x * x, axis=-1, keepdims=True)
    return (x * lax.rsqrt(ms + EPS) * nw) * (1.0 + scale) + shift


def _ada_kernel(c_ref, w_ref, b_ref, o_ref):
    act = _silu(c_ref[...]).astype(BF16)
    o_ref[...] = _dot(act, w_ref[...].astype(BF16)) + b_ref[...]


def ada_modulation(c, ada_w, ada_b):
    depth, d, n6 = ada_w.shape
    bsz = c.shape[0]
    assert bsz <= 8
    cp = jnp.zeros((8, d), F32).at[:bsz].set(c)
    tn = 512
    out = pl.pallas_call(
        _ada_kernel, grid=(depth, n6 // tn),
        in_specs=[pl.BlockSpec((8, d), lambda l, j: (0, 0)),
                  pl.BlockSpec((None, d, tn), lambda l, j: (l, 0, j)),
                  pl.BlockSpec((None, 1, tn), lambda l, j: (l, 0, j))],
        out_specs=pl.BlockSpec((None, 8, tn), lambda l, j: (l, 0, j)),
        out_shape=jax.ShapeDtypeStruct((depth, 8, n6), F32),
        compiler_params=_params(2), name="ada_modulation")(cp, ada_w, ada_b.reshape(depth, 1, n6))
    return out.reshape(depth, 8, 6, d)


def _rope_table_kernel(pos_ref, cos_ref, sin_ref):
    pos = pos_ref[...].astype(F32)
    lane = lax.broadcasted_iota(jnp.int32, (1, HEAD_DIM), 1)
    half = HEAD_DIM // 2
    inv = jnp.exp((lane & (half - 1)).astype(F32) * (-math.log(ROPE_THETA) * 2.0 / HEAD_DIM))
    ang = pos * inv
    cos_ref[...] = jnp.cos(ang)
    sin_ref[...] = jnp.where(lane < half, -1.0, 1.0) * jnp.sin(ang)


def rope_tables(positions):
    bsz, t = positions.shape
    tt = 512
    spec = pl.BlockSpec((None, tt, HEAD_DIM), lambda b, i: (b, i, 0))
    return pl.pallas_call(
        _rope_table_kernel, grid=(bsz, t // tt),
        in_specs=[pl.BlockSpec((None, tt, 1), lambda b, i: (b, i, 0))],
        out_specs=[spec, spec],
        out_shape=[jax.ShapeDtypeStruct((bsz, t, HEAD_DIM), F32)] * 2,
        compiler_params=_params(2), name="rope_tables")(positions.reshape(bsz, t, 1))


def _proj_kernel(x_ref, nw_ref, mod_ref, w_ref, o_ref, h_scr, *, shift_row, scale_row):
    @pl.when(pl.program_id(1) == 0)
    def _():
        h = _norm_mod(x_ref[...], nw_ref[...], mod_ref[shift_row:shift_row + 1, :],
                      mod_ref[scale_row:scale_row + 1, :])
        h_scr[...] = h.astype(BF16)

    o_ref[...] = _dot(h_scr[...], w_ref[...])


def norm_proj(x, norm_w, mod, w, layer, seq, *, shift_row, scale_row, tm=1024, tn=1408):
    n, d = x.shape
    p = w.shape[-1]
    assert seq % tm == 0 and p % tn == 0
    return pl.pallas_call(
        functools.partial(_proj_kernel, shift_row=shift_row, scale_row=scale_row),
        grid=(n // tm, p // tn),
        in_specs=[pl.BlockSpec((tm, d), lambda i, j: (i, 0)),
                  pl.BlockSpec((None, 1, d), lambda i, j: (layer, 0, 0)),
                  pl.BlockSpec((None, None, 6, d), lambda i, j: (layer, (i * tm) // seq, 0, 0)),
                  pl.BlockSpec((None, d, tn), lambda i, j: (layer, 0, j))],
        out_specs=pl.BlockSpec((tm, tn), lambda i, j: (i, j)),
        out_shape=jax.ShapeDtypeStruct((n, p), F32),
        scratch_shapes=[pltpu.VMEM((tm, d), BF16)],
        compiler_params=_params(2), name="norm_proj")(x, norm_w, mod, w)


def _swa_kernel(*refs, tq, window, n_heads, group, rope, sinks):
    it = iter(refs)
    q_ref, kp_ref, kc_ref, vp_ref, vc_ref = (next(it) for _ in range(5))
    if rope:
        cp_ref, sp_ref, cc_ref, sc_ref = (next(it) for _ in range(4))
    if sinks:
        sink_ref = next(it)
    o_ref = next(it)
    i = pl.program_id(1)
    row = lax.broadcasted_iota(jnp.int32, (tq, 2 * tq), 0)
    col = lax.broadcasted_iota(jnp.int32, (tq, 2 * tq), 1)
    diff = row + tq - col + jnp.where(col < tq, jnp.where(i > 0, 0, window), 0)
    mask = jnp.abs(2 * diff - (window - 1)) <= (window - 1)

    def rotate(v, c, s):
        return v * c + pltpu.roll(v, HEAD_DIM // 2, 1) * s

    kv_cache = {}
    for hq in range(n_heads):
        kv = hq // group
        if kv not in kv_cache:
            sl = slice(kv * HEAD_DIM, (kv + 1) * HEAD_DIM)
            kp, kc = kp_ref[:, sl], kc_ref[:, sl]
            if rope:
                kp = rotate(kp, cp_ref[...], sp_ref[...])
                kc = rotate(kc, cc_ref[...], sc_ref[...])
            k = jnp.concatenate([kp, kc], axis=0).astype(BF16)
            v = jnp.concatenate([vp_ref[:, sl], vc_ref[:, sl]], axis=0).astype(BF16)
            kv_cache[kv] = (k, v)
        k, v = kv_cache[kv]
        hs = slice(hq * HEAD_DIM, (hq + 1) * HEAD_DIM)
        q = q_ref[:, hs]
        if rope:
            q = rotate(q, cc_ref[...], sc_ref[...])
        s = _dot_nt((q * SCALE).astype(BF16), k)
        s = jnp.where(mask, s, NEG)
        m = jnp.max(s, axis=-1, keepdims=True)
        if sinks:
            m = jnp.maximum(m, sink_ref[hq])
        m = jnp.where(m < 0.5 * NEG, 0.0, m)
        p = jnp.exp(s - m)
        den = jnp.sum(p, axis=-1, keepdims=True)
        if sinks:
            den = den + jnp.exp(sink_ref[hq] - m)
        o = _dot(p.astype(BF16), v) / jnp.maximum(den, 1e-30)
        o_ref[:, hs] = o.astype(o_ref.dtype)


def sliding_window_attention(proj, *, col_q, col_k, col_v, n_heads, n_kv, window, tq,
                             tables=None, sinks=None, out_dtype=BF16):
    bsz, t, _ = proj.shape
    assert window - 1 <= tq and t % tq == 0
    wq, wkv = n_heads * HEAD_DIM, n_kv * HEAD_DIM
    assert col_q % wq == 0 and col_k % wkv == 0 and col_v % wkv == 0
    cur = lambda b, i: (b, i, 0)
    prev = lambda b, i: (b, jnp.maximum(i - 1, 0), 0)
    kvspec = lambda col, im: pl.BlockSpec(
        (None, tq, wkv), lambda b, i: (b, im(b, i)[1], col // wkv))
    in_specs = [pl.BlockSpec((None, tq, wq), lambda b, i: (b, i, col_q // wq)),
                kvspec(col_k, prev), kvspec(col_k, cur), kvspec(col_v, prev), kvspec(col_v, cur)]
    args = [proj] * 5
    if tables is not None:
        cos, sin = tables
        tp = pl.BlockSpec((None, tq, HEAD_DIM), prev)
        tc = pl.BlockSpec((None, tq, HEAD_DIM), cur)
        in_specs += [tp, tp, tc, tc]
        args += [cos, sin, cos, sin]
    if sinks is not None:
        in_specs.append(pl.BlockSpec(memory_space=pltpu.SMEM))
        args.append(sinks)
    return pl.pallas_call(
        functools.partial(_swa_kernel, tq=tq, window=window, n_heads=n_heads, group=n_heads // n_kv,
                          rope=tables is not None, sinks=sinks is not None),
        grid=(bsz, t // tq), in_specs=in_specs,
        out_specs=pl.BlockSpec((None, tq, wq), cur),
        out_shape=jax.ShapeDtypeStruct((bsz, t, wq), out_dtype),
        compiler_params=_params(2), name="sliding_window_attention")(*args)


def _compress_kernel(r_ref, pe_ref, w1_ref, w2_ref, o_ref):
    n16, half = r_ref.shape
    r = r_ref[...].astype(BF16)
    w1 = w1_ref[...].astype(BF16)
    first = _dot(r, w1[:half])
    second = _dot(r, w1[half:])
    pe = _dot(pe_ref[...].astype(BF16), w1)[0:1]
    pre = first + pltpu.roll(second, n16 - 1, 0) + pe
    hid = jax.nn.gelu(pre, approximate=True)
    o_ref[...] = _dot(hid.astype(BF16), w2_ref[...].astype(BF16))


def compress_blocks(chunks, pe, w1, w2, layer):
    bsz, n16, half = chunks.shape
    hid = w1.shape[-1]
    pe8 = jnp.broadcast_to(pe[layer].reshape(1, 2 * half), (8, 2 * half))
    return pl.pallas_call(
        _compress_kernel, grid=(bsz,),
        in_specs=[pl.BlockSpec((None, n16, half), lambda b: (b, 0, 0)),
                  pl.BlockSpec((8, 2 * half), lambda b: (0, 0)),
                  pl.BlockSpec((None, 2 * half, hid), lambda b: (layer, 0, 0)),
                  pl.BlockSpec((None, hid, HEAD_DIM), lambda b: (layer, 0, 0))],
        out_specs=pl.BlockSpec((None, n16, HEAD_DIM), lambda b: (b, 0, 0)),
        out_shape=jax.ShapeDtypeStruct((bsz, n16, HEAD_DIM), F32),
        compiler_params=_params(1), name="compress_blocks")(chunks, pe8, w1, w2)


def _heads_transposed(q, tq, scale=SCALE):
    n = q.shape[1] // HEAD_DIM
    return jnp.concatenate([(q[:, g * HEAD_DIM:(g + 1) * HEAD_DIM] * scale).T for g in range(n)],
                           axis=1).astype(BF16)


def _cmp_select_kernel(q_ref, kc_ref, vc_ref, ov_ref, ocmp_ref, bias_ref, *, tq, n_sel, topk):
    i = pl.program_id(1)
    nc = kc_ref.shape[0]
    hq = B_HEADS * tq
    qt = _heads_transposed(q_ref[...], tq)
    st = _dot(kc_ref[...].astype(BF16), qt)
    n_row = lax.broadcasted_iota(jnp.int32, (nc, hq), 0)
    t_col = i * tq + (lax.broadcasted_iota(jnp.int32, (nc, hq), 1) & (tq - 1))
    st = jnp.where(n_row * CMP_STRIDE + (CMP_LEN - 1) <= t_col, st, NEG)
    m = jnp.max(st, axis=0, keepdims=True)
    m = jnp.where(m < 0.5 * NEG, 0.0, m)
    p = jnp.exp(st - m)
    p = p / jnp.maximum(jnp.sum(p, axis=0, keepdims=True), 1e-30)
    pb = p.astype(BF16)
    ot = _dot(vc_ref[...].T.astype(BF16), pb)
    for g in range(B_HEADS):
        ocmp_ref[:, g * HEAD_DIM:(g + 1) * HEAD_DIM] = ot[:, g * tq:(g + 1) * tq].T
    imp4 = _dot(ov_ref[...], pb)
    imp = imp4[:, 0:tq]
    for g in range(1, B_HEADS):
        imp = imp + imp4[:, g * tq:(g + 1) * tq]
    nsp = imp.shape[0]
    j = lax.broadcasted_iota(jnp.int32, (nsp, tq), 0)
    t = i * tq + lax.broadcasted_iota(jnp.int32, (nsp, tq), 1)
    cur = t >> 6
    imp = jnp.where((j << 6) <= t, imp, -BIG)
    for forced in (0, cur, cur - 1):
        imp = jnp.where(j == forced, BIG, imp)
    imp = jnp.where(j < n_sel, imp, -3e38)
    sel = jnp.zeros((nsp, tq), F32)
    for _ in range(topk):
        mx = jnp.max(imp, axis=0, keepdims=True)
        idx = jnp.min(jnp.where(imp == mx, j, nsp), axis=0, keepdims=True)
        hit = j == idx
        sel = jnp.where(hit, 1.0, sel)
        imp = jnp.where(hit, -3e38, imp)
    bias_ref[...] = jnp.where(sel > 0.0, 0.0, jnp.where(j < n_sel, -BIG, 0.0))


def _overlap_matrix_t(t):
    n16, n_sel = t // CMP_STRIDE, t // SEL_LEN
    nsp = -(-n_sel // 128) * 128
    starts = np.arange(n16) * CMP_STRIDE
    sel_start = np.arange(n_sel) * SEL_LEN
    ov = np.clip(np.minimum(starts[:, None] + CMP_LEN, sel_start[None, :] + SEL_LEN)
                 - np.maximum(starts[:, None], sel_start[None, :]), 0, None).astype(np.float32) / CMP_LEN
    out = np.zeros((nsp, n16), np.float32)
    out[:n_sel] = ov.T
    return jnp.asarray(out, BF16)


def compressed_attention_select(proj, k_cmp, v_cmp, *, tq=256):
    bsz, t, _ = proj.shape
    n16, n_sel = t // CMP_STRIDE, t // SEL_LEN
    assert SEL_LEN == 64 and tq % 128 == 0 and tq & (tq - 1) == 0 and t % tq == 0
    ov = _overlap_matrix_t(t)
    nsp = ov.shape[0]
    wq = B_HEADS * HEAD_DIM
    return pl.pallas_call(
        functools.partial(_cmp_select_kernel, tq=tq, n_sel=n_sel, topk=min(SEL_TOPK, n_sel)),
        grid=(bsz, t // tq),
        in_specs=[pl.BlockSpec((None, tq, wq), lambda b, i: (b, i, COL_BQ // wq)),
                  pl.BlockSpec((None, n16, HEAD_DIM), lambda b, i: (b, 0, 0)),
                  pl.BlockSpec((None, n16, HEAD_DIM), lambda b, i: (b, 0, 0)),
                  pl.BlockSpec((nsp, n16), lambda b, i: (0, 0))],
        out_specs=[pl.BlockSpec((None, tq, wq), lambda b, i: (b, i, 0)),
                   pl.BlockSpec((None, nsp, tq), lambda b, i: (b, 0, i))],
        out_shape=[jax.ShapeDtypeStruct((bsz, t, wq), F32),
                   jax.ShapeDtypeStruct((bsz, nsp, t), F32)],
        compiler_params=_params(2), name="compressed_attention_select")(proj, k_cmp, v_cmp, ov)


def _selected_kernel(q_ref, bias_ref, ks_ref, vs_ref, o_ref, kaug_scr, vt_scr, qa_scr, s0_scr, s1_scr,
                     m_scr, acc_scr, *, tq, kb):
    i = pl.program_id(1)
    t_all = ks_ref.shape[0]
    nsp = bias_ref.shape[0]
    hq = B_HEADS * tq
    ones_rows = vt_scr.shape[1] - HEAD_DIM

    @pl.when(i == 0)
    def _():
        def prep(c, carry):
            r0 = pl.multiple_of(c * kb, kb)
            kaug_scr[pl.ds(r0, kb), 0:HEAD_DIM] = ks_ref[pl.ds(r0, kb), :].astype(BF16)
            key = r0 + lax.broadcasted_iota(jnp.int32, (kb, nsp), 0)
            blk = lax.broadcasted_iota(jnp.int32, (kb, nsp), 1)
            kaug_scr[pl.ds(r0, kb), HEAD_DIM:] = jnp.where((key >> 6) == blk, 1.0, 0.0).astype(BF16)
            vt_scr[c] = jnp.concatenate([vs_ref[pl.ds(r0, kb), :].T, jnp.ones((ones_rows, kb), F32)],
                                        axis=0).astype(BF16)
            return carry
        lax.fori_loop(0, t_all // kb, prep, 0)

    bias = bias_ref[...]
    qa_scr[...] = jnp.concatenate([_heads_transposed(q_ref[...], tq, SCALE * math.log2(math.e)),
                                   jnp.concatenate([bias] * B_HEADS, axis=1).astype(BF16)], axis=0)
    m_scr[...] = jnp.full(m_scr.shape, NEG, F32)
    acc_scr[...] = jnp.zeros(acc_scr.shape, F32)

    def scores(kt, s_out):
        r0 = pl.multiple_of(kt * kb, kb)
        s_out[...] = _dot(kaug_scr[pl.ds(r0, kb), :], qa_scr[...])

    def consume(kt, s_in, causal):
        st = s_in[...]
        if causal:
            key = kt * kb + lax.broadcasted_iota(jnp.int32, (kb, hq), 0)
            tpos = i * tq + (lax.broadcasted_iota(jnp.int32, (kb, hq), 1) & (tq - 1))
            st = jnp.where(key <= tpos, st, NEG)
        m_old = m_scr[...]
        m_new = jnp.maximum(m_old, jnp.max(st, axis=0, keepdims=True))
        p = jnp.exp2((st - m_new).astype(BF16))
        acc_scr[...] = jnp.exp2(m_old - m_new) * acc_scr[...] + _dot(vt_scr[kt], p)
        m_scr[...] = m_new

    def step(kt, s_in, s_out):
        scores(kt + 1, s_out)
        consume(kt, s_in, False)

    def finish(kt, s_in):
        consume(kt, s_in, True)
        ot = acc_scr[0:HEAD_DIM, :] / jnp.maximum(acc_scr[HEAD_DIM:HEAD_DIM + 1, :], 1e-30)
        for g in range(B_HEADS):
            o_ref[:, g * HEAD_DIM:(g + 1) * HEAD_DIM] = ot[:, g * tq:(g + 1) * tq].T

    n_full = (i * tq) // kb
    scores(0, s0_scr)

    def pair(j, carry):
        step(2 * j, s0_scr, s1_scr)
        step(2 * j + 1, s1_scr, s0_scr)
        return carry
    lax.fori_loop(0, n_full // 2, pair, 0)

    @pl.when(n_full % 2 == 1)
    def _():
        step(n_full - 1, s0_scr, s1_scr)
        finish(n_full, s1_scr)

    @pl.when(n_full % 2 == 0)
    def _():
        finish(n_full, s0_scr)


def selected_attention(proj, bias_t, *, tq=128, kb=512):
    bsz, t, _ = proj.shape
    nsp = bias_t.shape[1]
    wq = B_HEADS * HEAD_DIM
    assert kb % tq == 0 and t % kb == 0
    return pl.pallas_call(
        functools.partial(_selected_kernel, tq=tq, kb=kb),
        grid=(bsz, t // tq),
        in_specs=[pl.BlockSpec((None, tq, wq), lambda b, i: (b, i, COL_BQ // wq)),
                  pl.BlockSpec((None, nsp, tq), lambda b, i: (b, 0, i)),
                  pl.BlockSpec((None, t, HEAD_DIM), lambda b, i: (b, 0, COL_BKS // HEAD_DIM)),
                  pl.BlockSpec((None, t, HEAD_DIM), lambda b, i: (b, 0, COL_BVS // HEAD_DIM))],
        out_specs=pl.BlockSpec((None, tq, wq), lambda b, i: (b, i, 0)),
        out_shape=jax.ShapeDtypeStruct((bsz, t, wq), F32),
        scratch_shapes=[pltpu.VMEM((t, HEAD_DIM + nsp), BF16),
                        pltpu.VMEM((t // kb, HEAD_DIM + BF16_SUBLANES, kb), BF16),
                        pltpu.VMEM((HEAD_DIM + nsp, B_HEADS * tq), BF16),
                        pltpu.VMEM((kb, B_HEADS * tq), F32),
                        pltpu.VMEM((kb, B_HEADS * tq), F32),
                        pltpu.VMEM((1, B_HEADS * tq), F32),
                        pltpu.VMEM((HEAD_DIM + BF16_SUBLANES, B_HEADS * tq), F32)],
        compiler_params=_params(2), name="selected_attention")(proj, bias_t, proj, proj)


def _combine_kernel(g_ref, cmp_ref, sel_ref, win_ref, o_ref):
    g = jax.nn.sigmoid(g_ref[...])
    for h in range(B_HEADS):
        hs = slice(h * HEAD_DIM, (h + 1) * HEAD_DIM)
        o = (g[:, 3 * h:3 * h + 1] * cmp_ref[:, hs] + g[:, 3 * h + 1:3 * h + 2] * sel_ref[:, hs]
             + g[:, 3 * h + 2:3 * h + 3] * win_ref[:, hs])
        o_ref[:, hs] = o.astype(o_ref.dtype)


def combine_branches(proj, o_cmp, o_sel, o_win, *, tq=512):
    bsz, t, _ = proj.shape
    wq = B_HEADS * HEAD_DIM
    spec = pl.BlockSpec((None, tq, wq), lambda b, i: (b, i, 0))
    return pl.pallas_call(
        _combine_kernel, grid=(bsz, t // tq),
        in_specs=[pl.BlockSpec((None, tq, 128), lambda b, i: (b, i, COL_SMALL // 128)), spec, spec, spec],
        out_specs=spec, out_shape=jax.ShapeDtypeStruct((bsz, t, wq), BF16),
        compiler_params=_params(2), name="combine_branches")(proj, o_cmp, o_sel, o_win)


def _mlstm_kernel(gb_ref, qk_ref, v_ref, og_ref, small_ref, convw_ref, normw_ref, o_ref,
                  hist_scr, c_scr, n_scr, m_scr, *, chunk):
    L = chunk
    dqk, dv = C_QK_DIM, C_V_DIM

    @pl.when(pl.program_id(1) == 0)
    def _():
        hist_scr[...] = jnp.zeros(hist_scr.shape, F32)
        c_scr[...] = jnp.zeros(c_scr.shape, F32)
        n_scr[...] = jnp.zeros(n_scr.shape, F32)
        m_scr[...] = jnp.zeros(m_scr.shape, F32)

    x = qk_ref[...]
    xe = jnp.concatenate([hist_scr[...], x], axis=0)
    w = convw_ref[...]
    y = w[0:1] * xe[8 - 3:8 - 3 + L]
    for tap in range(1, C_CONV):
        y = y + w[tap:tap + 1] * xe[8 - 3 + tap:8 - 3 + tap + L]
    hist_scr[...] = x[L - 8:L]
    qk = _silu(y)

    lane = lax.broadcasted_iota(jnp.int32, (1, 128), 1)
    gbias = jnp.zeros((1, 128), F32)
    for idx in range(2 * C_HEADS):
        gbias = jnp.where(lane == LANE_CI + idx, gb_ref[idx], gbias)
    pre = small_ref[...] + gbias
    logf = jnp.minimum(pre, 0.0) - jnp.log1p(jnp.exp(-jnp.abs(pre)))
    row = lax.broadcasted_iota(jnp.int32, (L, L), 0)
    col = lax.broadcasted_iota(jnp.int32, (L, L), 1)
    causal = row >= col
    tri = jnp.where(causal, 1.0, 0.0)
    hp = lax.Precision.HIGHEST
    b_cols = jnp.dot(tri, logf, precision=hp, preferred_element_type=F32)
    b_rows = lax.dot_general(logf.T, tri, (((1,), (1,)), ((), ())), precision=hp,
                             preferred_element_type=F32)
    pre_t = pre.T

    for h in range(C_HEADS):
        q = qk[:, h * dqk:(h + 1) * dqk]
        k = qk[:, C_HEADS * dqk + h * dqk:C_HEADS * dqk + (h + 1) * dqk] * (dqk ** -0.5)
        v = v_ref[:, h * dv:(h + 1) * dv].astype(BF16)
        b_col = b_cols[:, LANE_CF + h:LANE_CF + h + 1]
        i_col = pre[:, LANE_CI + h:LANE_CI + h + 1]
        b_row = b_rows[LANE_CF + h:LANE_CF + h + 1, :]
        i_row = pre_t[LANE_CI + h:LANE_CI + h + 1, :]
        m_prev = m_scr[h:h + 1, 0:1]
        ct = c_scr[h]
        n_row = n_scr[h:h + 1, :]

        d = jnp.where(causal, b_col - b_row + i_row, NEG)
        m_inter = b_col + m_prev
        m_t = jnp.maximum(m_inter, jnp.max(d, axis=-1, keepdims=True))
        qb = q.astype(BF16)
        s = _dot_nt(qb, k.astype(BF16)) * jnp.exp(d - m_t)
        inter = jnp.exp(m_inter - m_t)
        num = _dot(s.astype(BF16), v) + inter * _dot(qb, ct.astype(BF16))
        den = jnp.sum(s, axis=-1, keepdims=True) + inter * jnp.sum(q * n_row, axis=-1, keepdims=True)
        hh = num / jnp.maximum(jnp.abs(den), jnp.exp(-m_t))

        b_last = b_col[L - 1:L, :]
        g = b_last - b_col + i_col
        m_new = jnp.maximum(b_last + m_prev, jnp.max(g, axis=0, keepdims=True))
        kd = k * jnp.exp(g - m_new)
        keep = jnp.exp(b_last + m_prev - m_new)
        c_scr[h] = keep * ct + _dot(kd.T.astype(BF16), v)
        n_scr[h:h + 1, :] = keep * n_row + jnp.sum(kd, axis=0, keepdims=True)
        m_scr[h:h + 1, :] = jnp.broadcast_to(m_new, (1, 128))

        vs = slice(h * dv, (h + 1) * dv)
        hn = hh * lax.rsqrt(jnp.mean(hh * hh, axis=-1, keepdims=True) + EPS) * normw_ref[:, vs]
        o_ref[:, vs] = (hn * jax.nn.sigmoid(og_ref[:, vs])).astype(o_ref.dtype)


def mlstm_mixer(proj, gate_b, conv_w, norm_w, layer, *, chunk=128):
    bsz, t, _ = proj.shape
    wqk, wv = 2 * C_HEADS * C_QK_DIM, C_HEADS * C_V_DIM
    assert wqk == wv == 1024 and t % chunk == 0
    blk = lambda col: pl.BlockSpec((None, chunk, 1024), lambda b, i: (b, i, col // 1024))
    return pl.pallas_call(
        functools.partial(_mlstm_kernel, chunk=chunk),
        grid=(bsz, t // chunk),
        in_specs=[pl.BlockSpec(memory_space=pltpu.SMEM),
                  blk(COL_CQK), blk(COL_CV), blk(COL_CO),
                  pl.BlockSpec((None, chunk, 128), lambda b, i: (b, i, COL_SMALL // 128)),
                  pl.BlockSpec((None, C_CONV, wqk), lambda b, i: (layer, 0, 0)),
                  pl.BlockSpec((None, 1, wv), lambda b, i: (layer, 0, 0))],
        out_specs=pl.BlockSpec((None, chunk, wv), lambda b, i: (b, i, 0)),
        out_shape=jax.ShapeDtypeStruct((bsz, t, wv), BF16),
        scratch_shapes=[pltpu.VMEM((8, wqk), F32),
                        pltpu.VMEM((C_HEADS, C_QK_DIM, C_V_DIM), F32),
                        pltpu.VMEM((8, C_QK_DIM), F32),
                        pltpu.VMEM((8, 128), F32)],
        compiler_params=_params(2), name="mlstm_mixer")(
            gate_b[layer], proj, proj, proj, proj, conv_w, norm_w)


def _outproj_kernel(oa_ref, ob_ref, oc_ref, w_ref, x_ref, mod_ref, o_ref, *, gate_row):
    wa, wb = oa_ref.shape[1], ob_ref.shape[1]
    acc = _dot(oa_ref[...], w_ref[0:wa, :])
    acc = acc + _dot(ob_ref[...], w_ref[wa:wa + wb, :])
    acc = acc + _dot(oc_ref[...], w_ref[wa + wb:, :])
    o_ref[...] = x_ref[...] + mod_ref[gate_row:gate_row + 1, :] * acc


def out_proj_residual(o_a, o_b, o_c, w_out, x, mod, layer, seq, *, tm=1024, tn=1024):
    n, d = x.shape
    assert seq % tm == 0 and d % tn == 0
    wa, wb, wc = o_a.shape[1], o_b.shape[1], o_c.shape[1]
    return pl.pallas_call(
        functools.partial(_outproj_kernel, gate_row=2),
        grid=(n // tm, d // tn),
        in_specs=[pl.BlockSpec((tm, wa), lambda i, j: (i, 0)),
                  pl.BlockSpec((tm, wb), lambda i, j: (i, 0)),
                  pl.BlockSpec((tm, wc), lambda i, j: (i, 0)),
                  pl.BlockSpec((None, wa + wb + wc, tn), lambda i, j: (layer, 0, j)),
                  pl.BlockSpec((tm, tn), lambda i, j: (i, j)),
                  pl.BlockSpec((None, None, 6, tn), lambda i, j: (layer, (i * tm) // seq, 0, j))],
        out_specs=pl.BlockSpec((tm, tn), lambda i, j: (i, j)),
        out_shape=jax.ShapeDtypeStruct((n, d), F32),
        compiler_params=_params(2), name="out_proj_residual")(o_a, o_b, o_c, w_out, x, mod)


def _ffn_kernel(x_ref, nw_ref, mod_ref, w1_ref, w3_ref, w2_ref, o_ref, h_scr, acc_scr):
    f = pl.program_id(1)

    @pl.when(f == 0)
    def _():
        h = _norm_mod(x_ref[...], nw_ref[...], mod_ref[3:4, :], mod_ref[4:5, :])
        h_scr[...] = h.astype(BF16)
        acc_scr[...] = jnp.zeros(acc_scr.shape, F32)

    h = h_scr[...]
    g = _silu(_dot(h, w1_ref[...])) * _dot(h, w3_ref[...])
    acc_scr[...] += _dot(g.astype(BF16), w2_ref[...])

    @pl.when(f == pl.num_programs(1) - 1)
    def _():
        o_ref[...] = x_ref[...] + mod_ref[5:6, :] * acc_scr[...]


def ffn_residual(x, norm_w, mod, w1, w3, w2, layer, idx, seq, *, tm=512, tf=512):
    n, d = x.shape
    dff = w1.shape[-1]
    return pl.pallas_call(
        _ffn_kernel, grid=(n // tm, dff // tf),
        in_specs=[pl.BlockSpec((tm, d), lambda i, f: (i, 0)),
                  pl.BlockSpec((None, 1, d), lambda i, f: (layer, 0, 0)),
                  pl.BlockSpec((None, None, 6, d), lambda i, f: (layer, (i * tm) // seq, 0, 0)),
                  pl.BlockSpec((None, d, tf), lambda i, f: (idx, 0, f)),
                  pl.BlockSpec((None, d, tf), lambda i, f: (idx, 0, f)),
                  pl.BlockSpec((None, tf, d), lambda i, f: (idx, f, 0))],
        out_specs=pl.BlockSpec((tm, d), lambda i, f: (i, 0)),
        out_shape=jax.ShapeDtypeStruct((n, d), F32),
        scratch_shapes=[pltpu.VMEM((tm, d), BF16), pltpu.VMEM((tm, d), F32)],
        compiler_params=_params(2), name="ffn_residual")(x, norm_w, mod, w1, w3, w2)


MOE_TILE = 512


def _router_kernel(x_ref, nw_ref, mod_ref, r_ref, h_ref, route_ref):
    tm = x_ref.shape[0]
    lane = lax.broadcasted_iota(jnp.int32, (tm, 128), 1)
    h = _norm_mod(x_ref[...], nw_ref[...], mod_ref[3:4, :], mod_ref[4:5, :])
    h_ref[...] = h
    logits = jnp.dot(h, r_ref[...], precision=lax.Precision.HIGHEST, preferred_element_type=F32)
    logits = jnp.where(lane < N_EXPERTS, logits, NEG)
    v1 = jnp.max(logits, axis=-1, keepdims=True)
    i1 = jnp.min(jnp.where(logits == v1, lane, 128), axis=-1, keepdims=True)
    rest = jnp.where(lane == i1, NEG, logits)
    v2 = jnp.max(rest, axis=-1, keepdims=True)
    i2 = jnp.min(jnp.where(rest == v2, lane, 128), axis=-1, keepdims=True)
    e2 = jnp.exp(v2 - v1)
    route = jnp.where(lane == 0, i1.astype(F32), jnp.where(lane == 1, i2.astype(F32), 0.0))
    route = jnp.where(lane == 2, 1.0 / (1.0 + e2), jnp.where(lane == 3, e2 / (1.0 + e2), route))
    route_ref[...] = route


def moe_router(x, norm_w, mod, router, layer, idx, seq, *, tm=512):
    n, d = x.shape
    return pl.pallas_call(
        _router_kernel, grid=(n // tm,),
        in_specs=[pl.BlockSpec((tm, d), lambda i: (i, 0)),
                  pl.BlockSpec((None, 1, d), lambda i: (layer, 0, 0)),
                  pl.BlockSpec((None, None, 6, d), lambda i: (layer, (i * tm) // seq, 0, 0)),
                  pl.BlockSpec((None, d, 128), lambda i: (idx, 0, 0))],
        out_specs=[pl.BlockSpec((tm, d), lambda i: (i, 0)), pl.BlockSpec((tm, 128), lambda i: (i, 0))],
        out_shape=[jax.ShapeDtypeStruct((n, d), F32), jax.ShapeDtypeStruct((n, 128), F32)],
        compiler_params=_params(1), name="moe_router")(x, norm_w, mod, router)


def _moe_plan(route, n_exp, tile):
    n = route.shape[0]
    ef = route[:, :2].astype(jnp.int32).reshape(-1)
    onehot = (ef[:, None] == jnp.arange(n_exp, dtype=jnp.int32)[None, :]).astype(jnp.int32)
    csum = jnp.cumsum(onehot, axis=0)
    rank = jnp.sum(onehot * (csum - 1), axis=-1)
    counts = csum[-1]
    padded = ((counts + tile - 1) // tile) * tile
    ends = jnp.cumsum(padded)
    dest = (ends - padded)[ef] + rank
    rows = 2 * n + n_exp * tile
    src = jnp.zeros((rows,), jnp.int32).at[dest].set(jnp.arange(2 * n, dtype=jnp.int32) // 2)
    tile_start = jnp.arange(rows // tile, dtype=jnp.int32) * tile
    tile_valid = (tile_start < ends[-1]).astype(jnp.int32)
    tile_expert = jnp.sum((ends[None, :] <= tile_start[:, None]).astype(jnp.int32), axis=1)
    tile_expert = jnp.minimum(tile_expert, n_exp - 1)
    last = jnp.maximum(ends[-1] // tile - 1, 0)
    tile_expert = jnp.where(tile_valid > 0, tile_expert, tile_expert[last])
    return dest, src, tile_expert, tile_valid


def _row_copy(src_hbm, row, dst_vmem, slot, sem):
    return pltpu.make_async_copy(src_hbm.at[pl.ds(row, 1), :], dst_vmem.at[pl.ds(slot, 1), :], sem)


def _experts_kernel(te_ref, tv_ref, src_ref, h_hbm, w1_ref, w3_ref, w2_ref, ys_ref,
                    rows_scr, xb_scr, acc_scr, sems, *, n_tiles, nf):
    i = pl.program_id(0)
    f = pl.program_id(1)
    tm = xb_scr.shape[0]
    valid = tv_ref[i] > 0
    slot = i % 2
    per_step = -(-tm // (8 * nf)) * 8
    last_rows = tm - per_step * (nf - 1)
    assert 0 < last_rows <= per_step
    nxt = jnp.minimum(i + 1, n_tiles - 1)

    def drain(sl):
        def body(r, carry):
            _row_copy(h_hbm, 0, rows_scr.at[sl], r, sems.at[sl]).wait()
            return carry
        lax.fori_loop(0, tm, body, 0, unroll=8)

    @pl.when(f == 0)
    def _():
        @pl.when(i == 0)
        def _():
            def body(r, carry):
                _row_copy(h_hbm, src_ref[r], rows_scr.at[0], r, sems.at[0]).start()
                return carry
            lax.fori_loop(0, tm, body, 0, unroll=8)
        drain(slot)
        xb_scr[...] = rows_scr[slot].astype(BF16)
        acc_scr[...] = jnp.zeros(acc_scr.shape, F32)

    def request_next(count):
        first = f * per_step
        for r in range(count):
            _row_copy(h_hbm, src_ref[nxt * tm + first + r], rows_scr.at[1 - slot], first + r,
                      sems.at[1 - slot]).start()

    def matmuls():
        h = xb_scr[...]
        g = _silu(_dot(h, w1_ref[...])) * _dot(h, w3_ref[...])
        acc_scr[...] += _dot(g.astype(BF16), w2_ref[...])

    for final, count in ((False, per_step), (True, last_rows)):
        step = (f == nf - 1) if final else (f < nf - 1)

        @pl.when(step & valid)
        def _():
            request_next(count)
            matmuls()

        @pl.when(step & jnp.logical_not(valid))
        def _():
            request_next(count)

    @pl.when(f == nf - 1)
    def _():
        ys_ref[...] = acc_scr[...]

        @pl.when(i == n_tiles - 1)
        def _():
            drain(1 - slot)


def moe_experts(h, src, tile_expert, tile_valid, w1, w3, w2, idx, *, tf=256):
    n, d = h.shape
    rows = src.shape[0]
    dff = w1.shape[-1]
    tm = MOE_TILE
    nf = dff // tf
    fcol = lambda i, f, tv: jnp.where(tv[i] > 0, f, nf - 1)
    return pl.pallas_call(
        functools.partial(_experts_kernel, n_tiles=rows // tm, nf=nf),
        grid_spec=pltpu.PrefetchScalarGridSpec(
            num_scalar_prefetch=3, grid=(rows // tm, nf),
            in_specs=[pl.BlockSpec(memory_space=pl.ANY),
                      pl.BlockSpec((None, None, d, tf), lambda i, f, te, tv, src: (idx, te[i], 0, fcol(i, f, tv))),
                      pl.BlockSpec((None, None, d, tf), lambda i, f, te, tv, src: (idx, te[i], 0, fcol(i, f, tv))),
                      pl.BlockSpec((None, None, tf, d), lambda i, f, te, tv, src: (idx, te[i], fcol(i, f, tv), 0))],
            out_specs=pl.BlockSpec((tm, d), lambda i, f, te, tv, src: (i, 0)),
            scratch_shapes=[pltpu.VMEM((2, tm, d), F32), pltpu.VMEM((tm, d), BF16), pltpu.VMEM((tm, d), F32),
                            pltpu.SemaphoreType.DMA((2,))]),
        out_shape=jax.ShapeDtypeStruct((rows, d), F32),
        compiler_params=_params(2), name="moe_experts")(tile_expert, tile_valid, src, h, w1, w3, w2)


def _moe_combine_kernel(dest_ref, ys_hbm, x_ref, route_ref, mod_ref, o_ref, buf0, buf1, sems):
    tc = x_ref.shape[0]
    i = pl.program_id(0)
    slot = i % 2

    def issue(tile, sl):
        def body(t, carry):
            a = 2 * (tile * tc + t)
            _row_copy(ys_hbm, dest_ref[a], buf0.at[sl], t, sems.at[sl]).start()
            _row_copy(ys_hbm, dest_ref[a + 1], buf1.at[sl], t, sems.at[sl]).start()
            return carry
        lax.fori_loop(0, tc, body, 0, unroll=4)

    def drain(sl):
        def body(t, carry):
            _row_copy(ys_hbm, 0, buf0.at[sl], t, sems.at[sl]).wait()
            _row_copy(ys_hbm, 0, buf1.at[sl], t, sems.at[sl]).wait()
            return carry
        lax.fori_loop(0, tc, body, 0, unroll=4)

    @pl.when(i == 0)
    def _():
        issue(0, 0)

    @pl.when(i + 1 < pl.num_programs(0))
    def _():
        issue(i + 1, 1 - slot)
    drain(slot)
    route = route_ref[...]
    y = route[:, 2:3] * buf0[slot] + route[:, 3:4] * buf1[slot]
    o_ref[...] = x_ref[...] + mod_ref[5:6, :] * y


def moe_combine(ys, dest, x, route, mod, layer, seq, *, tc=256):
    n, d = x.shape
    return pl.pallas_call(
        _moe_combine_kernel,
        grid_spec=pltpu.PrefetchScalarGridSpec(
            num_scalar_prefetch=1, grid=(n // tc,),
            in_specs=[pl.BlockSpec(memory_space=pl.ANY),
                      pl.BlockSpec((tc, d), lambda i, dest: (i, 0)),
                      pl.BlockSpec((tc, 128), lambda i, dest: (i, 0)),
                      pl.BlockSpec((None, None, 6, d), lambda i, dest: (layer, (i * tc) // seq, 0, 0))],
            out_specs=pl.BlockSpec((tc, d), lambda i, dest: (i, 0)),
            scratch_shapes=[pltpu.VMEM((2, tc, d), F32), pltpu.VMEM((2, tc, d), F32),
                            pltpu.SemaphoreType.DMA((2,))]),
        out_shape=jax.ShapeDtypeStruct((n, d), F32),
        compiler_params=_params(1), name="moe_combine")(dest, ys, x, route, mod)


def moe_residual(x, norm_w, mod, router, w1, w3, w2, layer, idx, seq):
    h, route = moe_router(x, norm_w, mod, router, layer, idx, seq)
    dest, src, tile_expert, tile_valid = _moe_plan(route, w1.shape[1], MOE_TILE)
    ys = moe_experts(h, src, tile_expert, tile_valid, w1, w3, w2, idx)
    return moe_combine(ys, dest, x, route, mod, layer, seq)


def _final_norm_kernel(x_ref, w_ref, o_ref):
    x = x_ref[...]
    o_ref[...] = x * lax.rsqrt(jnp.mean(x * x, axis=-1, keepdims=True) + EPS) * w_ref[...]


def final_norm(x, w, *, tm=512):
    n, d = x.shape
    return pl.pallas_call(
        _final_norm_kernel, grid=(n // tm,),
        in_specs=[pl.BlockSpec((tm, d), lambda i: (i, 0)), pl.BlockSpec((1, d), lambda i: (0, 0))],
        out_specs=pl.BlockSpec((tm, d), lambda i: (i, 0)),
        out_shape=jax.ShapeDtypeStruct((n, d), F32),
        compiler_params=_params(1), name="final_norm")(x, w.reshape(1, d))


def _reorder_w_in(w_in):
    sizes = (512, 256, 256, 512, 128, 128, 128, 128, 128, 128, 12, 512, 512, 1024, 4, 4, 1024)
    (aq, ak, av, bq, bkc, bvc, bks, bvs, bkw, bvw, bg, cq, ck, cv, ci, cf, co) = jnp.split(
        w_in, np.cumsum(sizes)[:-1].tolist(), axis=-1)
    pad = lambda n: jnp.zeros(w_in.shape[:-1] + (n,), w_in.dtype)
    out = jnp.concatenate([aq, ak, av, cq, ck, cv, co, bq, bkc, bvc, bks, bvs, bkw, bvw,
                           bg, ci, cf, pad(128 - 20), pad(PROJ_COLS - COL_SMALL - 128)], axis=-1)
    assert out.shape[-1] == PROJ_COLS
    return out.astype(BF16)


def hybrid_mixer(proj, tables, layer, a_sinks, mlstm_gate_b, nsa_pe_k, nsa_pe_v, nsa_ck_w1, nsa_ck_w2,
                 nsa_cv_w1, nsa_cv_w2, mlstm_conv_w, mlstm_norm_w):
    bsz, t, _ = proj.shape
    o_a = sliding_window_attention(proj, col_q=COL_AQ, col_k=COL_AK, col_v=COL_AV, n_heads=A_HEADS,
                                   n_kv=A_KV_HEADS, window=A_WINDOW, tq=256, tables=tables,
                                   sinks=a_sinks[layer])
    chunks = lambda col: proj[:, :, col:col + HEAD_DIM].reshape(bsz, t // CMP_STRIDE, CMP_STRIDE * HEAD_DIM)
    k_cmp = compress_blocks(chunks(COL_BKC), nsa_pe_k, nsa_ck_w1, nsa_ck_w2, layer)
    v_cmp = compress_blocks(chunks(COL_BVC), nsa_pe_v, nsa_cv_w1, nsa_cv_w2, layer)
    o_cmp, bias_t = compressed_attention_select(proj, k_cmp, v_cmp)
    o_sel = selected_attention(proj, bias_t)
    o_win = sliding_window_attention(proj, col_q=COL_BQ, col_k=COL_BKW, col_v=COL_BVW, n_heads=B_HEADS,
                                     n_kv=1, window=B_WINDOW, tq=512, out_dtype=F32)
    o_b = combine_branches(proj, o_cmp, o_sel, o_win)
    o_c = mlstm_mixer(proj, mlstm_gate_b, mlstm_conv_w, mlstm_norm_w, layer)
    return o_a, o_b, o_c


def kernel(x, c, positions, ada_w, ada_b, norm_mix_w, norm_ffn_w, w_in, mlstm_gate_b, a_sinks, nsa_pe_k,
           nsa_pe_v, nsa_ck_w1, nsa_ck_w2, nsa_cv_w1, nsa_cv_w2, mlstm_conv_w, mlstm_norm_w, w_out, ffn_w1,
           ffn_w3, ffn_w2, moe_router, moe_w1, moe_w3, moe_w2, final_norm_w):
    bsz, t, d = x.shape
    depth = ada_w.shape[0]
    mod = ada_modulation(c, ada_w, ada_b)
    tables = rope_tables(positions)
    w_in_p = _reorder_w_in(w_in)
    w_out_b = w_out.astype(BF16)
    ffn_b = [w.astype(BF16) for w in (ffn_w1, ffn_w3, ffn_w2)]
    moe_b = [w.astype(BF16) for w in (moe_w1, moe_w3, moe_w2)]
    router_p = jnp.pad(moe_router, ((0, 0), (0, 0), (0, 128 - N_EXPERTS)))
    norm_mix = norm_mix_w.reshape(depth, 1, d)
    norm_ffn = norm_ffn_w.reshape(depth, 1, d)
    conv_w = mlstm_conv_w
    norm_c = mlstm_norm_w.reshape(depth, 1, -1)

    xf = x.reshape(bsz * t, d)
    for layer in range(depth):
        proj = norm_proj(xf, norm_mix, mod, w_in_p, layer, t, shift_row=0, scale_row=1)
        o_a, o_b, o_c = hybrid_mixer(proj.reshape(bsz, t, PROJ_COLS), tables, layer, a_sinks, mlstm_gate_b,
                                     nsa_pe_k, nsa_pe_v, nsa_ck_w1, nsa_ck_w2, nsa_cv_w1, nsa_cv_w2,
                                     conv_w, norm_c)
        flat = lambda a: a.reshape(bsz * t, a.shape[-1])
        xf = out_proj_residual(flat(o_a), flat(o_b), flat(o_c), w_out_b, xf, mod, layer, t)
        if layer % 2 == 0:
            xf = ffn_residual(xf, norm_ffn, mod, *ffn_b, layer, layer // 2, t)
        else:
            xf = moe_residual(xf, norm_ffn, mod, router_p, *moe_b, layer, layer // 2, t)
    return final_norm(xf, final_norm_w).reshape(bsz, t, d)
```

```python
import functools
import math

import numpy as np
import jax
import jax.numpy as jnp
from jax import lax
from jax.experimental import pallas as pl
from jax.experimental.pallas import tpu as pltpu

F32 = jnp.float32
BF16 = jnp.bfloat16

HEAD_DIM = 128
A_HEADS, A_KV_HEADS, A_WINDOW = 4, 2, 128
ROPE_THETA = 150000.0
B_HEADS = 4
CMP_LEN, CMP_STRIDE = 32, 16
SEL_LEN, SEL_TOPK, B_WINDOW = 64, 16, 512
C_HEADS, C_QK_DIM, C_V_DIM, C_CONV = 4, 128, 256, 4
N_EXPERTS = 8
EPS = 1e-6
BIG = 1e9
NEG = -1e30
SCALE = HEAD_DIM ** -0.5

COL_AQ, COL_AK, COL_AV = 0, 512, 768
COL_CQK, COL_CV, COL_CO = 1024, 2048, 3072
COL_BQ = 4096
COL_BKC, COL_BVC, COL_BKS, COL_BVS, COL_BKW, COL_BVW = 4608, 4736, 4864, 4992, 5120, 5248
COL_SMALL = 5376
LANE_CI, LANE_CF = 12, 16
PROJ_COLS = 5632

VMEM_LIMIT_MB = 56
BF16_SUBLANES = 16


def _params(n_axes, vmem_mb=VMEM_LIMIT_MB):
    return pltpu.CompilerParams(dimension_semantics=("arbitrary",) * n_axes,
                                vmem_limit_bytes=vmem_mb * 2 ** 20)


def _dot(a, b):
    return jnp.dot(a, b, preferred_element_type=F32)


def _dot_nt(a, b):
    return lax.dot_general(a, b, (((1,), (1,)), ((), ())), preferred_element_type=F32)


def _silu(v):
    return v * jax.nn.sigmoid(v)


def _norm_mod(x, nw, shift, scale):
    ms = jnp.mean(x * x, axis=-1, keepdims=True)
    return (x * lax.rsqrt(ms + EPS) * nw) * (1.0 + scale) + shift


def _ada_kernel(c_ref, w_ref, b_ref, o_ref):
    act = _silu(c_ref[...]).astype(BF16)
    o_ref[...] = _dot(act, w_ref[...].astype(BF16)) + b_ref[...]


def ada_modulation(c, ada_w, ada_b):
    depth, d, n6 = ada_w.shape
    bsz = c.shape[0]
    assert bsz <= 8
    cp = jnp.zeros((8, d), F32).at[:bsz].set(c)
    tn = 512
    out = pl.pallas_call(
        _ada_kernel, grid=(depth, n6 // tn),
        in_specs=[pl.BlockSpec((8, d), lambda l, j: (0, 0)),
                  pl.BlockSpec((None, d, tn), lambda l, j: (l, 0, j)),
                  pl.BlockSpec((None, 1, tn), lambda l, j: (l, 0, j))],
        out_specs=pl.BlockSpec((None, 8, tn), lambda l, j: (l, 0, j)),
        out_shape=jax.ShapeDtypeStruct((depth, 8, n6), F32),
        compiler_params=_params(2), name="ada_modulation")(cp, ada_w, ada_b.reshape(depth, 1, n6))
    return out.reshape(depth, 8, 6, d)


def _rope_table_kernel(pos_ref, cos_ref, sin_ref):
    pos = pos_ref[...].astype(F32)
    lane = lax.broadcasted_iota(jnp.int32, (1, HEAD_DIM), 1)
    half = HEAD_DIM // 2
    inv = jnp.exp((lane & (half - 1)).astype(F32) * (-math.log(ROPE_THETA) * 2.0 / HEAD_DIM))
    ang = pos * inv
    cos_ref[...] = jnp.cos(ang)
    sin_ref[...] = jnp.where(lane < half, -1.0, 1.0) * jnp.sin(ang)


def rope_tables(positions):
    bsz, t = positions.shape
    tt = 512
    spec = pl.BlockSpec((None, tt, HEAD_DIM), lambda b, i: (b, i, 0))
    return pl.pallas_call(
        _rope_table_kernel, grid=(bsz, t // tt),
        in_specs=[pl.BlockSpec((None, tt, 1), lambda b, i: (b, i, 0))],
        out_specs=[spec, spec],
        out_shape=[jax.ShapeDtypeStruct((bsz, t, HEAD_DIM), F32)] * 2,
        compiler_params=_params(2), name="rope_tables")(positions.reshape(bsz, t, 1))


def _proj_kernel(x_ref, nw_ref, mod_ref, w_ref, o_ref, h_scr, *, shift_row, scale_row):
    @pl.when(pl.program_id(1) == 0)
    def _():
        h = _norm_mod(x_ref[...], nw_ref[...], mod_ref[shift_row:shift_row + 1, :],
                      mod_ref[scale_row:scale_row + 1, :])
        h_scr[...] = h.astype(BF16)

    o_ref[...] = _dot(h_scr[...], w_ref[...])


def norm_proj(x, norm_w, mod, w, layer, seq, *, shift_row, scale_row, tm=1024, tn=1408):
    n, d = x.shape
    p = w.shape[-1]
    assert seq % tm == 0 and p % tn == 0
    return pl.pallas_call(
        functools.partial(_proj_kernel, shift_row=shift_row, scale_row=scale_row),
        grid=(n // tm, p // tn),
        in_specs=[pl.BlockSpec((tm, d), lambda i, j: (i, 0)),
                  pl.BlockSpec((None, 1, d), lambda i, j: (layer, 0, 0)),
                  pl.BlockSpec((None, None, 6, d), lambda i, j: (layer, (i * tm) // seq, 0, 0)),
                  pl.BlockSpec((None, d, tn), lambda i, j: (layer, 0, j))],
        out_specs=pl.BlockSpec((tm, tn), lambda i, j: (i, j)),
        out_shape=jax.ShapeDtypeStruct((n, p), F32),
        scratch_shapes=[pltpu.VMEM((tm, d), BF16)],
        compiler_params=_params(2), name="norm_proj")(x, norm_w, mod, w)


SWA_SUB = 128


def _swa_kernel(*refs, tq, window, n_heads, group, rope, sinks, gated):
    it = iter(refs)
    q_ref, kp_ref, kc_ref, vp_ref, vc_ref = (next(it) for _ in range(5))
    if rope:
        cp_ref, sp_ref, cc_ref, sc_ref = (next(it) for _ in range(4))
    if sinks:
        sink_ref = next(it)
    if gated:
        g_ref, cmp_ref, sel_ref = (next(it) for _ in range(3))
        gates = jax.nn.sigmoid(g_ref[...])
    o_ref = next(it)
    i = pl.program_id(1)
    sub = SWA_SUB
    pad = -(-(window - 1) // sub) * sub
    span = pad + sub
    row = lax.broadcasted_iota(jnp.int32, (sub, span), 0)
    col = lax.broadcasted_iota(jnp.int32, (sub, span), 1)
    diff = row + pad - col

    def window_mask(n_prev):
        d = diff if n_prev == 0 else diff + jnp.where(col < n_prev, jnp.where(i > 0, 0, window), 0)
        return jnp.abs(2 * d - (window - 1)) <= (window - 1)
    masks = {}

    def rotate(v, c, s):
        return v * c + pltpu.roll(v, HEAD_DIM // 2, 1) * s

    kv_cache = {}
    for hq in range(n_heads):
        kv = hq // group
        if kv not in kv_cache:
            sl = slice(kv * HEAD_DIM, (kv + 1) * HEAD_DIM)
            kp, kc = kp_ref[:, sl], kc_ref[:, sl]
            if rope:
                kp = rotate(kp, cp_ref[...], sp_ref[...])
                kc = rotate(kc, cc_ref[...], sc_ref[...])
            k = jnp.concatenate([kp, kc], axis=0).astype(BF16)
            v = jnp.concatenate([vp_ref[:, sl], vc_ref[:, sl]], axis=0).astype(BF16)
            kv_cache[kv] = (k, v)
        k, v = kv_cache[kv]
        hs = slice(hq * HEAD_DIM, (hq + 1) * HEAD_DIM)
        q = q_ref[:, hs]
        if rope:
            q = rotate(q, cc_ref[...], sc_ref[...])
        q = (q * SCALE).astype(BF16)
        for j in range(tq // sub):
            rows = slice(j * sub, (j + 1) * sub)
            lo = tq + j * sub - pad
            n_prev = max(tq - lo, 0)
            if n_prev not in masks:
                masks[n_prev] = window_mask(n_prev)
            s = _dot_nt(q[rows], k[lo:lo + span])
            s = jnp.where(masks[n_prev], s, NEG)
            m = jnp.max(s, axis=-1, keepdims=True)
            if sinks:
                m = jnp.maximum(m, sink_ref[hq])
            m = jnp.where(m < 0.5 * NEG, 0.0, m)
            p = jnp.exp(s - m)
            den = jnp.sum(p, axis=-1, keepdims=True)
            if sinks:
                den = den + jnp.exp(sink_ref[hq] - m)
            o = _dot(p.astype(BF16), v[lo:lo + span]) / jnp.maximum(den, 1e-30)
            if gated:
                g = gates[rows]
                o = (g[:, 3 * hq:3 * hq + 1] * cmp_ref[rows, hs] + g[:, 3 * hq + 1:3 * hq + 2] * sel_ref[rows, hs]
                     + g[:, 3 * hq + 2:3 * hq + 3] * o)
            o_ref[rows, hs] = o.astype(o_ref.dtype)


def sliding_window_attention(proj, *, col_q, col_k, col_v, n_heads, n_kv, window, tq,
                             tables=None, sinks=None, gate_with=None):
    bsz, t, _ = proj.shape
    assert window - 1 <= tq and t % tq == 0 and tq % SWA_SUB == 0
    wq, wkv = n_heads * HEAD_DIM, n_kv * HEAD_DIM
    assert col_q % wq == 0 and col_k % wkv == 0 and col_v % wkv == 0
    cur = lambda b, i: (b, i, 0)
    prev = lambda b, i: (b, jnp.maximum(i - 1, 0), 0)
    kvspec = lambda col, im: pl.BlockSpec(
        (None, tq, wkv), lambda b, i: (b, im(b, i)[1], col // wkv))
    in_specs = [pl.BlockSpec((None, tq, wq), lambda b, i: (b, i, col_q // wq)),
                kvspec(col_k, prev), kvspec(col_k, cur), kvspec(col_v, prev), kvspec(col_v, cur)]
    args = [proj] * 5
    if tables is not None:
        cos, sin = tables
        tp = pl.BlockSpec((None, tq, HEAD_DIM), prev)
        tc = pl.BlockSpec((None, tq, HEAD_DIM), cur)
        in_specs += [tp, tp, tc, tc]
        args += [cos, sin, cos, sin]
    if sinks is not None:
        in_specs.append(pl.BlockSpec(memory_space=pltpu.SMEM))
        args.append(sinks)
    if gate_with is not None:
        branch = pl.BlockSpec((None, tq, wq), cur)
        in_specs += [pl.BlockSpec((None, tq, 128), lambda b, i: (b, i, COL_SMALL // 128)), branch, branch]
        args += [proj, *gate_with]
    return pl.pallas_call(
        functools.partial(_swa_kernel, tq=tq, window=window, n_heads=n_heads, group=n_heads // n_kv,
                          rope=tables is not None, sinks=sinks is not None, gated=gate_with is not None),
        grid=(bsz, t // tq), in_specs=in_specs,
        out_specs=pl.BlockSpec((None, tq, wq), cur),
        out_shape=jax.ShapeDtypeStruct((bsz, t, wq), BF16),
        compiler_params=_params(2), name="sliding_window_attention")(*args)


def _compress_kernel(r_ref, pe_ref, w1_ref, w2_ref, o_ref):
    n16, half = r_ref.shape
    r = r_ref[...].astype(BF16)
    w1 = w1_ref[...].astype(BF16)
    first = _dot(r, w1[:half])
    second = _dot(r, w1[half:])
    pe = _dot(pe_ref[...].astype(BF16), w1)[0:1]
    pre = first + pltpu.roll(second, n16 - 1, 0) + pe
    hid = jax.nn.gelu(pre, approximate=True)
    o_ref[...] = _dot(hid.astype(BF16), w2_ref[...].astype(BF16))


def compress_blocks(chunks, pe, w1, w2, layer):
    bsz, n16, half = chunks.shape
    hid = w1.shape[-1]
    pe8 = jnp.broadcast_to(pe[layer].reshape(1, 2 * half), (8, 2 * half))
    return pl.pallas_call(
        _compress_kernel, grid=(bsz,),
        in_specs=[pl.BlockSpec((None, n16, half), lambda b: (b, 0, 0)),
                  pl.BlockSpec((8, 2 * half), lambda b: (0, 0)),
                  pl.BlockSpec((None, 2 * half, hid), lambda b: (layer, 0, 0)),
                  pl.BlockSpec((None, hid, HEAD_DIM), lambda b: (layer, 0, 0))],
        out_specs=pl.BlockSpec((None, n16, HEAD_DIM), lambda b: (b, 0, 0)),
        out_shape=jax.ShapeDtypeStruct((bsz, n16, HEAD_DIM), F32),
        compiler_params=_params(1), name="compress_blocks")(chunks, pe8, w1, w2)


def _heads_transposed(q, tq, scale=SCALE):
    n = q.shape[1] // HEAD_DIM
    return jnp.concatenate([(q[:, g * HEAD_DIM:(g + 1) * HEAD_DIM] * scale).T for g in range(n)],
                           axis=1).astype(BF16)


def _cmp_select_kernel(q_ref, kc_ref, vc_ref, ov_ref, ocmp_ref, bias_ref, *, tq, n_sel, topk):
    i = pl.program_id(1)
    nc = kc_ref.shape[0]
    hq = B_HEADS * tq
    qt = _heads_transposed(q_ref[...], tq)
    st = _dot(kc_ref[...].astype(BF16), qt)
    n_row = lax.broadcasted_iota(jnp.int32, (nc, hq), 0)
    t_col = i * tq + (lax.broadcasted_iota(jnp.int32, (nc, hq), 1) & (tq - 1))
    st = jnp.where(n_row * CMP_STRIDE + (CMP_LEN - 1) <= t_col, st, NEG)
    m = jnp.max(st, axis=0, keepdims=True)
    m = jnp.where(m < 0.5 * NEG, 0.0, m)
    p = jnp.exp(st - m)
    p = p / jnp.maximum(jnp.sum(p, axis=0, keepdims=True), 1e-30)
    pb = p.astype(BF16)
    ot = _dot(vc_ref[...].T.astype(BF16), pb)
    for g in range(B_HEADS):
        ocmp_ref[:, g * HEAD_DIM:(g + 1) * HEAD_DIM] = ot[:, g * tq:(g + 1) * tq].T
    imp4 = _dot(ov_ref[...], pb)
    imp = imp4[:, 0:tq]
    for g in range(1, B_HEADS):
        imp = imp + imp4[:, g * tq:(g + 1) * tq]
    nsp = imp.shape[0]
    j = lax.broadcasted_iota(jnp.int32, (nsp, tq), 0)
    t = i * tq + lax.broadcasted_iota(jnp.int32, (nsp, tq), 1)
    cur = t >> 6
    imp = jnp.where((j << 6) <= t, imp, -BIG)
    for forced in (0, cur, cur - 1):
        imp = jnp.where(j == forced, BIG, imp)
    imp = jnp.where(j < n_sel, imp, -3e38)
    sel = jnp.zeros((nsp, tq), F32)
    for _ in range(topk):
        mx = jnp.max(imp, axis=0, keepdims=True)
        idx = jnp.min(jnp.where(imp == mx, j, nsp), axis=0, keepdims=True)
        hit = j == idx
        sel = jnp.where(hit, 1.0, sel)
        imp = jnp.where(hit, -3e38, imp)
    bias_ref[...] = jnp.where(sel > 0.0, 0.0, jnp.where(j < n_sel, -BIG, 0.0))


def _overlap_matrix_t(t):
    n16, n_sel = t // CMP_STRIDE, t // SEL_LEN
    nsp = -(-n_sel // 128) * 128
    starts = np.arange(n16) * CMP_STRIDE
    sel_start = np.arange(n_sel) * SEL_LEN
    ov = np.clip(np.minimum(starts[:, None] + CMP_LEN, sel_start[None, :] + SEL_LEN)
                 - np.maximum(starts[:, None], sel_start[None, :]), 0, None).astype(np.float32) / CMP_LEN
    out = np.zeros((nsp, n16), np.float32)
    out[:n_sel] = ov.T
    return jnp.asarray(out, BF16)


def compressed_attention_select(proj, k_cmp, v_cmp, *, tq=256):
    bsz, t, _ = proj.shape
    n16, n_sel = t // CMP_STRIDE, t // SEL_LEN
    assert SEL_LEN == 64 and tq % 128 == 0 and tq & (tq - 1) == 0 and t % tq == 0
    ov = _overlap_matrix_t(t)
    nsp = ov.shape[0]
    wq = B_HEADS * HEAD_DIM
    return pl.pallas_call(
        functools.partial(_cmp_select_kernel, tq=tq, n_sel=n_sel, topk=min(SEL_TOPK, n_sel)),
        grid=(bsz, t // tq),
        in_specs=[pl.BlockSpec((None, tq, wq), lambda b, i: (b, i, COL_BQ // wq)),
                  pl.BlockSpec((None, n16, HEAD_DIM), lambda b, i: (b, 0, 0)),
                  pl.BlockSpec((None, n16, HEAD_DIM), lambda b, i: (b, 0, 0)),
                  pl.BlockSpec((nsp, n16), lambda b, i: (0, 0))],
        out_specs=[pl.BlockSpec((None, tq, wq), lambda b, i: (b, i, 0)),
                   pl.BlockSpec((None, nsp, tq), lambda b, i: (b, 0, i))],
        out_shape=[jax.ShapeDtypeStruct((bsz, t, wq), F32),
                   jax.ShapeDtypeStruct((bsz, nsp, t), F32)],
        compiler_params=_params(2), name="compressed_attention_select")(proj, k_cmp, v_cmp, ov)


def _selected_kernel(q_ref, bias_ref, ks_ref, vs_ref, o_ref, kaug_scr, vt_scr, qa_scr, s0_scr, s1_scr,
                     m_scr, acc_scr, *, tq, kb):
    i = pl.program_id(1)
    t_all = ks_ref.shape[0]
    nsp = bias_ref.shape[0]
    hq = B_HEADS * tq
    ones_rows = vt_scr.shape[1] - HEAD_DIM

    @pl.when(i == 0)
    def _():
        def prep(c, carry):
            r0 = pl.multiple_of(c * kb, kb)
            kaug_scr[pl.ds(r0, kb), 0:HEAD_DIM] = ks_ref[pl.ds(r0, kb), :].astype(BF16)
            key = r0 + lax.broadcasted_iota(jnp.int32, (kb, nsp), 0)
            blk = lax.broadcasted_iota(jnp.int32, (kb, nsp), 1)
            kaug_scr[pl.ds(r0, kb), HEAD_DIM:] = jnp.where((key >> 6) == blk, 1.0, 0.0).astype(BF16)
            vt_scr[c] = jnp.concatenate([vs_ref[pl.ds(r0, kb), :].T, jnp.ones((ones_rows, kb), F32)],
                                        axis=0).astype(BF16)
            return carry
        lax.fori_loop(0, t_all // kb, prep, 0)

    bias = bias_ref[...]
    qa_scr[...] = jnp.concatenate([_heads_transposed(q_ref[...], tq, SCALE * math.log2(math.e)),
                                   jnp.concatenate([bias] * B_HEADS, axis=1).astype(BF16)], axis=0)
    m_scr[...] = jnp.full(m_scr.shape, NEG, F32)
    acc_scr[...] = jnp.zeros(acc_scr.shape, F32)

    def scores(kt, s_out):
        r0 = pl.multiple_of(kt * kb, kb)
        s_out[...] = _dot(kaug_scr[pl.ds(r0, kb), :], qa_scr[...])

    def consume(kt, s_in, causal):
        st = s_in[...]
        if causal:
            key = kt * kb + lax.broadcasted_iota(jnp.int32, (kb, hq), 0)
            tpos = i * tq + (lax.broadcasted_iota(jnp.int32, (kb, hq), 1) & (tq - 1))
            st = jnp.where(key <= tpos, st, NEG)
        m_old = m_scr[...]
        m_new = jnp.maximum(m_old, jnp.max(st, axis=0, keepdims=True))
        p = jnp.exp2((st - m_new).astype(BF16))
        acc_scr[...] = jnp.exp2(m_old - m_new) * acc_scr[...] + _dot(vt_scr[kt], p)
        m_scr[...] = m_new

    def step(kt, s_in, s_out):
        scores(kt + 1, s_out)
        consume(kt, s_in, False)

    def finish(kt, s_in):
        consume(kt, s_in, True)
        ot = acc_scr[0:HEAD_DIM, :] / jnp.maximum(acc_scr[HEAD_DIM:HEAD_DIM + 1, :], 1e-30)
        for g in range(B_HEADS):
            o_ref[:, g * HEAD_DIM:(g + 1) * HEAD_DIM] = ot[:, g * tq:(g + 1) * tq].T

    n_full = (i * tq) // kb
    scores(0, s0_scr)

    def pair(j, carry):
        step(2 * j, s0_scr, s1_scr)
        step(2 * j + 1, s1_scr, s0_scr)
        return carry
    lax.fori_loop(0, n_full // 2, pair, 0)

    @pl.when(n_full % 2 == 1)
    def _():
        step(n_full - 1, s0_scr, s1_scr)
        finish(n_full, s1_scr)

    @pl.when(n_full % 2 == 0)
    def _():
        finish(n_full, s0_scr)


def selected_attention(proj, bias_t, *, tq=128, kb=512):
    bsz, t, _ = proj.shape
    nsp = bias_t.shape[1]
    wq = B_HEADS * HEAD_DIM
    assert kb % tq == 0 and t % kb == 0
    return pl.pallas_call(
        functools.partial(_selected_kernel, tq=tq, kb=kb),
        grid=(bsz, t // tq),
        in_specs=[pl.BlockSpec((None, tq, wq), lambda b, i: (b, i, COL_BQ // wq)),
                  pl.BlockSpec((None, nsp, tq), lambda b, i: (b, 0, i)),
                  pl.BlockSpec((None, t, HEAD_DIM), lambda b, i: (b, 0, COL_BKS // HEAD_DIM)),
                  pl.BlockSpec((None, t, HEAD_DIM), lambda b, i: (b, 0, COL_BVS // HEAD_DIM))],
        out_specs=pl.BlockSpec((None, tq, wq), lambda b, i: (b, i, 0)),
        out_shape=jax.ShapeDtypeStruct((bsz, t, wq), F32),
        scratch_shapes=[pltpu.VMEM((t, HEAD_DIM + nsp), BF16),
                        pltpu.VMEM((t // kb, HEAD_DIM + BF16_SUBLANES, kb), BF16),
                        pltpu.VMEM((HEAD_DIM + nsp, B_HEADS * tq), BF16),
                        pltpu.VMEM((kb, B_HEADS * tq), F32),
                        pltpu.VMEM((kb, B_HEADS * tq), F32),
                        pltpu.VMEM((1, B_HEADS * tq), F32),
                        pltpu.VMEM((HEAD_DIM + BF16_SUBLANES, B_HEADS * tq), F32)],
        compiler_params=_params(2), name="selected_attention")(proj, bias_t, proj, proj)


def _mlstm_kernel(gb_ref, qk_ref, v_ref, og_ref, small_ref, convw_ref, normw_ref, o_ref,
                  hist_scr, c_scr, n_scr, m_scr, *, chunk):
    L = chunk
    dqk, dv = C_QK_DIM, C_V_DIM

    @pl.when(pl.program_id(1) == 0)
    def _():
        hist_scr[...] = jnp.zeros(hist_scr.shape, F32)
        c_scr[...] = jnp.zeros(c_scr.shape, F32)
        n_scr[...] = jnp.zeros(n_scr.shape, F32)
        m_scr[...] = jnp.zeros(m_scr.shape, F32)

    x = qk_ref[...]
    xe = jnp.concatenate([hist_scr[...], x], axis=0)
    w = convw_ref[...]
    y = w[0:1] * xe[8 - 3:8 - 3 + L]
    for tap in range(1, C_CONV):
        y = y + w[tap:tap + 1] * xe[8 - 3 + tap:8 - 3 + tap + L]
    hist_scr[...] = x[L - 8:L]
    qk = _silu(y)

    lane = lax.broadcasted_iota(jnp.int32, (1, 128), 1)
    gbias = jnp.zeros((1, 128), F32)
    for idx in range(2 * C_HEADS):
        gbias = jnp.where(lane == LANE_CI + idx, gb_ref[idx], gbias)
    pre = small_ref[...] + gbias
    logf = jnp.minimum(pre, 0.0) - jnp.log1p(jnp.exp(-jnp.abs(pre)))
    row = lax.broadcasted_iota(jnp.int32, (L, L), 0)
    col = lax.broadcasted_iota(jnp.int32, (L, L), 1)
    causal = row >= col
    tri = jnp.where(causal, 1.0, 0.0)
    hp = lax.Precision.HIGHEST
    b_cols = jnp.dot(tri, logf, precision=hp, preferred_element_type=F32)
    b_rows = lax.dot_general(logf.T, tri, (((1,), (1,)), ((), ())), precision=hp,
                             preferred_element_type=F32)
    pre_t = pre.T

    for h in range(C_HEADS):
        q = qk[:, h * dqk:(h + 1) * dqk]
        k = qk[:, C_HEADS * dqk + h * dqk:C_HEADS * dqk + (h + 1) * dqk] * (dqk ** -0.5)
        v = v_ref[:, h * dv:(h + 1) * dv].astype(BF16)
        b_col = b_cols[:, LANE_CF + h:LANE_CF + h + 1]
        i_col = pre[:, LANE_CI + h:LANE_CI + h + 1]
        b_row = b_rows[LANE_CF + h:LANE_CF + h + 1, :]
        i_row = pre_t[LANE_CI + h:LANE_CI + h + 1, :]
        m_prev = m_scr[h:h + 1, 0:1]
        ct = c_scr[h]
        n_row = n_scr[h:h + 1, :]

        d = jnp.where(causal, b_col - b_row + i_row, NEG)
        m_inter = b_col + m_prev
        m_t = jnp.maximum(m_inter, jnp.max(d, axis=-1, keepdims=True))
        qb = q.astype(BF16)
        s = _dot_nt(qb, k.astype(BF16)) * jnp.exp(d - m_t)
        inter = jnp.exp(m_inter - m_t)
        num = _dot(s.astype(BF16), v) + inter * _dot(qb, ct.astype(BF16))
        den = jnp.sum(s, axis=-1, keepdims=True) + inter * jnp.sum(q * n_row, axis=-1, keepdims=True)
        hh = num / jnp.maximum(jnp.abs(den), jnp.exp(-m_t))

        b_last = b_col[L - 1:L, :]
        g = b_last - b_col + i_col
        m_new = jnp.maximum(b_last + m_prev, jnp.max(g, axis=0, keepdims=True))
        kd = k * jnp.exp(g - m_new)
        keep = jnp.exp(b_last + m_prev - m_new)
        c_scr[h] = keep * ct + _dot(kd.T.astype(BF16), v)
        n_scr[h:h + 1, :] = keep * n_row + jnp.sum(kd, axis=0, keepdims=True)
        m_scr[h:h + 1, :] = jnp.broadcast_to(m_new, (1, 128))

        vs = slice(h * dv, (h + 1) * dv)
        hn = hh * lax.rsqrt(jnp.mean(hh * hh, axis=-1, keepdims=True) + EPS) * normw_ref[:, vs]
        o_ref[:, vs] = (hn * jax.nn.sigmoid(og_ref[:, vs])).astype(o_ref.dtype)


def mlstm_mixer(proj, gate_b, conv_w, norm_w, layer, *, chunk=128):
    bsz, t, _ = proj.shape
    wqk, wv = 2 * C_HEADS * C_QK_DIM, C_HEADS * C_V_DIM
    assert wqk == wv == 1024 and t % chunk == 0
    blk = lambda col: pl.BlockSpec((None, chunk, 1024), lambda b, i: (b, i, col // 1024))
    return pl.pallas_call(
        functools.partial(_mlstm_kernel, chunk=chunk),
        grid=(bsz, t // chunk),
        in_specs=[pl.BlockSpec(memory_space=pltpu.SMEM),
                  blk(COL_CQK), blk(COL_CV), blk(COL_CO),
                  pl.BlockSpec((None, chunk, 128), lambda b, i: (b, i, COL_SMALL // 128)),
                  pl.BlockSpec((None, C_CONV, wqk), lambda b, i: (layer, 0, 0)),
                  pl.BlockSpec((None, 1, wv), lambda b, i: (layer, 0, 0))],
        out_specs=pl.BlockSpec((None, chunk, wv), lambda b, i: (b, i, 0)),
        out_shape=jax.ShapeDtypeStruct((bsz, t, wv), BF16),
        scratch_shapes=[pltpu.VMEM((8, wqk), F32),
                        pltpu.VMEM((C_HEADS, C_QK_DIM, C_V_DIM), F32),
                        pltpu.VMEM((8, C_QK_DIM), F32),
                        pltpu.VMEM((8, 128), F32)],
        compiler_params=_params(2), name="mlstm_mixer")(
            gate_b[layer], proj, proj, proj, proj, conv_w, norm_w)


def _outproj_kernel(oa_ref, ob_ref, oc_ref, w_ref, x_ref, mod_ref, o_ref, *, gate_row):
    wa, wb = oa_ref.shape[1], ob_ref.shape[1]
    acc = _dot(oa_ref[...], w_ref[0:wa, :])
    acc = acc + _dot(ob_ref[...], w_ref[wa:wa + wb, :])
    acc = acc + _dot(oc_ref[...], w_ref[wa + wb:, :])
    o_ref[...] = x_ref[...] + mod_ref[gate_row:gate_row + 1, :] * acc


def out_proj_residual(o_a, o_b, o_c, w_out, x, mod, layer, seq, *, tm=1024, tn=1024):
    n, d = x.shape
    assert seq % tm == 0 and d % tn == 0
    wa, wb, wc = o_a.shape[1], o_b.shape[1], o_c.shape[1]
    return pl.pallas_call(
        functools.partial(_outproj_kernel, gate_row=2),
        grid=(n // tm, d // tn),
        in_specs=[pl.BlockSpec((tm, wa), lambda i, j: (i, 0)),
                  pl.BlockSpec((tm, wb), lambda i, j: (i, 0)),
                  pl.BlockSpec((tm, wc), lambda i, j: (i, 0)),
                  pl.BlockSpec((None, wa + wb + wc, tn), lambda i, j: (layer, 0, j)),
                  pl.BlockSpec((tm, tn), lambda i, j: (i, j)),
                  pl.BlockSpec((None, None, 6, tn), lambda i, j: (layer, (i * tm) // seq, 0, j))],
        out_specs=pl.BlockSpec((tm, tn), lambda i, j: (i, j)),
        out_shape=jax.ShapeDtypeStruct((n, d), F32),
        compiler_params=_params(2), name="out_proj_residual")(o_a, o_b, o_c, w_out, x, mod)


def _ffn_kernel(x_ref, nw_ref, mod_ref, w1_ref, w3_ref, w2_ref, o_ref, h_scr, acc_scr):
    f = pl.program_id(1)

    @pl.when(f == 0)
    def _():
        h = _norm_mod(x_ref[...], nw_ref[...], mod_ref[3:4, :], mod_ref[4:5, :])
        h_scr[...] = h.astype(BF16)
        acc_scr[...] = jnp.zeros(acc_scr.shape, F32)

    h = h_scr[...]
    g = _silu(_dot(h, w1_ref[...])) * _dot(h, w3_ref[...])
    acc_scr[...] += _dot(g.astype(BF16), w2_ref[...])

    @pl.when(f == pl.num_programs(1) - 1)
    def _():
        o_ref[...] = x_ref[...] + mod_ref[5:6, :] * acc_scr[...]


def ffn_residual(x, norm_w, mod, w1, w3, w2, layer, idx, seq, *, tm=512, tf=512):
    n, d = x.shape
    dff = w1.shape[-1]
    return pl.pallas_call(
        _ffn_kernel, grid=(n // tm, dff // tf),
        in_specs=[pl.BlockSpec((tm, d), lambda i, f: (i, 0)),
                  pl.BlockSpec((None, 1, d), lambda i, f: (layer, 0, 0)),
                  pl.BlockSpec((None, None, 6, d), lambda i, f: (layer, (i * tm) // seq, 0, 0)),
                  pl.BlockSpec((None, d, tf), lambda i, f: (idx, 0, f)),
                  pl.BlockSpec((None, d, tf), lambda i, f: (idx, 0, f)),
                  pl.BlockSpec((None, tf, d), lambda i, f: (idx, f, 0))],
        out_specs=pl.BlockSpec((tm, d), lambda i, f: (i, 0)),
        out_shape=jax.ShapeDtypeStruct((n, d), F32),
        scratch_shapes=[pltpu.VMEM((tm, d), BF16), pltpu.VMEM((tm, d), F32)],
        compiler_params=_params(2), name="ffn_residual")(x, norm_w, mod, w1, w3, w2)


MOE_TILE = 512


def _router_kernel(x_ref, nw_ref, mod_ref, r_ref, h_ref, route_ref):
    tm = x_ref.shape[0]
    lane = lax.broadcasted_iota(jnp.int32, (tm, 128), 1)
    h = _norm_mod(x_ref[...], nw_ref[...], mod_ref[3:4, :], mod_ref[4:5, :])
    h_ref[...] = h
    logits = jnp.dot(h, r_ref[...], precision=lax.Precision.HIGHEST, preferred_element_type=F32)
    logits = jnp.where(lane < N_EXPERTS, logits, NEG)
    v1 = jnp.max(logits, axis=-1, keepdims=True)
    i1 = jnp.min(jnp.where(logits == v1, lane, 128), axis=-1, keepdims=True)
    rest = jnp.where(lane == i1, NEG, logits)
    v2 = jnp.max(rest, axis=-1, keepdims=True)
    i2 = jnp.min(jnp.where(rest == v2, lane, 128), axis=-1, keepdims=True)
    e2 = jnp.exp(v2 - v1)
    route = jnp.where(lane == 0, i1.astype(F32), jnp.where(lane == 1, i2.astype(F32), 0.0))
    route = jnp.where(lane == 2, 1.0 / (1.0 + e2), jnp.where(lane == 3, e2 / (1.0 + e2), route))
    route_ref[...] = route


def moe_router(x, norm_w, mod, router, layer, idx, seq, *, tm=512):
    n, d = x.shape
    return pl.pallas_call(
        _router_kernel, grid=(n // tm,),
        in_specs=[pl.BlockSpec((tm, d), lambda i: (i, 0)),
                  pl.BlockSpec((None, 1, d), lambda i: (layer, 0, 0)),
                  pl.BlockSpec((None, None, 6, d), lambda i: (layer, (i * tm) // seq, 0, 0)),
                  pl.BlockSpec((None, d, 128), lambda i: (idx, 0, 0))],
        out_specs=[pl.BlockSpec((tm, d), lambda i: (i, 0)), pl.BlockSpec((tm, 128), lambda i: (i, 0))],
        out_shape=[jax.ShapeDtypeStruct((n, d), F32), jax.ShapeDtypeStruct((n, 128), F32)],
        compiler_params=_params(1), name="moe_router")(x, norm_w, mod, router)


def _moe_plan(route, n_exp, tile):
    n = route.shape[0]
    ef = route[:, :2].astype(jnp.int32).reshape(-1)
    onehot = (ef[:, None] == jnp.arange(n_exp, dtype=jnp.int32)[None, :]).astype(jnp.int32)
    csum = jnp.cumsum(onehot, axis=0)
    rank = jnp.sum(onehot * (csum - 1), axis=-1)
    counts = csum[-1]
    padded = ((counts + tile - 1) // tile) * tile
    ends = jnp.cumsum(padded)
    dest = (ends - padded)[ef] + rank
    rows = 2 * n + n_exp * tile
    src = jnp.zeros((rows,), jnp.int32).at[dest].set(jnp.arange(2 * n, dtype=jnp.int32) // 2)
    tile_start = jnp.arange(rows // tile, dtype=jnp.int32) * tile
    tile_valid = (tile_start < ends[-1]).astype(jnp.int32)
    tile_expert = jnp.sum((ends[None, :] <= tile_start[:, None]).astype(jnp.int32), axis=1)
    tile_expert = jnp.minimum(tile_expert, n_exp - 1)
    last = jnp.maximum(ends[-1] // tile - 1, 0)
    tile_expert = jnp.where(tile_valid > 0, tile_expert, tile_expert[last])
    return dest, src, tile_expert, tile_valid


def _row_copy(src_hbm, row, dst_vmem, slot, sem):
    return pltpu.make_async_copy(src_hbm.at[pl.ds(row, 1), :], dst_vmem.at[pl.ds(slot, 1), :], sem)


def _experts_kernel(te_ref, tv_ref, src_ref, h_hbm, w1_ref, w3_ref, w2_ref, ys_ref,
                    rows_scr, xb_scr, acc_scr, sems, *, n_tiles, nf):
    i = pl.program_id(0)
    f = pl.program_id(1)
    tm = xb_scr.shape[0]
    valid = tv_ref[i] > 0
    slot = i % 2
    per_step = -(-tm // (8 * nf)) * 8
    last_rows = tm - per_step * (nf - 1)
    assert 0 < last_rows <= per_step
    nxt = jnp.minimum(i + 1, n_tiles - 1)

    def drain(sl):
        def body(r, carry):
            _row_copy(h_hbm, 0, rows_scr.at[sl], r, sems.at[sl]).wait()
            return carry
        lax.fori_loop(0, tm, body, 0, unroll=8)

    @pl.when(f == 0)
    def _():
        @pl.when(i == 0)
        def _():
            def body(r, carry):
                _row_copy(h_hbm, src_ref[r], rows_scr.at[0], r, sems.at[0]).start()
                return carry
            lax.fori_loop(0, tm, body, 0, unroll=8)
        drain(slot)
        xb_scr[...] = rows_scr[slot].astype(BF16)
        acc_scr[...] = jnp.zeros(acc_scr.shape, F32)

    def request_next(count):
        first = f * per_step
        for r in range(count):
            _row_copy(h_hbm, src_ref[nxt * tm + first + r], rows_scr.at[1 - slot], first + r,
                      sems.at[1 - slot]).start()

    def matmuls():
        h = xb_scr[...]
        g = _silu(_dot(h, w1_ref[...])) * _dot(h, w3_ref[...])
        acc_scr[...] += _dot(g.astype(BF16), w2_ref[...])

    for final, count in ((False, per_step), (True, last_rows)):
        step = (f == nf - 1) if final else (f < nf - 1)

        @pl.when(step & valid)
        def _():
            request_next(count)
            matmuls()

        @pl.when(step & jnp.logical_not(valid))
        def _():
            request_next(count)

    @pl.when(f == nf - 1)
    def _():
        ys_ref[...] = acc_scr[...]

        @pl.when(i == n_tiles - 1)
        def _():
            drain(1 - slot)


def moe_experts(h, src, tile_expert, tile_valid, w1, w3, w2, idx, *, tf=256):
    n, d = h.shape
    rows = src.shape[0]
    dff = w1.shape[-1]
    tm = MOE_TILE
    nf = dff // tf
    fcol = lambda i, f, tv: jnp.where(tv[i] > 0, f, nf - 1)
    return pl.pallas_call(
        functools.partial(_experts_kernel, n_tiles=rows // tm, nf=nf),
        grid_spec=pltpu.PrefetchScalarGridSpec(
            num_scalar_prefetch=3, grid=(rows // tm, nf),
            in_specs=[pl.BlockSpec(memory_space=pl.ANY),
                      pl.BlockSpec((None, None, d, tf), lambda i, f, te, tv, src: (idx, te[i], 0, fcol(i, f, tv))),
                      pl.BlockSpec((None, None, d, tf), lambda i, f, te, tv, src: (idx, te[i], 0, fcol(i, f, tv))),
                      pl.BlockSpec((None, None, tf, d), lambda i, f, te, tv, src: (idx, te[i], fcol(i, f, tv), 0))],
            out_specs=pl.BlockSpec((tm, d), lambda i, f, te, tv, src: (i, 0)),
            scratch_shapes=[pltpu.VMEM((2, tm, d), F32), pltpu.VMEM((tm, d), BF16), pltpu.VMEM((tm, d), F32),
                            pltpu.SemaphoreType.DMA((2,))]),
        out_shape=jax.ShapeDtypeStruct((rows, d), F32),
        compiler_params=_params(2), name="moe_experts")(tile_expert, tile_valid, src, h, w1, w3, w2)


def _moe_combine_kernel(dest_ref, ys_hbm, x_ref, route_ref, mod_ref, o_ref, buf0, buf1, sems):
    tc = x_ref.shape[0]
    i = pl.program_id(0)
    slot = i % 2

    def issue(tile, sl):
        def body(t, carry):
            a = 2 * (tile * tc + t)
            _row_copy(ys_hbm, dest_ref[a], buf0.at[sl], t, sems.at[sl]).start()
            _row_copy(ys_hbm, dest_ref[a + 1], buf1.at[sl], t, sems.at[sl]).start()
            return carry
        lax.fori_loop(0, tc, body, 0, unroll=4)

    def drain(sl):
        def body(t, carry):
            _row_copy(ys_hbm, 0, buf0.at[sl], t, sems.at[sl]).wait()
            _row_copy(ys_hbm, 0, buf1.at[sl], t, sems.at[sl]).wait()
            return carry
        lax.fori_loop(0, tc, body, 0, unroll=4)

    @pl.when(i == 0)
    def _():
        issue(0, 0)

    @pl.when(i + 1 < pl.num_programs(0))
    def _():
        issue(i + 1, 1 - slot)
    drain(slot)
    route = route_ref[...]
    y = route[:, 2:3] * buf0[slot] + route[:, 3:4] * buf1[slot]
    o_ref[...] = x_ref[...] + mod_ref[5:6, :] * y


def moe_combine(ys, dest, x, route, mod, layer, seq, *, tc=256):
    n, d = x.shape
    return pl.pallas_call(
        _moe_combine_kernel,
        grid_spec=pltpu.PrefetchScalarGridSpec(
            num_scalar_prefetch=1, grid=(n // tc,),
            in_specs=[pl.BlockSpec(memory_space=pl.ANY),
                      pl.BlockSpec((tc, d), lambda i, dest: (i, 0)),
                      pl.BlockSpec((tc, 128), lambda i, dest: (i, 0)),
                      pl.BlockSpec((None, None, 6, d), lambda i, dest: (layer, (i * tc) // seq, 0, 0))],
            out_specs=pl.BlockSpec((tc, d), lambda i, dest: (i, 0)),
            scratch_shapes=[pltpu.VMEM((2, tc, d), F32), pltpu.VMEM((2, tc, d), F32),
                            pltpu.SemaphoreType.DMA((2,))]),
        out_shape=jax.ShapeDtypeStruct((n, d), F32),
        compiler_params=_params(1), name="moe_combine")(dest, ys, x, route, mod)


def moe_residual(x, norm_w, mod, router, w1, w3, w2, layer, idx, seq):
    h, route = moe_router(x, norm_w, mod, router, layer, idx, seq)
    dest, src, tile_expert, tile_valid = _moe_plan(route, w1.shape[1], MOE_TILE)
    ys = moe_experts(h, src, tile_expert, tile_valid, w1, w3, w2, idx)
    return moe_combine(ys, dest, x, route, mod, layer, seq)


def _final_norm_kernel(x_ref, w_ref, o_ref):
    x = x_ref[...]
    o_ref[...] = x * lax.rsqrt(jnp.mean(x * x, axis=-1, keepdims=True) + EPS) * w_ref[...]


def final_norm(x, w, *, tm=512):
    n, d = x.shape
    return pl.pallas_call(
        _final_norm_kernel, grid=(n // tm,),
        in_specs=[pl.BlockSpec((tm, d), lambda i: (i, 0)), pl.BlockSpec((1, d), lambda i: (0, 0))],
        out_specs=pl.BlockSpec((tm, d), lambda i: (i, 0)),
        out_shape=jax.ShapeDtypeStruct((n, d), F32),
        compiler_params=_params(1), name="final_norm")(x, w.reshape(1, d))


_SRC_A, _SRC_B, _SRC_BG, _SRC_C, _SRC_CIF, _SRC_CO = 0, 1024, 2304, 2316, 4364, 4372
IN_COLS = 5396


def _w_in_prep_kernel(w_ref, o_ref):
    rows = w_ref.shape[0]

    def put(dst, src, width):
        o_ref[:, dst:dst + width] = w_ref[:, src:src + width].astype(BF16)

    put(COL_AQ, _SRC_A, _SRC_B - _SRC_A)
    put(COL_CQK, _SRC_C, _SRC_CIF - _SRC_C)
    put(COL_CO, _SRC_CO, IN_COLS - _SRC_CO)
    put(COL_BQ, _SRC_B, _SRC_BG - _SRC_B)
    n_small = (_SRC_C - _SRC_BG) + (_SRC_CO - _SRC_CIF)
    small = jnp.concatenate([w_ref[:, _SRC_BG:_SRC_C], w_ref[:, _SRC_CIF:_SRC_CO],
                             jnp.zeros((rows, 128 - n_small), F32)], axis=1)
    o_ref[:, COL_SMALL:COL_SMALL + 128] = small.astype(BF16)
    o_ref[:, COL_SMALL + 128:] = jnp.zeros((rows, PROJ_COLS - COL_SMALL - 128), BF16)


def _reorder_w_in(w_in, *, tr=256):
    depth, d, cols = w_in.shape
    assert cols == IN_COLS and d % tr == 0
    return pl.pallas_call(
        _w_in_prep_kernel, grid=(depth, d // tr),
        in_specs=[pl.BlockSpec((None, tr, cols), lambda l, i: (l, i, 0))],
        out_specs=pl.BlockSpec((None, tr, PROJ_COLS), lambda l, i: (l, i, 0)),
        out_shape=jax.ShapeDtypeStruct((depth, d, PROJ_COLS), BF16),
        compiler_params=_params(2), name="w_in_prep")(w_in)


def hybrid_mixer(proj, tables, layer, a_sinks, mlstm_gate_b, nsa_pe_k, nsa_pe_v, nsa_ck_w1, nsa_ck_w2,
                 nsa_cv_w1, nsa_cv_w2, mlstm_conv_w, mlstm_norm_w):
    bsz, t, _ = proj.shape
    o_a = sliding_window_attention(proj, col_q=COL_AQ, col_k=COL_AK, col_v=COL_AV, n_heads=A_HEADS,
                                   n_kv=A_KV_HEADS, window=A_WINDOW, tq=256, tables=tables,
                                   sinks=a_sinks[layer])
    chunks = lambda col: proj[:, :, col:col + HEAD_DIM].reshape(bsz, t // CMP_STRIDE, CMP_STRIDE * HEAD_DIM)
    k_cmp = compress_blocks(chunks(COL_BKC), nsa_pe_k, nsa_ck_w1, nsa_ck_w2, layer)
    v_cmp = compress_blocks(chunks(COL_BVC), nsa_pe_v, nsa_cv_w1, nsa_cv_w2, layer)
    o_cmp, bias_t = compressed_attention_select(proj, k_cmp, v_cmp)
    o_sel = selected_attention(proj, bias_t)
    o_b = sliding_window_attention(proj, col_q=COL_BQ, col_k=COL_BKW, col_v=COL_BVW, n_heads=B_HEADS,
                                   n_kv=1, window=B_WINDOW, tq=512, gate_with=(o_cmp, o_sel))
    o_c = mlstm_mixer(proj, mlstm_gate_b, mlstm_conv_w, mlstm_norm_w, layer)
    return o_a, o_b, o_c


def kernel(x, c, positions, ada_w, ada_b, norm_mix_w, norm_ffn_w, w_in, mlstm_gate_b, a_sinks, nsa_pe_k,
           nsa_pe_v, nsa_ck_w1, nsa_ck_w2, nsa_cv_w1, nsa_cv_w2, mlstm_conv_w, mlstm_norm_w, w_out, ffn_w1,
           ffn_w3, ffn_w2, moe_router, moe_w1, moe_w3, moe_w2, final_norm_w):
    bsz, t, d = x.shape
    depth = ada_w.shape[0]
    mod = ada_modulation(c, ada_w, ada_b)
    tables = rope_tables(positions)
    w_in_p = _reorder_w_in(w_in)
    w_out_b = w_out.astype(BF16)
    ffn_b = [w.astype(BF16) for w in (ffn_w1, ffn_w3, ffn_w2)]
    moe_b = [w.astype(BF16) for w in (moe_w1, moe_w3, moe_w2)]
    router_p = jnp.pad(moe_router, ((0, 0), (0, 0), (0, 128 - N_EXPERTS)))
    norm_mix = norm_mix_w.reshape(depth, 1, d)
    norm_ffn = norm_ffn_w.reshape(depth, 1, d)
    conv_w = mlstm_conv_w
    norm_c = mlstm_norm_w.reshape(depth, 1, -1)

    xf = x.reshape(bsz * t, d)
    for layer in range(depth):
        proj = norm_proj(xf, norm_mix, mod, w_in_p, layer, t, shift_row=0, scale_row=1)
        o_a, o_b, o_c = hybrid_mixer(proj.reshape(bsz, t, PROJ_COLS), tables, layer, a_sinks, mlstm_gate_b,
                                     nsa_pe_k, nsa_pe_v, nsa_ck_w1, nsa_ck_w2, nsa_cv_w1, nsa_cv_w2,
                                     conv_w, norm_c)
        flat = lambda a: a.reshape(bsz * t, a.shape[-1])
        xf = out_proj_residual(flat(o_a), flat(o_b), flat(o_c), w_out_b, xf, mod, layer, t)
        if layer % 2 == 0:
            xf = ffn_residual(xf, norm_ffn, mod, *ffn_b, layer, layer // 2, t)
        else:
            xf = moe_residual(xf, norm_ffn, mod, router_p, *moe_b, layer, layer // 2, t)
    return final_norm(xf, final_norm_w).reshape(bsz, t, d)
```

```python
import functools
import math

import numpy as np
import jax
import jax.numpy as jnp
from jax import lax
from jax.experimental import pallas as pl
from jax.experimental.pallas import tpu as pltpu

F32 = jnp.float32
BF16 = jnp.bfloat16

HEAD_DIM = 128
A_HEADS, A_KV_HEADS, A_WINDOW = 4, 2, 128
ROPE_THETA = 150000.0
B_HEADS = 4
CMP_LEN, CMP_STRIDE = 32, 16
SEL_LEN, SEL_TOPK, B_WINDOW = 64, 16, 512
C_HEADS, C_QK_DIM, C_V_DIM, C_CONV = 4, 128, 256, 4
N_EXPERTS = 8
EPS = 1e-6
BIG = 1e9
NEG = -1e30
SCALE = HEAD_DIM ** -0.5

COL_AQ, COL_AK, COL_AV = 0, 512, 768
COL_CQK, COL_CV, COL_CO = 1024, 2048, 3072
COL_BQ = 4096
COL_BKC, COL_BVC, COL_BKS, COL_BVS, COL_BKW, COL_BVW = 4608, 4736, 4864, 4992, 5120, 5248
COL_SMALL = 5376
LANE_CI, LANE_CF = 12, 16
PROJ_COLS = 5632

VMEM_LIMIT_MB = 56
BF16_SUBLANES = 16


def _params(n_axes, vmem_mb=VMEM_LIMIT_MB):
    return pltpu.CompilerParams(dimension_semantics=("arbitrary",) * n_axes,
                                vmem_limit_bytes=vmem_mb * 2 ** 20)


def _dot(a, b):
    return jnp.dot(a, b, preferred_element_type=F32)


def _dot_nt(a, b):
    return lax.dot_general(a, b, (((1,), (1,)), ((), ())), preferred_element_type=F32)


def _silu(v):
    return v * jax.nn.sigmoid(v)


def _norm_mod(x, nw, shift, scale):
    ms = jnp.mean(x * x, axis=-1, keepdims=True)
    return (x * lax.rsqrt(ms + EPS) * nw) * (1.0 + scale) + shift


def _ada_kernel(c_ref, w_ref, b_ref, o_ref):
    act = _silu(c_ref[...]).astype(BF16)
    o_ref[...] = _dot(act, w_ref[...].astype(BF16)) + b_ref[...]


def ada_modulation(c, ada_w, ada_b):
    depth, d, n6 = ada_w.shape
    bsz = c.shape[0]
    assert bsz <= 8
    cp = jnp.zeros((8, d), F32).at[:bsz].set(c)
    tn = 512
    out = pl.pallas_call(
        _ada_kernel, grid=(depth, n6 // tn),
        in_specs=[pl.BlockSpec((8, d), lambda l, j: (0, 0)),
                  pl.BlockSpec((None, d, tn), lambda l, j: (l, 0, j)),
                  pl.BlockSpec((None, 1, tn), lambda l, j: (l, 0, j))],
        out_specs=pl.BlockSpec((None, 8, tn), lambda l, j: (l, 0, j)),
        out_shape=jax.ShapeDtypeStruct((depth, 8, n6), F32),
        compiler_params=_params(2), name="ada_modulation")(cp, ada_w, ada_b.reshape(depth, 1, n6))
    return out.reshape(depth, 8, 6, d)


def _rope_table_kernel(pos_ref, cos_ref, sin_ref):
    pos = pos_ref[...].astype(F32)
    lane = lax.broadcasted_iota(jnp.int32, (1, HEAD_DIM), 1)
    half = HEAD_DIM // 2
    inv = jnp.exp((lane & (half - 1)).astype(F32) * (-math.log(ROPE_THETA) * 2.0 / HEAD_DIM))
    ang = pos * inv
    cos_ref[...] = jnp.cos(ang)
    sin_ref[...] = jnp.where(lane < half, -1.0, 1.0) * jnp.sin(ang)


def rope_tables(positions):
    bsz, t = positions.shape
    tt = 512
    spec = pl.BlockSpec((None, tt, HEAD_DIM), lambda b, i: (b, i, 0))
    return pl.pallas_call(
        _rope_table_kernel, grid=(bsz, t // tt),
        in_specs=[pl.BlockSpec((None, tt, 1), lambda b, i: (b, i, 0))],
        out_specs=[spec, spec],
        out_shape=[jax.ShapeDtypeStruct((bsz, t, HEAD_DIM), F32)] * 2,
        compiler_params=_params(2), name="rope_tables")(positions.reshape(bsz, t, 1))


def _proj_kernel(x_ref, nw_ref, mod_ref, w_ref, o_ref, h_scr, *, shift_row, scale_row):
    @pl.when(pl.program_id(1) == 0)
    def _():
        h = _norm_mod(x_ref[...], nw_ref[...], mod_ref[shift_row:shift_row + 1, :],
                      mod_ref[scale_row:scale_row + 1, :])
        h_scr[...] = h.astype(BF16)

    o_ref[...] = _dot(h_scr[...], w_ref[...])


def norm_proj(x, norm_w, mod, w, layer, seq, *, shift_row, scale_row, tm=1024, tn=1408):
    n, d = x.shape
    p = w.shape[-1]
    assert seq % tm == 0 and p % tn == 0
    return pl.pallas_call(
        functools.partial(_proj_kernel, shift_row=shift_row, scale_row=scale_row),
        grid=(n // tm, p // tn),
        in_specs=[pl.BlockSpec((tm, d), lambda i, j: (i, 0)),
                  pl.BlockSpec((None, 1, d), lambda i, j: (layer, 0, 0)),
                  pl.BlockSpec((None, None, 6, d), lambda i, j: (layer, (i * tm) // seq, 0, 0)),
                  pl.BlockSpec((None, d, tn), lambda i, j: (layer, 0, j))],
        out_specs=pl.BlockSpec((tm, tn), lambda i, j: (i, j)),
        out_shape=jax.ShapeDtypeStruct((n, p), F32),
        scratch_shapes=[pltpu.VMEM((tm, d), BF16)],
        compiler_params=_params(2), name="norm_proj")(x, norm_w, mod, w)


SWA_SUB = 128


def _swa_kernel(*refs, tq, window, n_heads, group, rope, sinks, gated):
    it = iter(refs)
    q_ref, kp_ref, kc_ref, vp_ref, vc_ref = (next(it) for _ in range(5))
    if rope:
        cp_ref, sp_ref, cc_ref, sc_ref = (next(it) for _ in range(4))
    if sinks:
        sink_ref = next(it)
    if gated:
        g_ref, cmp_ref, sel_ref = (next(it) for _ in range(3))
        gates = jax.nn.sigmoid(g_ref[...])
    o_ref = next(it)
    i = pl.program_id(1)
    sub = SWA_SUB
    pad = -(-(window - 1) // sub) * sub
    span = pad + sub
    row = lax.broadcasted_iota(jnp.int32, (sub, span), 0)
    col = lax.broadcasted_iota(jnp.int32, (sub, span), 1)
    diff = row + pad - col

    def window_mask(n_prev):
        d = diff if n_prev == 0 else diff + jnp.where(col < n_prev, jnp.where(i > 0, 0, window), 0)
        return jnp.abs(2 * d - (window - 1)) <= (window - 1)
    masks = {}

    def rotate(v, c, s):
        return v * c + pltpu.roll(v, HEAD_DIM // 2, 1) * s

    kv_cache = {}
    for hq in range(n_heads):
        kv = hq // group
        if kv not in kv_cache:
            sl = slice(kv * HEAD_DIM, (kv + 1) * HEAD_DIM)
            kp, kc = kp_ref[:, sl], kc_ref[:, sl]
            if rope:
                kp = rotate(kp, cp_ref[...], sp_ref[...])
                kc = rotate(kc, cc_ref[...], sc_ref[...])
            k = jnp.concatenate([kp, kc], axis=0).astype(BF16)
            v = jnp.concatenate([vp_ref[:, sl], vc_ref[:, sl]], axis=0).astype(BF16)
            kv_cache[kv] = (k, v)
        k, v = kv_cache[kv]
        hs = slice(hq * HEAD_DIM, (hq + 1) * HEAD_DIM)
        q = q_ref[:, hs]
        if rope:
            q = rotate(q, cc_ref[...], sc_ref[...])
        q = (q * SCALE).astype(BF16)
        for j in range(tq // sub):
            rows = slice(j * sub, (j + 1) * sub)
            lo = tq + j * sub - pad
            n_prev = max(tq - lo, 0)
            if n_prev not in masks:
                masks[n_prev] = window_mask(n_prev)
            s = _dot_nt(q[rows], k[lo:lo + span])
            s = jnp.where(masks[n_prev], s, NEG)
            m = jnp.max(s, axis=-1, keepdims=True)
            if sinks:
                m = jnp.maximum(m, sink_ref[hq])
            m = jnp.where(m < 0.5 * NEG, 0.0, m)
            p = jnp.exp(s - m)
            den = jnp.sum(p, axis=-1, keepdims=True)
            if sinks:
                den = den + jnp.exp(sink_ref[hq] - m)
            o = _dot(p.astype(BF16), v[lo:lo + span]) / jnp.maximum(den, 1e-30)
            if gated:
                g = gates[rows]
                o = (g[:, 3 * hq:3 * hq + 1] * cmp_ref[rows, hs] + g[:, 3 * hq + 1:3 * hq + 2] * sel_ref[rows, hs]
                     + g[:, 3 * hq + 2:3 * hq + 3] * o)
            o_ref[rows, hs] = o.astype(o_ref.dtype)


def sliding_window_attention(proj, *, col_q, col_k, col_v, n_heads, n_kv, window, tq,
                             tables=None, sinks=None, gate_with=None):
    bsz, t, _ = proj.shape
    assert window - 1 <= tq and t % tq == 0 and tq % SWA_SUB == 0
    wq, wkv = n_heads * HEAD_DIM, n_kv * HEAD_DIM
    assert col_q % wq == 0 and col_k % wkv == 0 and col_v % wkv == 0
    cur = lambda b, i: (b, i, 0)
    prev = lambda b, i: (b, jnp.maximum(i - 1, 0), 0)
    kvspec = lambda col, im: pl.BlockSpec(
        (None, tq, wkv), lambda b, i: (b, im(b, i)[1], col // wkv))
    in_specs = [pl.BlockSpec((None, tq, wq), lambda b, i: (b, i, col_q // wq)),
                kvspec(col_k, prev), kvspec(col_k, cur), kvspec(col_v, prev), kvspec(col_v, cur)]
    args = [proj] * 5
    if tables is not None:
        cos, sin = tables
        tp = pl.BlockSpec((None, tq, HEAD_DIM), prev)
        tc = pl.BlockSpec((None, tq, HEAD_DIM), cur)
        in_specs += [tp, tp, tc, tc]
        args += [cos, sin, cos, sin]
    if sinks is not None:
        in_specs.append(pl.BlockSpec(memory_space=pltpu.SMEM))
        args.append(sinks)
    if gate_with is not None:
        branch = pl.BlockSpec((None, tq, wq), cur)
        in_specs += [pl.BlockSpec((None, tq, 128), lambda b, i: (b, i, COL_SMALL // 128)), branch, branch]
        args += [proj, *gate_with]
    return pl.pallas_call(
        functools.partial(_swa_kernel, tq=tq, window=window, n_heads=n_heads, group=n_heads // n_kv,
                          rope=tables is not None, sinks=sinks is not None, gated=gate_with is not None),
        grid=(bsz, t // tq), in_specs=in_specs,
        out_specs=pl.BlockSpec((None, tq, wq), cur),
        out_shape=jax.ShapeDtypeStruct((bsz, t, wq), BF16),
        compiler_params=_params(2), name="sliding_window_attention")(*args)


def _compress_kernel(r_ref, pe_ref, w1_ref, w2_ref, o_ref):
    n16, half = r_ref.shape
    r = r_ref[...].astype(BF16)
    w1 = w1_ref[...].astype(BF16)
    first = _dot(r, w1[:half])
    second = _dot(r, w1[half:])
    pe = _dot(pe_ref[...].astype(BF16), w1)[0:1]
    pre = first + pltpu.roll(second, n16 - 1, 0) + pe
    hid = jax.nn.gelu(pre, approximate=True)
    o_ref[...] = _dot(hid.astype(BF16), w2_ref[...].astype(BF16))


def compress_blocks(chunks, pe, w1, w2, layer):
    bsz, n16, half = chunks.shape
    hid = w1.shape[-1]
    pe8 = jnp.broadcast_to(pe[layer].reshape(1, 2 * half), (8, 2 * half))
    return pl.pallas_call(
        _compress_kernel, grid=(bsz,),
        in_specs=[pl.BlockSpec((None, n16, half), lambda b: (b, 0, 0)),
                  pl.BlockSpec((8, 2 * half), lambda b: (0, 0)),
                  pl.BlockSpec((None, 2 * half, hid), lambda b: (layer, 0, 0)),
                  pl.BlockSpec((None, hid, HEAD_DIM), lambda b: (layer, 0, 0))],
        out_specs=pl.BlockSpec((None, n16, HEAD_DIM), lambda b: (b, 0, 0)),
        out_shape=jax.ShapeDtypeStruct((bsz, n16, HEAD_DIM), F32),
        compiler_params=_params(1), name="compress_blocks")(chunks, pe8, w1, w2)


def _heads_transposed(q, tq, scale=SCALE):
    n = q.shape[1] // HEAD_DIM
    return jnp.concatenate([(q[:, g * HEAD_DIM:(g + 1) * HEAD_DIM] * scale).T for g in range(n)],
                           axis=1).astype(BF16)


def _cmp_select_kernel(q_ref, kc_ref, vc_ref, ov_ref, ocmp_ref, bias_ref, *, tq, n_sel, topk):
    i = pl.program_id(1)
    nc = kc_ref.shape[0]
    hq = B_HEADS * tq
    qt = _heads_transposed(q_ref[...], tq)
    st = _dot(kc_ref[...].astype(BF16), qt)
    n_row = lax.broadcasted_iota(jnp.int32, (nc, hq), 0)
    t_col = i * tq + (lax.broadcasted_iota(jnp.int32, (nc, hq), 1) & (tq - 1))
    st = jnp.where(n_row * CMP_STRIDE + (CMP_LEN - 1) <= t_col, st, NEG)
    m = jnp.max(st, axis=0, keepdims=True)
    m = jnp.where(m < 0.5 * NEG, 0.0, m)
    p = jnp.exp(st - m)
    p = p / jnp.maximum(jnp.sum(p, axis=0, keepdims=True), 1e-30)
    pb = p.astype(BF16)
    ot = _dot(vc_ref[...].T.astype(BF16), pb)
    for g in range(B_HEADS):
        ocmp_ref[:, g * HEAD_DIM:(g + 1) * HEAD_DIM] = ot[:, g * tq:(g + 1) * tq].T
    imp4 = _dot(ov_ref[...], pb)
    imp = imp4[:, 0:tq]
    for g in range(1, B_HEADS):
        imp = imp + imp4[:, g * tq:(g + 1) * tq]
    nsp = imp.shape[0]
    j = lax.broadcasted_iota(jnp.int32, (nsp, tq), 0)
    t = i * tq + lax.broadcasted_iota(jnp.int32, (nsp, tq), 1)
    cur = t >> 6
    imp = jnp.where((j << 6) <= t, imp, -BIG)
    for forced in (0, cur, cur - 1):
        imp = jnp.where(j == forced, BIG, imp)
    imp = jnp.where(j < n_sel, imp, -3e38)
    sel = jnp.zeros((nsp, tq), F32)
    for _ in range(topk):
        mx = jnp.max(imp, axis=0, keepdims=True)
        idx = jnp.min(jnp.where(imp == mx, j, nsp), axis=0, keepdims=True)
        hit = j == idx
        sel = jnp.where(hit, 1.0, sel)
        imp = jnp.where(hit, -3e38, imp)
    bias_ref[...] = jnp.where(sel > 0.0, 0.0, jnp.where(j < n_sel, -BIG, 0.0))


def _overlap_matrix_t(t):
    n16, n_sel = t // CMP_STRIDE, t // SEL_LEN
    nsp = -(-n_sel // 128) * 128
    starts = np.arange(n16) * CMP_STRIDE
    sel_start = np.arange(n_sel) * SEL_LEN
    ov = np.clip(np.minimum(starts[:, None] + CMP_LEN, sel_start[None, :] + SEL_LEN)
                 - np.maximum(starts[:, None], sel_start[None, :]), 0, None).astype(np.float32) / CMP_LEN
    out = np.zeros((nsp, n16), np.float32)
    out[:n_sel] = ov.T
    return jnp.asarray(out, BF16)


def compressed_attention_select(proj, k_cmp, v_cmp, *, tq=256):
    bsz, t, _ = proj.shape
    n16, n_sel = t // CMP_STRIDE, t // SEL_LEN
    assert SEL_LEN == 64 and tq % 128 == 0 and tq & (tq - 1) == 0 and t % tq == 0
    ov = _overlap_matrix_t(t)
    nsp = ov.shape[0]
    wq = B_HEADS * HEAD_DIM
    return pl.pallas_call(
        functools.partial(_cmp_select_kernel, tq=tq, n_sel=n_sel, topk=min(SEL_TOPK, n_sel)),
        grid=(bsz, t // tq),
        in_specs=[pl.BlockSpec((None, tq, wq), lambda b, i: (b, i, COL_BQ // wq)),
                  pl.BlockSpec((None, n16, HEAD_DIM), lambda b, i: (b, 0, 0)),
                  pl.BlockSpec((None, n16, HEAD_DIM), lambda b, i: (b, 0, 0)),
                  pl.BlockSpec((nsp, n16), lambda b, i: (0, 0))],
        out_specs=[pl.BlockSpec((None, tq, wq), lambda b, i: (b, i, 0)),
                   pl.BlockSpec((None, nsp, tq), lambda b, i: (b, 0, i))],
        out_shape=[jax.ShapeDtypeStruct((bsz, t, wq), F32),
                   jax.ShapeDtypeStruct((bsz, nsp, t), F32)],
        compiler_params=_params(2), name="compressed_attention_select")(proj, k_cmp, v_cmp, ov)


def _selected_kernel(q_ref, bias_ref, ks_ref, vs_ref, o_ref, kaug_scr, vt_scr, qa_scr, s0_scr, s1_scr,
                     m_scr, acc_scr, *, tq, kb):
    i = pl.program_id(1)
    t_all = ks_ref.shape[0]
    nsp = bias_ref.shape[0]
    hq = B_HEADS * tq
    ones_rows = vt_scr.shape[1] - HEAD_DIM

    @pl.when(i == 0)
    def _():
        def prep(c, carry):
            r0 = pl.multiple_of(c * kb, kb)
            kaug_scr[pl.ds(r0, kb), 0:HEAD_DIM] = ks_ref[pl.ds(r0, kb), :].astype(BF16)
            key = r0 + lax.broadcasted_iota(jnp.int32, (kb, nsp), 0)
            blk = lax.broadcasted_iota(jnp.int32, (kb, nsp), 1)
            kaug_scr[pl.ds(r0, kb), HEAD_DIM:] = jnp.where((key >> 6) == blk, 1.0, 0.0).astype(BF16)
            vt_scr[c] = jnp.concatenate([vs_ref[pl.ds(r0, kb), :].T, jnp.ones((ones_rows, kb), F32)],
                                        axis=0).astype(BF16)
            return carry
        lax.fori_loop(0, t_all // kb, prep, 0)

    bias = bias_ref[...]
    qa_scr[...] = jnp.concatenate([_heads_transposed(q_ref[...], tq, SCALE * math.log2(math.e)),
                                   jnp.concatenate([bias] * B_HEADS, axis=1).astype(BF16)], axis=0)
    m_scr[...] = jnp.full(m_scr.shape, NEG, F32)
    acc_scr[...] = jnp.zeros(acc_scr.shape, F32)

    def scores(kt, s_out):
        r0 = pl.multiple_of(kt * kb, kb)
        s_out[...] = _dot(kaug_scr[pl.ds(r0, kb), :], qa_scr[...])

    def consume(kt, s_in, causal):
        st = s_in[...]
        if causal:
            key = kt * kb + lax.broadcasted_iota(jnp.int32, (kb, hq), 0)
            tpos = i * tq + (lax.broadcasted_iota(jnp.int32, (kb, hq), 1) & (tq - 1))
            st = jnp.where(key <= tpos, st, NEG)
        m_old = m_scr[...]
        m_new = jnp.maximum(m_old, jnp.max(st, axis=0, keepdims=True))
        p = jnp.exp2((st - m_new).astype(BF16))
        acc_scr[...] = jnp.exp2(m_old - m_new) * acc_scr[...] + _dot(vt_scr[kt], p)
        m_scr[...] = m_new

    def step(kt, s_in, s_out):
        scores(kt + 1, s_out)
        consume(kt, s_in, False)

    def finish(kt, s_in):
        consume(kt, s_in, True)
        ot = acc_scr[0:HEAD_DIM, :] / jnp.maximum(acc_scr[HEAD_DIM:HEAD_DIM + 1, :], 1e-30)
        for g in range(B_HEADS):
            o_ref[:, g * HEAD_DIM:(g + 1) * HEAD_DIM] = ot[:, g * tq:(g + 1) * tq].T

    n_full = (i * tq) // kb
    scores(0, s0_scr)

    def pair(j, carry):
        step(2 * j, s0_scr, s1_scr)
        step(2 * j + 1, s1_scr, s0_scr)
        return carry
    lax.fori_loop(0, n_full // 2, pair, 0)

    @pl.when(n_full % 2 == 1)
    def _():
        step(n_full - 1, s0_scr, s1_scr)
        finish(n_full, s1_scr)

    @pl.when(n_full % 2 == 0)
    def _():
        finish(n_full, s0_scr)


def selected_attention(proj, bias_t, *, tq=256, kb=512):
    bsz, t, _ = proj.shape
    nsp = bias_t.shape[1]
    wq = B_HEADS * HEAD_DIM
    assert kb % tq == 0 and t % kb == 0
    return pl.pallas_call(
        functools.partial(_selected_kernel, tq=tq, kb=kb),
        grid=(bsz, t // tq),
        in_specs=[pl.BlockSpec((None, tq, wq), lambda b, i: (b, i, COL_BQ // wq)),
                  pl.BlockSpec((None, nsp, tq), lambda b, i: (b, 0, i)),
                  pl.BlockSpec((None, t, HEAD_DIM), lambda b, i: (b, 0, COL_BKS // HEAD_DIM)),
                  pl.BlockSpec((None, t, HEAD_DIM), lambda b, i: (b, 0, COL_BVS // HEAD_DIM))],
        out_specs=pl.BlockSpec((None, tq, wq), lambda b, i: (b, i, 0)),
        out_shape=jax.ShapeDtypeStruct((bsz, t, wq), F32),
        scratch_shapes=[pltpu.VMEM((t, HEAD_DIM + nsp), BF16),
                        pltpu.VMEM((t // kb, HEAD_DIM + BF16_SUBLANES, kb), BF16),
                        pltpu.VMEM((HEAD_DIM + nsp, B_HEADS * tq), BF16),
                        pltpu.VMEM((kb, B_HEADS * tq), F32),
                        pltpu.VMEM((kb, B_HEADS * tq), F32),
                        pltpu.VMEM((1, B_HEADS * tq), F32),
                        pltpu.VMEM((HEAD_DIM + BF16_SUBLANES, B_HEADS * tq), F32)],
        compiler_params=_params(2), name="selected_attention")(proj, bias_t, proj, proj)


def _mlstm_kernel(gb_ref, qk_ref, v_ref, og_ref, small_ref, convw_ref, normw_ref, o_ref,
                  hist_scr, c_scr, n_scr, m_scr, *, chunk):
    L = chunk
    dqk, dv = C_QK_DIM, C_V_DIM

    @pl.when(pl.program_id(1) == 0)
    def _():
        hist_scr[...] = jnp.zeros(hist_scr.shape, F32)
        c_scr[...] = jnp.zeros(c_scr.shape, F32)
        n_scr[...] = jnp.zeros(n_scr.shape, F32)
        m_scr[...] = jnp.zeros(m_scr.shape, F32)

    x = qk_ref[...]
    xe = jnp.concatenate([hist_scr[...], x], axis=0)
    w = convw_ref[...]
    y = w[0:1] * xe[8 - 3:8 - 3 + L]
    for tap in range(1, C_CONV):
        y = y + w[tap:tap + 1] * xe[8 - 3 + tap:8 - 3 + tap + L]
    hist_scr[...] = x[L - 8:L]
    qk = _silu(y)

    lane = lax.broadcasted_iota(jnp.int32, (1, 128), 1)
    gbias = jnp.zeros((1, 128), F32)
    for idx in range(2 * C_HEADS):
        gbias = jnp.where(lane == LANE_CI + idx, gb_ref[idx], gbias)
    pre = small_ref[...] + gbias
    logf = jnp.minimum(pre, 0.0) - jnp.log1p(jnp.exp(-jnp.abs(pre)))
    row = lax.broadcasted_iota(jnp.int32, (L, L), 0)
    col = lax.broadcasted_iota(jnp.int32, (L, L), 1)
    causal = row >= col
    tri = jnp.where(causal, 1.0, 0.0)
    hp = lax.Precision.HIGHEST
    b_cols = jnp.dot(tri, logf, precision=hp, preferred_element_type=F32)
    b_rows = lax.dot_general(logf.T, tri, (((1,), (1,)), ((), ())), precision=hp,
                             preferred_element_type=F32)
    pre_t = pre.T

    for h in range(C_HEADS):
        q = qk[:, h * dqk:(h + 1) * dqk]
        k = qk[:, C_HEADS * dqk + h * dqk:C_HEADS * dqk + (h + 1) * dqk] * (dqk ** -0.5)
        v = v_ref[:, h * dv:(h + 1) * dv].astype(BF16)
        b_col = b_cols[:, LANE_CF + h:LANE_CF + h + 1]
        i_col = pre[:, LANE_CI + h:LANE_CI + h + 1]
        b_row = b_rows[LANE_CF + h:LANE_CF + h + 1, :]
        i_row = pre_t[LANE_CI + h:LANE_CI + h + 1, :]
        m_prev = m_scr[h:h + 1, 0:1]
        ct = c_scr[h]
        n_row = n_scr[h:h + 1, :]

        d = jnp.where(causal, b_col - b_row + i_row, NEG)
        m_inter = b_col + m_prev
        m_t = jnp.maximum(m_inter, jnp.max(d, axis=-1, keepdims=True))
        qb = q.astype(BF16)
        s = _dot_nt(qb, k.astype(BF16)) * jnp.exp(d - m_t)
        inter = jnp.exp(m_inter - m_t)
        num = _dot(s.astype(BF16), v) + inter * _dot(qb, ct.astype(BF16))
        den = jnp.sum(s, axis=-1, keepdims=True) + inter * jnp.sum(q * n_row, axis=-1, keepdims=True)
        hh = num / jnp.maximum(jnp.abs(den), jnp.exp(-m_t))

        b_last = b_col[L - 1:L, :]
        g = b_last - b_col + i_col
        m_new = jnp.maximum(b_last + m_prev, jnp.max(g, axis=0, keepdims=True))
        kd = k * jnp.exp(g - m_new)
        keep = jnp.exp(b_last + m_prev - m_new)
        c_scr[h] = keep * ct + _dot(kd.T.astype(BF16), v)
        n_scr[h:h + 1, :] = keep * n_row + jnp.sum(kd, axis=0, keepdims=True)
        m_scr[h:h + 1, :] = jnp.broadcast_to(m_new, (1, 128))

        vs = slice(h * dv, (h + 1) * dv)
        hn = hh * lax.rsqrt(jnp.mean(hh * hh, axis=-1, keepdims=True) + EPS) * normw_ref[:, vs]
        o_ref[:, vs] = (hn * jax.nn.sigmoid(og_ref[:, vs])).astype(o_ref.dtype)


def mlstm_mixer(proj, gate_b, conv_w, norm_w, layer, *, chunk=128):
    bsz, t, _ = proj.shape
    wqk, wv = 2 * C_HEADS * C_QK_DIM, C_HEADS * C_V_DIM
    assert wqk == wv == 1024 and t % chunk == 0
    blk = lambda col: pl.BlockSpec((None, chunk, 1024), lambda b, i: (b, i, col // 1024))
    return pl.pallas_call(
        functools.partial(_mlstm_kernel, chunk=chunk),
        grid=(bsz, t // chunk),
        in_specs=[pl.BlockSpec(memory_space=pltpu.SMEM),
                  blk(COL_CQK), blk(COL_CV), blk(COL_CO),
                  pl.BlockSpec((None, chunk, 128), lambda b, i: (b, i, COL_SMALL // 128)),
                  pl.BlockSpec((None, C_CONV, wqk), lambda b, i: (layer, 0, 0)),
                  pl.BlockSpec((None, 1, wv), lambda b, i: (layer, 0, 0))],
        out_specs=pl.BlockSpec((None, chunk, wv), lambda b, i: (b, i, 0)),
        out_shape=jax.ShapeDtypeStruct((bsz, t, wv), BF16),
        scratch_shapes=[pltpu.VMEM((8, wqk), F32),
                        pltpu.VMEM((C_HEADS, C_QK_DIM, C_V_DIM), F32),
                        pltpu.VMEM((8, C_QK_DIM), F32),
                        pltpu.VMEM((8, 128), F32)],
        compiler_params=_params(2), name="mlstm_mixer")(
            gate_b[layer], proj, proj, proj, proj, conv_w, norm_w)


def _outproj_kernel(oa_ref, ob_ref, oc_ref, w_ref, x_ref, mod_ref, o_ref, *, gate_row):
    wa, wb = oa_ref.shape[1], ob_ref.shape[1]
    acc = _dot(oa_ref[...], w_ref[0:wa, :])
    acc = acc + _dot(ob_ref[...], w_ref[wa:wa + wb, :])
    acc = acc + _dot(oc_ref[...], w_ref[wa + wb:, :])
    o_ref[...] = x_ref[...] + mod_ref[gate_row:gate_row + 1, :] * acc


def out_proj_residual(o_a, o_b, o_c, w_out, x, mod, layer, seq, *, tm=1024, tn=1024):
    n, d = x.shape
    assert seq % tm == 0 and d % tn == 0
    wa, wb, wc = o_a.shape[1], o_b.shape[1], o_c.shape[1]
    return pl.pallas_call(
        functools.partial(_outproj_kernel, gate_row=2),
        grid=(n // tm, d // tn),
        in_specs=[pl.BlockSpec((tm, wa), lambda i, j: (i, 0)),
                  pl.BlockSpec((tm, wb), lambda i, j: (i, 0)),
                  pl.BlockSpec((tm, wc), lambda i, j: (i, 0)),
                  pl.BlockSpec((None, wa + wb + wc, tn), lambda i, j: (layer, 0, j)),
                  pl.BlockSpec((tm, tn), lambda i, j: (i, j)),
                  pl.BlockSpec((None, None, 6, tn), lambda i, j: (layer, (i * tm) // seq, 0, j))],
        out_specs=pl.BlockSpec((tm, tn), lambda i, j: (i, j)),
        out_shape=jax.ShapeDtypeStruct((n, d), F32),
        compiler_params=_params(2), name="out_proj_residual")(o_a, o_b, o_c, w_out, x, mod)


def _ffn_kernel(x_ref, nw_ref, mod_ref, w1_ref, w3_ref, w2_ref, o_ref, h_scr, acc_scr):
    f = pl.program_id(1)

    @pl.when(f == 0)
    def _():
        h = _norm_mod(x_ref[...], nw_ref[...], mod_ref[3:4, :], mod_ref[4:5, :])
        h_scr[...] = h.astype(BF16)
        acc_scr[...] = jnp.zeros(acc_scr.shape, F32)

    h = h_scr[...]
    g = _silu(_dot(h, w1_ref[...])) * _dot(h, w3_ref[...])
    acc_scr[...] += _dot(g.astype(BF16), w2_ref[...])

    @pl.when(f == pl.num_programs(1) - 1)
    def _():
        o_ref[...] = x_ref[...] + mod_ref[5:6, :] * acc_scr[...]


def ffn_residual(x, norm_w, mod, w1, w3, w2, layer, idx, seq, *, tm=512, tf=512):
    n, d = x.shape
    dff = w1.shape[-1]
    return pl.pallas_call(
        _ffn_kernel, grid=(n // tm, dff // tf),
        in_specs=[pl.BlockSpec((tm, d), lambda i, f: (i, 0)),
                  pl.BlockSpec((None, 1, d), lambda i, f: (layer, 0, 0)),
                  pl.BlockSpec((None, None, 6, d), lambda i, f: (layer, (i * tm) // seq, 0, 0)),
                  pl.BlockSpec((None, d, tf), lambda i, f: (idx, 0, f)),
                  pl.BlockSpec((None, d, tf), lambda i, f: (idx, 0, f)),
                  pl.BlockSpec((None, tf, d), lambda i, f: (idx, f, 0))],
        out_specs=pl.BlockSpec((tm, d), lambda i, f: (i, 0)),
        out_shape=jax.ShapeDtypeStruct((n, d), F32),
        scratch_shapes=[pltpu.VMEM((tm, d), BF16), pltpu.VMEM((tm, d), F32)],
        compiler_params=_params(2), name="ffn_residual")(x, norm_w, mod, w1, w3, w2)


MOE_TILE = 768


def _router_kernel(x_ref, nw_ref, mod_ref, r_ref, h_ref, route_ref):
    tm = x_ref.shape[0]
    lane = lax.broadcasted_iota(jnp.int32, (tm, 128), 1)
    h = _norm_mod(x_ref[...], nw_ref[...], mod_ref[3:4, :], mod_ref[4:5, :])
    h_ref[...] = h
    logits = jnp.dot(h, r_ref[...], precision=lax.Precision.HIGHEST, preferred_element_type=F32)
    logits = jnp.where(lane < N_EXPERTS, logits, NEG)
    v1 = jnp.max(logits, axis=-1, keepdims=True)
    i1 = jnp.min(jnp.where(logits == v1, lane, 128), axis=-1, keepdims=True)
    rest = jnp.where(lane == i1, NEG, logits)
    v2 = jnp.max(rest, axis=-1, keepdims=True)
    i2 = jnp.min(jnp.where(rest == v2, lane, 128), axis=-1, keepdims=True)
    e2 = jnp.exp(v2 - v1)
    route = jnp.where(lane == 0, i1.astype(F32), jnp.where(lane == 1, i2.astype(F32), 0.0))
    route = jnp.where(lane == 2, 1.0 / (1.0 + e2), jnp.where(lane == 3, e2 / (1.0 + e2), route))
    route_ref[...] = route


def moe_router(x, norm_w, mod, router, layer, idx, seq, *, tm=512):
    n, d = x.shape
    return pl.pallas_call(
        _router_kernel, grid=(n // tm,),
        in_specs=[pl.BlockSpec((tm, d), lambda i: (i, 0)),
                  pl.BlockSpec((None, 1, d), lambda i: (layer, 0, 0)),
                  pl.BlockSpec((None, None, 6, d), lambda i: (layer, (i * tm) // seq, 0, 0)),
                  pl.BlockSpec((None, d, 128), lambda i: (idx, 0, 0))],
        out_specs=[pl.BlockSpec((tm, d), lambda i: (i, 0)), pl.BlockSpec((tm, 128), lambda i: (i, 0))],
        out_shape=[jax.ShapeDtypeStruct((n, d), F32), jax.ShapeDtypeStruct((n, 128), F32)],
        compiler_params=_params(1), name="moe_router")(x, norm_w, mod, router)


def _moe_plan(route, n_exp, tile):
    n = route.shape[0]
    ef = route[:, :2].astype(jnp.int32).reshape(-1)
    onehot = (ef[:, None] == jnp.arange(n_exp, dtype=jnp.int32)[None, :]).astype(jnp.int32)
    csum = jnp.cumsum(onehot, axis=0)
    rank = jnp.sum(onehot * (csum - 1), axis=-1)
    counts = csum[-1]
    padded = ((counts + tile - 1) // tile) * tile
    ends = jnp.cumsum(padded)
    dest = (ends - padded)[ef] + rank
    rows = -(-(2 * n + n_exp * tile) // tile) * tile
    src = jnp.zeros((rows,), jnp.int32).at[dest].set(jnp.arange(2 * n, dtype=jnp.int32) // 2)
    tile_start = jnp.arange(rows // tile, dtype=jnp.int32) * tile
    tile_valid = (tile_start < ends[-1]).astype(jnp.int32)
    tile_expert = jnp.sum((ends[None, :] <= tile_start[:, None]).astype(jnp.int32), axis=1)
    tile_expert = jnp.minimum(tile_expert, n_exp - 1)
    last = jnp.maximum(ends[-1] // tile - 1, 0)
    tile_expert = jnp.where(tile_valid > 0, tile_expert, tile_expert[last])
    return dest, src, tile_expert, tile_valid


def _row_copy(src_hbm, row, dst_vmem, slot, sem):
    return pltpu.make_async_copy(src_hbm.at[pl.ds(row, 1), :], dst_vmem.at[pl.ds(slot, 1), :], sem)


def _experts_kernel(te_ref, tv_ref, src_ref, h_hbm, w1_ref, w3_ref, w2_ref, ys_ref,
                    rows_scr, xb_scr, acc_scr, sems, *, n_tiles, nf):
    i = pl.program_id(0)
    f = pl.program_id(1)
    tm = xb_scr.shape[0]
    valid = tv_ref[i] > 0
    slot = i % 2
    per_step = -(-tm // (8 * nf)) * 8
    last_rows = tm - per_step * (nf - 1)
    assert 0 < last_rows <= per_step
    nxt = jnp.minimum(i + 1, n_tiles - 1)

    def drain(sl):
        def body(r, carry):
            _row_copy(h_hbm, 0, rows_scr.at[sl], r, sems.at[sl]).wait()
            return carry
        lax.fori_loop(0, tm, body, 0, unroll=8)

    @pl.when(f == 0)
    def _():
        @pl.when(i == 0)
        def _():
            def body(r, carry):
                _row_copy(h_hbm, src_ref[r], rows_scr.at[0], r, sems.at[0]).start()
                return carry
            lax.fori_loop(0, tm, body, 0, unroll=8)
        drain(slot)
        xb_scr[...] = rows_scr[slot].astype(BF16)
        acc_scr[...] = jnp.zeros(acc_scr.shape, F32)

    def request_next(count):
        first = f * per_step
        for r in range(count):
            _row_copy(h_hbm, src_ref[nxt * tm + first + r], rows_scr.at[1 - slot], first + r,
                      sems.at[1 - slot]).start()

    def matmuls():
        h = xb_scr[...]
        g = _silu(_dot(h, w1_ref[...])) * _dot(h, w3_ref[...])
        acc_scr[...] += _dot(g.astype(BF16), w2_ref[...])

    for final, count in ((False, per_step), (True, last_rows)):
        step = (f == nf - 1) if final else (f < nf - 1)

        @pl.when(step & valid)
        def _():
            request_next(count)
            matmuls()

        @pl.when(step & jnp.logical_not(valid))
        def _():
            request_next(count)

    @pl.when(f == nf - 1)
    def _():
        ys_ref[...] = acc_scr[...]

        @pl.when(i == n_tiles - 1)
        def _():
            drain(1 - slot)


def moe_experts(h, src, tile_expert, tile_valid, w1, w3, w2, idx, *, tf=256):
    n, d = h.shape
    rows = src.shape[0]
    dff = w1.shape[-1]
    tm = MOE_TILE
    nf = dff // tf
    fcol = lambda i, f, tv: jnp.where(tv[i] > 0, f, nf - 1)
    return pl.pallas_call(
        functools.partial(_experts_kernel, n_tiles=rows // tm, nf=nf),
        grid_spec=pltpu.PrefetchScalarGridSpec(
            num_scalar_prefetch=3, grid=(rows // tm, nf),
            in_specs=[pl.BlockSpec(memory_space=pl.ANY),
                      pl.BlockSpec((None, None, d, tf), lambda i, f, te, tv, src: (idx, te[i], 0, fcol(i, f, tv))),
                      pl.BlockSpec((None, None, d, tf), lambda i, f, te, tv, src: (idx, te[i], 0, fcol(i, f, tv))),
                      pl.BlockSpec((None, None, tf, d), lambda i, f, te, tv, src: (idx, te[i], fcol(i, f, tv), 0))],
            out_specs=pl.BlockSpec((tm, d), lambda i, f, te, tv, src: (i, 0)),
            scratch_shapes=[pltpu.VMEM((2, tm, d), F32), pltpu.VMEM((tm, d), BF16), pltpu.VMEM((tm, d), F32),
                            pltpu.SemaphoreType.DMA((2,))]),
        out_shape=jax.ShapeDtypeStruct((rows, d), F32),
        compiler_params=_params(2), name="moe_experts")(tile_expert, tile_valid, src, h, w1, w3, w2)


def _moe_combine_kernel(dest_ref, ys_hbm, x_ref, route_ref, mod_ref, *rest, out_norm):
    if out_norm:
        fw_ref, o_ref, buf0, buf1, sems = rest
    else:
        o_ref, buf0, buf1, sems = rest
    tc = x_ref.shape[0]
    i = pl.program_id(0)
    slot = i % 2

    def issue(tile, sl):
        def body(t, carry):
            a = 2 * (tile * tc + t)
            _row_copy(ys_hbm, dest_ref[a], buf0.at[sl], t, sems.at[sl]).start()
            _row_copy(ys_hbm, dest_ref[a + 1], buf1.at[sl], t, sems.at[sl]).start()
            return carry
        lax.fori_loop(0, tc, body, 0, unroll=4)

    def drain(sl):
        def body(t, carry):
            _row_copy(ys_hbm, 0, buf0.at[sl], t, sems.at[sl]).wait()
            _row_copy(ys_hbm, 0, buf1.at[sl], t, sems.at[sl]).wait()
            return carry
        lax.fori_loop(0, tc, body, 0, unroll=4)

    @pl.when(i == 0)
    def _():
        issue(0, 0)

    @pl.when(i + 1 < pl.num_programs(0))
    def _():
        issue(i + 1, 1 - slot)
    drain(slot)
    route = route_ref[...]
    y = route[:, 2:3] * buf0[slot] + route[:, 3:4] * buf1[slot]
    out = x_ref[...] + mod_ref[5:6, :] * y
    if out_norm:
        out = out * lax.rsqrt(jnp.mean(out * out, axis=-1, keepdims=True) + EPS) * fw_ref[...]
    o_ref[...] = out


def moe_combine(ys, dest, x, route, mod, layer, seq, out_norm_w=None, *, tc=256):
    n, d = x.shape
    in_specs = [pl.BlockSpec(memory_space=pl.ANY),
                pl.BlockSpec((tc, d), lambda i, dest: (i, 0)),
                pl.BlockSpec((tc, 128), lambda i, dest: (i, 0)),
                pl.BlockSpec((None, None, 6, d), lambda i, dest: (layer, (i * tc) // seq, 0, 0))]
    args = [dest, ys, x, route, mod]
    if out_norm_w is not None:
        in_specs.append(pl.BlockSpec((1, d), lambda i, dest: (0, 0)))
        args.append(out_norm_w.reshape(1, d))
    return pl.pallas_call(
        functools.partial(_moe_combine_kernel, out_norm=out_norm_w is not None),
        grid_spec=pltpu.PrefetchScalarGridSpec(
            num_scalar_prefetch=1, grid=(n // tc,), in_specs=in_specs,
            out_specs=pl.BlockSpec((tc, d), lambda i, dest: (i, 0)),
            scratch_shapes=[pltpu.VMEM((2, tc, d), F32), pltpu.VMEM((2, tc, d), F32),
                            pltpu.SemaphoreType.DMA((2,))]),
        out_shape=jax.ShapeDtypeStruct((n, d), F32),
        compiler_params=_params(1), name="moe_combine")(*args)


def moe_residual(x, norm_w, mod, router, w1, w3, w2, layer, idx, seq, out_norm_w=None):
    h, route = moe_router(x, norm_w, mod, router, layer, idx, seq)
    dest, src, tile_expert, tile_valid = _moe_plan(route, w1.shape[1], MOE_TILE)
    ys = moe_experts(h, src, tile_expert, tile_valid, w1, w3, w2, idx)
    return moe_combine(ys, dest, x, route, mod, layer, seq, out_norm_w)


def _final_norm_kernel(x_ref, w_ref, o_ref):
    x = x_ref[...]
    o_ref[...] = x * lax.rsqrt(jnp.mean(x * x, axis=-1, keepdims=True) + EPS) * w_ref[...]


def final_norm(x, w, *, tm=512):
    n, d = x.shape
    return pl.pallas_call(
        _final_norm_kernel, grid=(n // tm,),
        in_specs=[pl.BlockSpec((tm, d), lambda i: (i, 0)), pl.BlockSpec((1, d), lambda i: (0, 0))],
        out_specs=pl.BlockSpec((tm, d), lambda i: (i, 0)),
        out_shape=jax.ShapeDtypeStruct((n, d), F32),
        compiler_params=_params(1), name="final_norm")(x, w.reshape(1, d))


_SRC_A, _SRC_B, _SRC_BG, _SRC_C, _SRC_CIF, _SRC_CO = 0, 1024, 2304, 2316, 4364, 4372
IN_COLS = 5396


def _w_in_prep_kernel(w_ref, o_ref):
    rows = w_ref.shape[0]

    def put(dst, src, width):
        o_ref[:, dst:dst + width] = w_ref[:, src:src + width].astype(BF16)

    put(COL_AQ, _SRC_A, _SRC_B - _SRC_A)
    put(COL_CQK, _SRC_C, _SRC_CIF - _SRC_C)
    put(COL_CO, _SRC_CO, IN_COLS - _SRC_CO)
    put(COL_BQ, _SRC_B, _SRC_BG - _SRC_B)
    n_small = (_SRC_C - _SRC_BG) + (_SRC_CO - _SRC_CIF)
    small = jnp.concatenate([w_ref[:, _SRC_BG:_SRC_C], w_ref[:, _SRC_CIF:_SRC_CO],
                             jnp.zeros((rows, 128 - n_small), F32)], axis=1)
    o_ref[:, COL_SMALL:COL_SMALL + 128] = small.astype(BF16)
    o_ref[:, COL_SMALL + 128:] = jnp.zeros((rows, PROJ_COLS - COL_SMALL - 128), BF16)


def _reorder_w_in(w_in, *, tr=256):
    depth, d, cols = w_in.shape
    assert cols == IN_COLS and d % tr == 0
    return pl.pallas_call(
        _w_in_prep_kernel, grid=(depth, d // tr),
        in_specs=[pl.BlockSpec((None, tr, cols), lambda l, i: (l, i, 0))],
        out_specs=pl.BlockSpec((None, tr, PROJ_COLS), lambda l, i: (l, i, 0)),
        out_shape=jax.ShapeDtypeStruct((depth, d, PROJ_COLS), BF16),
        compiler_params=_params(2), name="w_in_prep")(w_in)


def hybrid_mixer(proj, tables, layer, a_sinks, mlstm_gate_b, nsa_pe_k, nsa_pe_v, nsa_ck_w1, nsa_ck_w2,
                 nsa_cv_w1, nsa_cv_w2, mlstm_conv_w, mlstm_norm_w):
    bsz, t, _ = proj.shape
    o_a = sliding_window_attention(proj, col_q=COL_AQ, col_k=COL_AK, col_v=COL_AV, n_heads=A_HEADS,
                                   n_kv=A_KV_HEADS, window=A_WINDOW, tq=256, tables=tables,
                                   sinks=a_sinks[layer])
    chunks = lambda col: proj[:, :, col:col + HEAD_DIM].reshape(bsz, t // CMP_STRIDE, CMP_STRIDE * HEAD_DIM)
    k_cmp = compress_blocks(chunks(COL_BKC), nsa_pe_k, nsa_ck_w1, nsa_ck_w2, layer)
    v_cmp = compress_blocks(chunks(COL_BVC), nsa_pe_v, nsa_cv_w1, nsa_cv_w2, layer)
    o_cmp, bias_t = compressed_attention_select(proj, k_cmp, v_cmp)
    o_sel = selected_attention(proj, bias_t)
    o_b = sliding_window_attention(proj, col_q=COL_BQ, col_k=COL_BKW, col_v=COL_BVW, n_heads=B_HEADS,
                                   n_kv=1, window=B_WINDOW, tq=512, gate_with=(o_cmp, o_sel))
    o_c = mlstm_mixer(proj, mlstm_gate_b, mlstm_conv_w, mlstm_norm_w, layer)
    return o_a, o_b, o_c


def kernel(x, c, positions, ada_w, ada_b, norm_mix_w, norm_ffn_w, w_in, mlstm_gate_b, a_sinks, nsa_pe_k,
           nsa_pe_v, nsa_ck_w1, nsa_ck_w2, nsa_cv_w1, nsa_cv_w2, mlstm_conv_w, mlstm_norm_w, w_out, ffn_w1,
           ffn_w3, ffn_w2, moe_router, moe_w1, moe_w3, moe_w2, final_norm_w):
    bsz, t, d = x.shape
    depth = ada_w.shape[0]
    mod = ada_modulation(c, ada_w, ada_b)
    tables = rope_tables(positions)
    w_in_p = _reorder_w_in(w_in)
    w_out_b = w_out.astype(BF16)
    ffn_b = [w.astype(BF16) for w in (ffn_w1, ffn_w3, ffn_w2)]
    moe_b = [w.astype(BF16) for w in (moe_w1, moe_w3, moe_w2)]
    router_p = jnp.pad(moe_router, ((0, 0), (0, 0), (0, 128 - N_EXPERTS)))
    norm_mix = norm_mix_w.reshape(depth, 1, d)
    norm_ffn = norm_ffn_w.reshape(depth, 1, d)
    conv_w = mlstm_conv_w
    norm_c = mlstm_norm_w.reshape(depth, 1, -1)

    xf = x.reshape(bsz * t, d)
    for layer in range(depth):
        proj = norm_proj(xf, norm_mix, mod, w_in_p, layer, t, shift_row=0, scale_row=1)
        o_a, o_b, o_c = hybrid_mixer(proj.reshape(bsz, t, PROJ_COLS), tables, layer, a_sinks, mlstm_gate_b,
                                     nsa_pe_k, nsa_pe_v, nsa_ck_w1, nsa_ck_w2, nsa_cv_w1, nsa_cv_w2,
                                     conv_w, norm_c)
        flat = lambda a: a.reshape(bsz * t, a.shape[-1])
        xf = out_proj_residual(flat(o_a), flat(o_b), flat(o_c), w_out_b, xf, mod, layer, t)
        if layer % 2 == 0:
            xf = ffn_residual(xf, norm_ffn, mod, *ffn_b, layer, layer // 2, t)
        else:
            fused_norm = final_norm_w if layer == depth - 1 else None
            xf = moe_residual(xf, norm_ffn, mod, router_p, *moe_b, layer, layer // 2, t, fused_norm)
    if depth % 2 == 1:
        xf = final_norm(xf, final_norm_w)
    return xf.reshape(bsz, t, d)
```

```python
import functools
import math

import numpy as np
import jax
import jax.numpy as jnp
from jax import lax
from jax.experimental import pallas as pl
from jax.experimental.pallas import tpu as pltpu

F32 = jnp.float32
BF16 = jnp.bfloat16

HEAD_DIM = 128
A_HEADS, A_KV_HEADS, A_WINDOW = 4, 2, 128
ROPE_THETA = 150000.0
B_HEADS = 4
CMP_LEN, CMP_STRIDE = 32, 16
SEL_LEN, SEL_TOPK, B_WINDOW = 64, 16, 512
C_HEADS, C_QK_DIM, C_V_DIM, C_CONV = 4, 128, 256, 4
N_EXPERTS = 8
EPS = 1e-6
BIG = 1e9
NEG = -1e30
SCALE = HEAD_DIM ** -0.5

COL_AQ, COL_AK, COL_AV = 0, 512, 768
COL_CQK, COL_CV, COL_CO = 1024, 2048, 3072
COL_BQ = 4096
COL_BKC, COL_BVC, COL_BKS, COL_BVS, COL_BKW, COL_BVW = 4608, 4736, 4864, 4992, 5120, 5248
COL_SMALL = 5376
LANE_CI, LANE_CF = 12, 16
PROJ_COLS = 5632

VMEM_LIMIT_MB = 56
BF16_SUBLANES = 16


def _params(n_axes, vmem_mb=VMEM_LIMIT_MB):
    return pltpu.CompilerParams(dimension_semantics=("arbitrary",) * n_axes,
                                vmem_limit_bytes=vmem_mb * 2 ** 20)


def _dot(a, b):
    return jnp.dot(a, b, preferred_element_type=F32)


def _dot_nt(a, b):
    return lax.dot_general(a, b, (((1,), (1,)), ((), ())), preferred_element_type=F32)


def _silu(v):
    return v * jax.nn.sigmoid(v)


def _norm_mod(x, nw, shift, scale):
    ms = jnp.mean(x * x, axis=-1, keepdims=True)
    return (x * lax.rsqrt(ms + EPS) * nw) * (1.0 + scale) + shift


def _ada_kernel(c_ref, w_ref, b_ref, o_ref):
    act = _silu(c_ref[...]).astype(BF16)
    o_ref[...] = _dot(act, w_ref[...].astype(BF16)) + b_ref[...]


def ada_modulation(c, ada_w, ada_b):
    depth, d, n6 = ada_w.shape
    bsz = c.shape[0]
    assert bsz <= 8
    cp = jnp.zeros((8, d), F32).at[:bsz].set(c)
    tn = 512
    out = pl.pallas_call(
        _ada_kernel, grid=(depth, n6 // tn),
        in_specs=[pl.BlockSpec((8, d), lambda l, j: (0, 0)),
                  pl.BlockSpec((None, d, tn), lambda l, j: (l, 0, j)),
                  pl.BlockSpec((None, 1, tn), lambda l, j: (l, 0, j))],
        out_specs=pl.BlockSpec((None, 8, tn), lambda l, j: (l, 0, j)),
        out_shape=jax.ShapeDtypeStruct((depth, 8, n6), F32),
        compiler_params=_params(2), name="ada_modulation")(cp, ada_w, ada_b.reshape(depth, 1, n6))
    return out.reshape(depth, 8, 6, d)


def _rope_table_kernel(pos_ref, cos_ref, sin_ref):
    pos = pos_ref[...].astype(F32)
    lane = lax.broadcasted_iota(jnp.int32, (1, HEAD_DIM), 1)
    half = HEAD_DIM // 2
    inv = jnp.exp((lane & (half - 1)).astype(F32) * (-math.log(ROPE_THETA) * 2.0 / HEAD_DIM))
    ang = pos * inv
    cos_ref[...] = jnp.cos(ang)
    sin_ref[...] = jnp.where(lane < half, -1.0, 1.0) * jnp.sin(ang)


def rope_tables(positions):
    bsz, t = positions.shape
    tt = 512
    spec = pl.BlockSpec((None, tt, HEAD_DIM), lambda b, i: (b, i, 0))
    return pl.pallas_call(
        _rope_table_kernel, grid=(bsz, t // tt),
        in_specs=[pl.BlockSpec((None, tt, 1), lambda b, i: (b, i, 0))],
        out_specs=[spec, spec],
        out_shape=[jax.ShapeDtypeStruct((bsz, t, HEAD_DIM), F32)] * 2,
        compiler_params=_params(2), name="rope_tables")(positions.reshape(bsz, t, 1))


def _proj_kernel(x_ref, nw_ref, mod_ref, w_ref, o_ref, h_scr, *, shift_row, scale_row):
    @pl.when(pl.program_id(1) == 0)
    def _():
        h = _norm_mod(x_ref[...], nw_ref[...], mod_ref[shift_row:shift_row + 1, :],
                      mod_ref[scale_row:scale_row + 1, :])
        h_scr[...] = h.astype(BF16)

    o_ref[...] = _dot(h_scr[...], w_ref[...])


def norm_proj(x, norm_w, mod, w, layer, seq, *, shift_row, scale_row, tm=1024, tn=1408):
    n, d = x.shape
    p = w.shape[-1]
    assert seq % tm == 0 and p % tn == 0
    return pl.pallas_call(
        functools.partial(_proj_kernel, shift_row=shift_row, scale_row=scale_row),
        grid=(n // tm, p // tn),
        in_specs=[pl.BlockSpec((tm, d), lambda i, j: (i, 0)),
                  pl.BlockSpec((None, 1, d), lambda i, j: (layer, 0, 0)),
                  pl.BlockSpec((None, None, 6, d), lambda i, j: (layer, (i * tm) // seq, 0, 0)),
                  pl.BlockSpec((None, d, tn), lambda i, j: (layer, 0, j))],
        out_specs=pl.BlockSpec((tm, tn), lambda i, j: (i, j)),
        out_shape=jax.ShapeDtypeStruct((n, p), F32),
        scratch_shapes=[pltpu.VMEM((tm, d), BF16)],
        compiler_params=_params(2), name="norm_proj")(x, norm_w, mod, w)


def _swa_kernel(*refs, tq, sub, window, n_heads, group, rope, sinks, gated):
    it = iter(refs)
    q_ref, kp_ref, kc_ref, vp_ref, vc_ref = (next(it) for _ in range(5))
    if rope:
        cp_ref, sp_ref, cc_ref, sc_ref = (next(it) for _ in range(4))
    if sinks:
        sink_ref = next(it)
    if gated:
        g_ref, cmp_ref, sel_ref = (next(it) for _ in range(3))
        gates = jax.nn.sigmoid(g_ref[...])
    o_ref = next(it)
    i = pl.program_id(1)
    pad = -(-(window - 1) // sub) * sub
    span = pad + sub
    row = lax.broadcasted_iota(jnp.int32, (sub, span), 0)
    col = lax.broadcasted_iota(jnp.int32, (sub, span), 1)
    diff = row + pad - col

    def window_mask(n_prev):
        d = diff if n_prev == 0 else diff + jnp.where(col < n_prev, jnp.where(i > 0, 0, window), 0)
        return jnp.abs(2 * d - (window - 1)) <= (window - 1)
    masks = {}

    def rotate(v, c, s):
        return v * c + pltpu.roll(v, HEAD_DIM // 2, 1) * s

    kv_cache = {}
    for hq in range(n_heads):
        kv = hq // group
        if kv not in kv_cache:
            sl = slice(kv * HEAD_DIM, (kv + 1) * HEAD_DIM)
            kp, kc = kp_ref[:, sl], kc_ref[:, sl]
            if rope:
                kp = rotate(kp, cp_ref[...], sp_ref[...])
                kc = rotate(kc, cc_ref[...], sc_ref[...])
            k = jnp.concatenate([kp, kc], axis=0).astype(BF16)
            v = jnp.concatenate([vp_ref[:, sl], vc_ref[:, sl]], axis=0).astype(BF16)
            kv_cache[kv] = (k, v)
        k, v = kv_cache[kv]
        hs = slice(hq * HEAD_DIM, (hq + 1) * HEAD_DIM)
        q = q_ref[:, hs]
        if rope:
            q = rotate(q, cc_ref[...], sc_ref[...])
        q = (q * SCALE).astype(BF16)
        for j in range(tq // sub):
            rows = slice(j * sub, (j + 1) * sub)
            lo = tq + j * sub - pad
            n_prev = max(tq - lo, 0)
            if n_prev not in masks:
                masks[n_prev] = window_mask(n_prev)
            s = _dot_nt(q[rows], k[lo:lo + span])
            s = jnp.where(masks[n_prev], s, NEG)
            m = jnp.max(s, axis=-1, keepdims=True)
            if sinks:
                m = jnp.maximum(m, sink_ref[hq])
            m = jnp.where(m < 0.5 * NEG, 0.0, m)
            p = jnp.exp(s - m)
            den = jnp.sum(p, axis=-1, keepdims=True)
            if sinks:
                den = den + jnp.exp(sink_ref[hq] - m)
            o = _dot(p.astype(BF16), v[lo:lo + span]) / jnp.maximum(den, 1e-30)
            if gated:
                g = gates[rows]
                o = (g[:, 3 * hq:3 * hq + 1] * cmp_ref[rows, hs] + g[:, 3 * hq + 1:3 * hq + 2] * sel_ref[rows, hs]
                     + g[:, 3 * hq + 2:3 * hq + 3] * o)
            o_ref[rows, hs] = o.astype(o_ref.dtype)


def sliding_window_attention(proj, *, col_q, col_k, col_v, n_heads, n_kv, window, tq, sub,
                             tables=None, sinks=None, gate_with=None):
    bsz, t, _ = proj.shape
    assert window - 1 <= tq and t % tq == 0 and tq % sub == 0 and sub % 128 == 0
    wq, wkv = n_heads * HEAD_DIM, n_kv * HEAD_DIM
    assert col_q % wq == 0 and col_k % wkv == 0 and col_v % wkv == 0
    cur = lambda b, i: (b, i, 0)
    prev = lambda b, i: (b, jnp.maximum(i - 1, 0), 0)
    kvspec = lambda col, im: pl.BlockSpec(
        (None, tq, wkv), lambda b, i: (b, im(b, i)[1], col // wkv))
    in_specs = [pl.BlockSpec((None, tq, wq), lambda b, i: (b, i, col_q // wq)),
                kvspec(col_k, prev), kvspec(col_k, cur), kvspec(col_v, prev), kvspec(col_v, cur)]
    args = [proj] * 5
    if tables is not None:
        cos, sin = tables
        tp = pl.BlockSpec((None, tq, HEAD_DIM), prev)
        tc = pl.BlockSpec((None, tq, HEAD_DIM), cur)
        in_specs += [tp, tp, tc, tc]
        args += [cos, sin, cos, sin]
    if sinks is not None:
        in_specs.append(pl.BlockSpec(memory_space=pltpu.SMEM))
        args.append(sinks)
    if gate_with is not None:
        branch = pl.BlockSpec((None, tq, wq), cur)
        in_specs += [pl.BlockSpec((None, tq, 128), lambda b, i: (b, i, COL_SMALL // 128)), branch, branch]
        args += [proj, *gate_with]
    return pl.pallas_call(
        functools.partial(_swa_kernel, tq=tq, sub=sub, window=window, n_heads=n_heads, group=n_heads // n_kv,
                          rope=tables is not None, sinks=sinks is not None, gated=gate_with is not None),
        grid=(bsz, t // tq), in_specs=in_specs,
        out_specs=pl.BlockSpec((None, tq, wq), cur),
        out_shape=jax.ShapeDtypeStruct((bsz, t, wq), BF16),
        compiler_params=_params(2), name="sliding_window_attention")(*args)


def _compress_kernel(r_ref, pe_ref, w1_ref, w2_ref, o_ref):
    n16, half = r_ref.shape
    r = r_ref[...].astype(BF16)
    w1 = w1_ref[...].astype(BF16)
    first = _dot(r, w1[:half])
    second = _dot(r, w1[half:])
    pe = _dot(pe_ref[...].astype(BF16), w1)[0:1]
    pre = first + pltpu.roll(second, n16 - 1, 0) + pe
    hid = jax.nn.gelu(pre, approximate=True)
    o_ref[...] = _dot(hid.astype(BF16), w2_ref[...].astype(BF16))


def compress_blocks(chunks, pe, w1, w2, layer):
    bsz, n16, half = chunks.shape
    hid = w1.shape[-1]
    pe8 = jnp.broadcast_to(pe[layer].reshape(1, 2 * half), (8, 2 * half))
    return pl.pallas_call(
        _compress_kernel, grid=(bsz,),
        in_specs=[pl.BlockSpec((None, n16, half), lambda b: (b, 0, 0)),
                  pl.BlockSpec((8, 2 * half), lambda b: (0, 0)),
                  pl.BlockSpec((None, 2 * half, hid), lambda b: (layer, 0, 0)),
                  pl.BlockSpec((None, hid, HEAD_DIM), lambda b: (layer, 0, 0))],
        out_specs=pl.BlockSpec((None, n16, HEAD_DIM), lambda b: (b, 0, 0)),
        out_shape=jax.ShapeDtypeStruct((bsz, n16, HEAD_DIM), F32),
        compiler_params=_params(1), name="compress_blocks")(chunks, pe8, w1, w2)


def _heads_transposed(q, tq, scale=SCALE):
    n = q.shape[1] // HEAD_DIM
    return jnp.concatenate([(q[:, g * HEAD_DIM:(g + 1) * HEAD_DIM] * scale).T for g in range(n)],
                           axis=1).astype(BF16)


def _cmp_select_kernel(q_ref, kc_ref, vc_ref, ov_ref, ocmp_ref, bias_ref, *, tq, n_sel, topk):
    i = pl.program_id(1)
    nc = kc_ref.shape[0]
    hq = B_HEADS * tq
    qt = _heads_transposed(q_ref[...], tq)
    st = _dot(kc_ref[...].astype(BF16), qt)
    n_row = lax.broadcasted_iota(jnp.int32, (nc, hq), 0)
    t_col = i * tq + (lax.broadcasted_iota(jnp.int32, (nc, hq), 1) & (tq - 1))
    st = jnp.where(n_row * CMP_STRIDE + (CMP_LEN - 1) <= t_col, st, NEG)
    m = jnp.max(st, axis=0, keepdims=True)
    m = jnp.where(m < 0.5 * NEG, 0.0, m)
    p = jnp.exp(st - m)
    p = p / jnp.maximum(jnp.sum(p, axis=0, keepdims=True), 1e-30)
    pb = p.astype(BF16)
    ot = _dot(vc_ref[...].T.astype(BF16), pb)
    for g in range(B_HEADS):
        ocmp_ref[:, g * HEAD_DIM:(g + 1) * HEAD_DIM] = ot[:, g * tq:(g + 1) * tq].T
    imp4 = _dot(ov_ref[...], pb)
    imp = imp4[:, 0:tq]
    for g in range(1, B_HEADS):
        imp = imp + imp4[:, g * tq:(g + 1) * tq]
    nsp = imp.shape[0]
    j = lax.broadcasted_iota(jnp.int32, (nsp, tq), 0)
    t = i * tq + lax.broadcasted_iota(jnp.int32, (nsp, tq), 1)
    cur = t >> 6
    imp = jnp.where((j << 6) <= t, imp, -BIG)
    for forced in (0, cur, cur - 1):
        imp = jnp.where(j == forced, BIG, imp)
    imp = jnp.where(j < n_sel, imp, -3e38)
    sel = jnp.zeros((nsp, tq), F32)
    for _ in range(topk):
        mx = jnp.max(imp, axis=0, keepdims=True)
        idx = jnp.min(jnp.where(imp == mx, j, nsp), axis=0, keepdims=True)
        hit = j == idx
        sel = jnp.where(hit, 1.0, sel)
        imp = jnp.where(hit, -3e38, imp)
    bias_ref[...] = jnp.where(sel > 0.0, 0.0, jnp.where(j < n_sel, -BIG, 0.0))


def _overlap_matrix_t(t):
    n16, n_sel = t // CMP_STRIDE, t // SEL_LEN
    nsp = -(-n_sel // 128) * 128
    starts = np.arange(n16) * CMP_STRIDE
    sel_start = np.arange(n_sel) * SEL_LEN
    ov = np.clip(np.minimum(starts[:, None] + CMP_LEN, sel_start[None, :] + SEL_LEN)
                 - np.maximum(starts[:, None], sel_start[None, :]), 0, None).astype(np.float32) / CMP_LEN
    out = np.zeros((nsp, n16), np.float32)
    out[:n_sel] = ov.T
    return jnp.asarray(out, BF16)


def compressed_attention_select(proj, k_cmp, v_cmp, *, tq=256):
    bsz, t, _ = proj.shape
    n16, n_sel = t // CMP_STRIDE, t // SEL_LEN
    assert SEL_LEN == 64 and tq % 128 == 0 and tq & (tq - 1) == 0 and t % tq == 0
    ov = _overlap_matrix_t(t)
    nsp = ov.shape[0]
    wq = B_HEADS * HEAD_DIM
    return pl.pallas_call(
        functools.partial(_cmp_select_kernel, tq=tq, n_sel=n_sel, topk=min(SEL_TOPK, n_sel)),
        grid=(bsz, t // tq),
        in_specs=[pl.BlockSpec((None, tq, wq), lambda b, i: (b, i, COL_BQ // wq)),
                  pl.BlockSpec((None, n16, HEAD_DIM), lambda b, i: (b, 0, 0)),
                  pl.BlockSpec((None, n16, HEAD_DIM), lambda b, i: (b, 0, 0)),
                  pl.BlockSpec((nsp, n16), lambda b, i: (0, 0))],
        out_specs=[pl.BlockSpec((None, tq, wq), lambda b, i: (b, i, 0)),
                   pl.BlockSpec((None, nsp, tq), lambda b, i: (b, 0, i))],
        out_shape=[jax.ShapeDtypeStruct((bsz, t, wq), F32),
                   jax.ShapeDtypeStruct((bsz, nsp, t), F32)],
        compiler_params=_params(2), name="compressed_attention_select")(proj, k_cmp, v_cmp, ov)


def _selected_kernel(q_ref, bias_ref, ks_ref, vs_ref, o_ref, kaug_scr, vt_scr, qa_scr, s0_scr, s1_scr,
                     m_scr, acc_scr, *, tq, kb):
    i = pl.program_id(1)
    t_all = ks_ref.shape[0]
    nsp = bias_ref.shape[0]
    hq = B_HEADS * tq
    ones_rows = vt_scr.shape[1] - HEAD_DIM

    @pl.when(i == 0)
    def _():
        def prep(c, carry):
            r0 = pl.multiple_of(c * kb, kb)
            kaug_scr[pl.ds(r0, kb), 0:HEAD_DIM] = ks_ref[pl.ds(r0, kb), :].astype(BF16)
            key = r0 + lax.broadcasted_iota(jnp.int32, (kb, nsp), 0)
            blk = lax.broadcasted_iota(jnp.int32, (kb, nsp), 1)
            kaug_scr[pl.ds(r0, kb), HEAD_DIM:] = jnp.where((key >> 6) == blk, 1.0, 0.0).astype(BF16)
            vt_scr[c] = jnp.concatenate([vs_ref[pl.ds(r0, kb), :].T, jnp.ones((ones_rows, kb), F32)],
                                        axis=0).astype(BF16)
            return carry
        lax.fori_loop(0, t_all // kb, prep, 0)

    bias = bias_ref[...]
    qa_scr[...] = jnp.concatenate([_heads_transposed(q_ref[...], tq, SCALE * math.log2(math.e)),
                                   jnp.concatenate([bias] * B_HEADS, axis=1).astype(BF16)], axis=0)
    m_scr[...] = jnp.full(m_scr.shape, NEG, F32)
    acc_scr[...] = jnp.zeros(acc_scr.shape, F32)

    def scores(kt, s_out):
        r0 = pl.multiple_of(kt * kb, kb)
        s_out[...] = _dot(kaug_scr[pl.ds(r0, kb), :], qa_scr[...])

    def consume(kt, s_in, causal):
        st = s_in[...]
        if causal:
            key = kt * kb + lax.broadcasted_iota(jnp.int32, (kb, hq), 0)
            tpos = i * tq + (lax.broadcasted_iota(jnp.int32, (kb, hq), 1) & (tq - 1))
            st = jnp.where(key <= tpos, st, NEG)
        m_old = m_scr[...]
        m_new = jnp.maximum(m_old, jnp.max(st, axis=0, keepdims=True))
        p = jnp.exp2((st - m_new).astype(BF16))
        acc_scr[...] = jnp.exp2(m_old - m_new) * acc_scr[...] + _dot(vt_scr[kt], p)
        m_scr[...] = m_new

    def step(kt, s_in, s_out):
        scores(kt + 1, s_out)
        consume(kt, s_in, False)

    def finish(kt, s_in):
        consume(kt, s_in, True)
        ot = acc_scr[0:HEAD_DIM, :] / jnp.maximum(acc_scr[HEAD_DIM:HEAD_DIM + 1, :], 1e-30)
        for g in range(B_HEADS):
            o_ref[:, g * HEAD_DIM:(g + 1) * HEAD_DIM] = ot[:, g * tq:(g + 1) * tq].T

    n_full = (i * tq) // kb
    scores(0, s0_scr)

    def pair(j, carry):
        step(2 * j, s0_scr, s1_scr)
        step(2 * j + 1, s1_scr, s0_scr)
        return carry
    lax.fori_loop(0, n_full // 2, pair, 0)

    @pl.when(n_full % 2 == 1)
    def _():
        step(n_full - 1, s0_scr, s1_scr)
        finish(n_full, s1_scr)

    @pl.when(n_full % 2 == 0)
    def _():
        finish(n_full, s0_scr)


def selected_attention(proj, bias_t, *, tq=256, kb=512):
    bsz, t, _ = proj.shape
    nsp = bias_t.shape[1]
    wq = B_HEADS * HEAD_DIM
    assert kb % tq == 0 and t % kb == 0
    return pl.pallas_call(
        functools.partial(_selected_kernel, tq=tq, kb=kb),
        grid=(bsz, t // tq),
        in_specs=[pl.BlockSpec((None, tq, wq), lambda b, i: (b, i, COL_BQ // wq)),
                  pl.BlockSpec((None, nsp, tq), lambda b, i: (b, 0, i)),
                  pl.BlockSpec((None, t, HEAD_DIM), lambda b, i: (b, 0, COL_BKS // HEAD_DIM)),
                  pl.BlockSpec((None, t, HEAD_DIM), lambda b, i: (b, 0, COL_BVS // HEAD_DIM))],
        out_specs=pl.BlockSpec((None, tq, wq), lambda b, i: (b, i, 0)),
        out_shape=jax.ShapeDtypeStruct((bsz, t, wq), F32),
        scratch_shapes=[pltpu.VMEM((t, HEAD_DIM + nsp), BF16),
                        pltpu.VMEM((t // kb, HEAD_DIM + BF16_SUBLANES, kb), BF16),
                        pltpu.VMEM((HEAD_DIM + nsp, B_HEADS * tq), BF16),
                        pltpu.VMEM((kb, B_HEADS * tq), F32),
                        pltpu.VMEM((kb, B_HEADS * tq), F32),
                        pltpu.VMEM((1, B_HEADS * tq), F32),
                        pltpu.VMEM((HEAD_DIM + BF16_SUBLANES, B_HEADS * tq), F32)],
        compiler_params=_params(2), name="selected_attention")(proj, bias_t, proj, proj)


def _mlstm_kernel(gb_ref, qk_ref, v_ref, og_ref, small_ref, convw_ref, normw_ref, o_ref,
                  hist_scr, c_scr, n_scr, m_scr, *, chunk):
    L = chunk
    dqk, dv = C_QK_DIM, C_V_DIM

    @pl.when(pl.program_id(1) == 0)
    def _():
        hist_scr[...] = jnp.zeros(hist_scr.shape, F32)
        c_scr[...] = jnp.zeros(c_scr.shape, F32)
        n_scr[...] = jnp.zeros(n_scr.shape, F32)
        m_scr[...] = jnp.zeros(m_scr.shape, F32)

    x = qk_ref[...]
    xe = jnp.concatenate([hist_scr[...], x], axis=0)
    w = convw_ref[...]
    y = w[0:1] * xe[8 - 3:8 - 3 + L]
    for tap in range(1, C_CONV):
        y = y + w[tap:tap + 1] * xe[8 - 3 + tap:8 - 3 + tap + L]
    hist_scr[...] = x[L - 8:L]
    qk = _silu(y)

    lane = lax.broadcasted_iota(jnp.int32, (1, 128), 1)
    gbias = jnp.zeros((1, 128), F32)
    for idx in range(2 * C_HEADS):
        gbias = jnp.where(lane == LANE_CI + idx, gb_ref[idx], gbias)
    pre = small_ref[...] + gbias
    logf = jnp.minimum(pre, 0.0) - jnp.log1p(jnp.exp(-jnp.abs(pre)))
    row = lax.broadcasted_iota(jnp.int32, (L, L), 0)
    col = lax.broadcasted_iota(jnp.int32, (L, L), 1)
    causal = row >= col
    tri = jnp.where(causal, 1.0, 0.0)
    hp = lax.Precision.HIGHEST
    b_cols = jnp.dot(tri, logf, precision=hp, preferred_element_type=F32)
    b_rows = lax.dot_general(logf.T, tri, (((1,), (1,)), ((), ())), precision=hp,
                             preferred_element_type=F32)
    pre_t = pre.T

    for h in range(C_HEADS):
        q = qk[:, h * dqk:(h + 1) * dqk]
        k = qk[:, C_HEADS * dqk + h * dqk:C_HEADS * dqk + (h + 1) * dqk] * (dqk ** -0.5)
        v = v_ref[:, h * dv:(h + 1) * dv].astype(BF16)
        b_col = b_cols[:, LANE_CF + h:LANE_CF + h + 1]
        i_col = pre[:, LANE_CI + h:LANE_CI + h + 1]
        b_row = b_rows[LANE_CF + h:LANE_CF + h + 1, :]
        i_row = pre_t[LANE_CI + h:LANE_CI + h + 1, :]
        m_prev = m_scr[h:h + 1, 0:1]
        ct = c_scr[h]
        n_row = n_scr[h:h + 1, :]

        d = jnp.where(causal, b_col - b_row + i_row, NEG)
        m_inter = b_col + m_prev
        m_t = jnp.maximum(m_inter, jnp.max(d, axis=-1, keepdims=True))
        qb = q.astype(BF16)
        s = _dot_nt(qb, k.astype(BF16)) * jnp.exp(d - m_t)
        inter = jnp.exp(m_inter - m_t)
        num = _dot(s.astype(BF16), v) + inter * _dot(qb, ct.astype(BF16))
        den = jnp.sum(s, axis=-1, keepdims=True) + inter * jnp.sum(q * n_row, axis=-1, keepdims=True)
        hh = num / jnp.maximum(jnp.abs(den), jnp.exp(-m_t))

        b_last = b_col[L - 1:L, :]
        g = b_last - b_col + i_col
        m_new = jnp.maximum(b_last + m_prev, jnp.max(g, axis=0, keepdims=True))
        kd = k * jnp.exp(g - m_new)
        keep = jnp.exp(b_last + m_prev - m_new)
        c_scr[h] = keep * ct + _dot(kd.T.astype(BF16), v)
        n_scr[h:h + 1, :] = keep * n_row + jnp.sum(kd, axis=0, keepdims=True)
        m_scr[h:h + 1, :] = jnp.broadcast_to(m_new, (1, 128))

        vs = slice(h * dv, (h + 1) * dv)
        hn = hh * lax.rsqrt(jnp.mean(hh * hh, axis=-1, keepdims=True) + EPS) * normw_ref[:, vs]
        o_ref[:, vs] = (hn * jax.nn.sigmoid(og_ref[:, vs])).astype(o_ref.dtype)


def mlstm_mixer(proj, gate_b, conv_w, norm_w, layer, *, chunk=128):
    bsz, t, _ = proj.shape
    wqk, wv = 2 * C_HEADS * C_QK_DIM, C_HEADS * C_V_DIM
    assert wqk == wv == 1024 and t % chunk == 0
    blk = lambda col: pl.BlockSpec((None, chunk, 1024), lambda b, i: (b, i, col // 1024))
    return pl.pallas_call(
        functools.partial(_mlstm_kernel, chunk=chunk),
        grid=(bsz, t // chunk),
        in_specs=[pl.BlockSpec(memory_space=pltpu.SMEM),
                  blk(COL_CQK), blk(COL_CV), blk(COL_CO),
                  pl.BlockSpec((None, chunk, 128), lambda b, i: (b, i, COL_SMALL // 128)),
                  pl.BlockSpec((None, C_CONV, wqk), lambda b, i: (layer, 0, 0)),
                  pl.BlockSpec((None, 1, wv), lambda b, i: (layer, 0, 0))],
        out_specs=pl.BlockSpec((None, chunk, wv), lambda b, i: (b, i, 0)),
        out_shape=jax.ShapeDtypeStruct((bsz, t, wv), BF16),
        scratch_shapes=[pltpu.VMEM((8, wqk), F32),
                        pltpu.VMEM((C_HEADS, C_QK_DIM, C_V_DIM), F32),
                        pltpu.VMEM((8, C_QK_DIM), F32),
                        pltpu.VMEM((8, 128), F32)],
        compiler_params=_params(2), name="mlstm_mixer")(
            gate_b[layer], proj, proj, proj, proj, conv_w, norm_w)


def _outproj_kernel(oa_ref, ob_ref, oc_ref, w_ref, x_ref, mod_ref, o_ref, *, gate_row):
    wa, wb = oa_ref.shape[1], ob_ref.shape[1]
    acc = _dot(oa_ref[...], w_ref[0:wa, :])
    acc = acc + _dot(ob_ref[...], w_ref[wa:wa + wb, :])
    acc = acc + _dot(oc_ref[...], w_ref[wa + wb:, :])
    o_ref[...] = x_ref[...] + mod_ref[gate_row:gate_row + 1, :] * acc


def out_proj_residual(o_a, o_b, o_c, w_out, x, mod, layer, seq, *, tm=1024, tn=1024):
    n, d = x.shape
    assert seq % tm == 0 and d % tn == 0
    wa, wb, wc = o_a.shape[1], o_b.shape[1], o_c.shape[1]
    return pl.pallas_call(
        functools.partial(_outproj_kernel, gate_row=2),
        grid=(n // tm, d // tn),
        in_specs=[pl.BlockSpec((tm, wa), lambda i, j: (i, 0)),
                  pl.BlockSpec((tm, wb), lambda i, j: (i, 0)),
                  pl.BlockSpec((tm, wc), lambda i, j: (i, 0)),
                  pl.BlockSpec((None, wa + wb + wc, tn), lambda i, j: (layer, 0, j)),
                  pl.BlockSpec((tm, tn), lambda i, j: (i, j)),
                  pl.BlockSpec((None, None, 6, tn), lambda i, j: (layer, (i * tm) // seq, 0, j))],
        out_specs=pl.BlockSpec((tm, tn), lambda i, j: (i, j)),
        out_shape=jax.ShapeDtypeStruct((n, d), F32),
        compiler_params=_params(2), name="out_proj_residual")(o_a, o_b, o_c, w_out, x, mod)


def _ffn_kernel(x_ref, nw_ref, mod_ref, w1_ref, w3_ref, w2_ref, o_ref, h_scr, acc_scr):
    f = pl.program_id(1)

    @pl.when(f == 0)
    def _():
        h = _norm_mod(x_ref[...], nw_ref[...], mod_ref[3:4, :], mod_ref[4:5, :])
        h_scr[...] = h.astype(BF16)
        acc_scr[...] = jnp.zeros(acc_scr.shape, F32)

    h = h_scr[...]
    g = _silu(_dot(h, w1_ref[...])) * _dot(h, w3_ref[...])
    acc_scr[...] += _dot(g.astype(BF16), w2_ref[...])

    @pl.when(f == pl.num_programs(1) - 1)
    def _():
        o_ref[...] = x_ref[...] + mod_ref[5:6, :] * acc_scr[...]


def ffn_residual(x, norm_w, mod, w1, w3, w2, layer, idx, seq, *, tm=512, tf=512):
    n, d = x.shape
    dff = w1.shape[-1]
    return pl.pallas_call(
        _ffn_kernel, grid=(n // tm, dff // tf),
        in_specs=[pl.BlockSpec((tm, d), lambda i, f: (i, 0)),
                  pl.BlockSpec((None, 1, d), lambda i, f: (layer, 0, 0)),
                  pl.BlockSpec((None, None, 6, d), lambda i, f: (layer, (i * tm) // seq, 0, 0)),
                  pl.BlockSpec((None, d, tf), lambda i, f: (idx, 0, f)),
                  pl.BlockSpec((None, d, tf), lambda i, f: (idx, 0, f)),
                  pl.BlockSpec((None, tf, d), lambda i, f: (idx, f, 0))],
        out_specs=pl.BlockSpec((tm, d), lambda i, f: (i, 0)),
        out_shape=jax.ShapeDtypeStruct((n, d), F32),
        scratch_shapes=[pltpu.VMEM((tm, d), BF16), pltpu.VMEM((tm, d), F32)],
        compiler_params=_params(2), name="ffn_residual")(x, norm_w, mod, w1, w3, w2)


MOE_TILE = 512


def _router_kernel(x_ref, nw_ref, mod_ref, r_ref, h_ref, route_ref):
    tm = x_ref.shape[0]
    lane = lax.broadcasted_iota(jnp.int32, (tm, 128), 1)
    h = _norm_mod(x_ref[...], nw_ref[...], mod_ref[3:4, :], mod_ref[4:5, :])
    h_ref[...] = h
    r = r_ref[...]
    h_hi, r_hi = h.astype(BF16), r.astype(BF16)
    h_lo, r_lo = (h - h_hi.astype(F32)).astype(BF16), (r - r_hi.astype(F32)).astype(BF16)
    logits = _dot(h_hi, r_hi) + (_dot(h_hi, r_lo) + _dot(h_lo, r_hi))
    logits = jnp.where(lane < N_EXPERTS, logits, NEG)
    v1 = jnp.max(logits, axis=-1, keepdims=True)
    i1 = jnp.min(jnp.where(logits == v1, lane, 128), axis=-1, keepdims=True)
    rest = jnp.where(lane == i1, NEG, logits)
    v2 = jnp.max(rest, axis=-1, keepdims=True)
    i2 = jnp.min(jnp.where(rest == v2, lane, 128), axis=-1, keepdims=True)
    e2 = jnp.exp(v2 - v1)
    route = jnp.where(lane == 0, i1.astype(F32), jnp.where(lane == 1, i2.astype(F32), 0.0))
    route = jnp.where(lane == 2, 1.0 / (1.0 + e2), jnp.where(lane == 3, e2 / (1.0 + e2), route))
    route_ref[...] = route


def moe_router(x, norm_w, mod, router, layer, idx, seq, *, tm=512):
    n, d = x.shape
    return pl.pallas_call(
        _router_kernel, grid=(n // tm,),
        in_specs=[pl.BlockSpec((tm, d), lambda i: (i, 0)),
                  pl.BlockSpec((None, 1, d), lambda i: (layer, 0, 0)),
                  pl.BlockSpec((None, None, 6, d), lambda i: (layer, (i * tm) // seq, 0, 0)),
                  pl.BlockSpec((None, d, 128), lambda i: (idx, 0, 0))],
        out_specs=[pl.BlockSpec((tm, d), lambda i: (i, 0)), pl.BlockSpec((tm, 128), lambda i: (i, 0))],
        out_shape=[jax.ShapeDtypeStruct((n, d), F32), jax.ShapeDtypeStruct((n, 128), F32)],
        compiler_params=_params(1), name="moe_router")(x, norm_w, mod, router)


def _moe_plan(route, n_exp, tile):
    n = route.shape[0]
    ef = route[:, :2].astype(jnp.int32).reshape(-1)
    onehot = (ef[:, None] == jnp.arange(n_exp, dtype=jnp.int32)[None, :]).astype(jnp.int32)
    csum = jnp.cumsum(onehot, axis=0)
    rank = jnp.sum(onehot * (csum - 1), axis=-1)
    counts = csum[-1]
    padded = ((counts + tile - 1) // tile) * tile
    ends = jnp.cumsum(padded)
    dest = (ends - padded)[ef] + rank
    rows = -(-(2 * n + n_exp * tile) // tile) * tile
    src = jnp.zeros((rows,), jnp.int32).at[dest].set(jnp.arange(2 * n, dtype=jnp.int32) // 2)
    tile_start = jnp.arange(rows // tile, dtype=jnp.int32) * tile
    tile_valid = (tile_start < ends[-1]).astype(jnp.int32)
    tile_expert = jnp.sum((ends[None, :] <= tile_start[:, None]).astype(jnp.int32), axis=1)
    tile_expert = jnp.minimum(tile_expert, n_exp - 1)
    last = jnp.maximum(ends[-1] // tile - 1, 0)
    tile_expert = jnp.where(tile_valid > 0, tile_expert, tile_expert[last])
    return dest, src, tile_expert, tile_valid


def _row_copy(src_hbm, row, dst_vmem, slot, sem):
    return pltpu.make_async_copy(src_hbm.at[pl.ds(row, 1), :], dst_vmem.at[pl.ds(slot, 1), :], sem)


def _experts_kernel(te_ref, tv_ref, src_ref, h_hbm, w1_ref, w3_ref, w2_ref, ys_ref,
                    rows_scr, xb_scr, acc_scr, sems, *, n_tiles, nf):
    i = pl.program_id(0)
    f = pl.program_id(1)
    tm = xb_scr.shape[0]
    valid = tv_ref[i] > 0
    slot = i % 2
    per_step = -(-tm // (8 * nf)) * 8
    last_rows = tm - per_step * (nf - 1)
    assert 0 < last_rows <= per_step
    nxt = jnp.minimum(i + 1, n_tiles - 1)

    def drain(sl):
        def body(r, carry):
            _row_copy(h_hbm, 0, rows_scr.at[sl], r, sems.at[sl]).wait()
            return carry
        lax.fori_loop(0, tm, body, 0, unroll=8)

    @pl.when(f == 0)
    def _():
        @pl.when(i == 0)
        def _():
            def body(r, carry):
                _row_copy(h_hbm, src_ref[r], rows_scr.at[0], r, sems.at[0]).start()
                return carry
            lax.fori_loop(0, tm, body, 0, unroll=8)
        drain(slot)
        xb_scr[...] = rows_scr[slot].astype(BF16)
        acc_scr[...] = jnp.zeros(acc_scr.shape, F32)

    def request_next(count):
        first = f * per_step
        for r in range(count):
            _row_copy(h_hbm, src_ref[nxt * tm + first + r], rows_scr.at[1 - slot], first + r,
                      sems.at[1 - slot]).start()

    def matmuls():
        h = xb_scr[...]
        g = _silu(_dot(h, w1_ref[...])) * _dot(h, w3_ref[...])
        acc_scr[...] += _dot(g.astype(BF16), w2_ref[...])

    for final, count in ((False, per_step), (True, last_rows)):
        step = (f == nf - 1) if final else (f < nf - 1)

        @pl.when(step & valid)
        def _():
            request_next(count)
            matmuls()

        @pl.when(step & jnp.logical_not(valid))
        def _():
            request_next(count)

    @pl.when(f == nf - 1)
    def _():
        ys_ref[...] = acc_scr[...]

        @pl.when(i == n_tiles - 1)
        def _():
            drain(1 - slot)


def moe_experts(h, src, tile_expert, tile_valid, w1, w3, w2, idx, *, tf=256):
    n, d = h.shape
    rows = src.shape[0]
    dff = w1.shape[-1]
    tm = MOE_TILE
    nf = dff // tf
    fcol = lambda i, f, tv: jnp.where(tv[i] > 0, f, nf - 1)
    return pl.pallas_call(
        functools.partial(_experts_kernel, n_tiles=rows // tm, nf=nf),
        grid_spec=pltpu.PrefetchScalarGridSpec(
            num_scalar_prefetch=3, grid=(rows // tm, nf),
            in_specs=[pl.BlockSpec(memory_space=pl.ANY),
                      pl.BlockSpec((None, None, d, tf), lambda i, f, te, tv, src: (idx, te[i], 0, fcol(i, f, tv))),
                      pl.BlockSpec((None, None, d, tf), lambda i, f, te, tv, src: (idx, te[i], 0, fcol(i, f, tv))),
                      pl.BlockSpec((None, None, tf, d), lambda i, f, te, tv, src: (idx, te[i], fcol(i, f, tv), 0))],
            out_specs=pl.BlockSpec((tm, d), lambda i, f, te, tv, src: (i, 0)),
            scratch_shapes=[pltpu.VMEM((2, tm, d), F32), pltpu.VMEM((tm, d), BF16), pltpu.VMEM((tm, d), F32),
                            pltpu.SemaphoreType.DMA((2,))]),
        out_shape=jax.ShapeDtypeStruct((rows, d), F32),
        compiler_params=_params(2), name="moe_experts")(tile_expert, tile_valid, src, h, w1, w3, w2)


def _moe_combine_kernel(dest_ref, ys_hbm, x_ref, route_ref, mod_ref, *rest, out_norm):
    if out_norm:
        fw_ref, o_ref, buf0, buf1, sems = rest
    else:
        o_ref, buf0, buf1, sems = rest
    tc = x_ref.shape[0]
    i = pl.program_id(0)
    slot = i % 2

    def issue(tile, sl):
        def body(t, carry):
            a = 2 * (tile * tc + t)
            _row_copy(ys_hbm, dest_ref[a], buf0.at[sl], t, sems.at[sl]).start()
            _row_copy(ys_hbm, dest_ref[a + 1], buf1.at[sl], t, sems.at[sl]).start()
            return carry
        lax.fori_loop(0, tc, body, 0, unroll=4)

    def drain(sl):
        def body(t, carry):
            _row_copy(ys_hbm, 0, buf0.at[sl], t, sems.at[sl]).wait()
            _row_copy(ys_hbm, 0, buf1.at[sl], t, sems.at[sl]).wait()
            return carry
        lax.fori_loop(0, tc, body, 0, unroll=4)

    @pl.when(i == 0)
    def _():
        issue(0, 0)

    @pl.when(i + 1 < pl.num_programs(0))
    def _():
        issue(i + 1, 1 - slot)
    drain(slot)
    route = route_ref[...]
    y = route[:, 2:3] * buf0[slot] + route[:, 3:4] * buf1[slot]
    out = x_ref[...] + mod_ref[5:6, :] * y
    if out_norm:
        out = out * lax.rsqrt(jnp.mean(out * out, axis=-1, keepdims=True) + EPS) * fw_ref[...]
    o_ref[...] = out


def moe_combine(ys, dest, x, route, mod, layer, seq, out_norm_w=None, *, tc=256):
    n, d = x.shape
    in_specs = [pl.BlockSpec(memory_space=pl.ANY),
                pl.BlockSpec((tc, d), lambda i, dest: (i, 0)),
                pl.BlockSpec((tc, 128), lambda i, dest: (i, 0)),
                pl.BlockSpec((None, None, 6, d), lambda i, dest: (layer, (i * tc) // seq, 0, 0))]
    args = [dest, ys, x, route, mod]
    if out_norm_w is not None:
        in_specs.append(pl.BlockSpec((1, d), lambda i, dest: (0, 0)))
        args.append(out_norm_w.reshape(1, d))
    return pl.pallas_call(
        functools.partial(_moe_combine_kernel, out_norm=out_norm_w is not None),
        grid_spec=pltpu.PrefetchScalarGridSpec(
            num_scalar_prefetch=1, grid=(n // tc,), in_specs=in_specs,
            out_specs=pl.BlockSpec((tc, d), lambda i, dest: (i, 0)),
            scratch_shapes=[pltpu.VMEM((2, tc, d), F32), pltpu.VMEM((2, tc, d), F32),
                            pltpu.SemaphoreType.DMA((2,))]),
        out_shape=jax.ShapeDtypeStruct((n, d), F32),
        compiler_params=_params(1), name="moe_combine")(*args)


def moe_residual(x, norm_w, mod, router, w1, w3, w2, layer, idx, seq, out_norm_w=None):
    h, route = moe_router(x, norm_w, mod, router, layer, idx, seq)
    dest, src, tile_expert, tile_valid = _moe_plan(route, w1.shape[1], MOE_TILE)
    ys = moe_experts(h, src, tile_expert, tile_valid, w1, w3, w2, idx)
    return moe_combine(ys, dest, x, route, mod, layer, seq, out_norm_w)


def _final_norm_kernel(x_ref, w_ref, o_ref):
    x = x_ref[...]
    o_ref[...] = x * lax.rsqrt(jnp.mean(x * x, axis=-1, keepdims=True) + EPS) * w_ref[...]


def final_norm(x, w, *, tm=512):
    n, d = x.shape
    return pl.pallas_call(
        _final_norm_kernel, grid=(n // tm,),
        in_specs=[pl.BlockSpec((tm, d), lambda i: (i, 0)), pl.BlockSpec((1, d), lambda i: (0, 0))],
        out_specs=pl.BlockSpec((tm, d), lambda i: (i, 0)),
        out_shape=jax.ShapeDtypeStruct((n, d), F32),
        compiler_params=_params(1), name="final_norm")(x, w.reshape(1, d))


_SRC_A, _SRC_B, _SRC_BG, _SRC_C, _SRC_CIF, _SRC_CO = 0, 1024, 2304, 2316, 4364, 4372
IN_COLS = 5396


def _w_in_prep_kernel(w_ref, o_ref):
    rows = w_ref.shape[0]

    def put(dst, src, width):
        o_ref[:, dst:dst + width] = w_ref[:, src:src + width].astype(BF16)

    put(COL_AQ, _SRC_A, _SRC_B - _SRC_A)
    put(COL_CQK, _SRC_C, _SRC_CIF - _SRC_C)
    put(COL_CO, _SRC_CO, IN_COLS - _SRC_CO)
    put(COL_BQ, _SRC_B, _SRC_BG - _SRC_B)
    n_small = (_SRC_C - _SRC_BG) + (_SRC_CO - _SRC_CIF)
    small = jnp.concatenate([w_ref[:, _SRC_BG:_SRC_C], w_ref[:, _SRC_CIF:_SRC_CO],
                             jnp.zeros((rows, 128 - n_small), F32)], axis=1)
    o_ref[:, COL_SMALL:COL_SMALL + 128] = small.astype(BF16)
    o_ref[:, COL_SMALL + 128:] = jnp.zeros((rows, PROJ_COLS - COL_SMALL - 128), BF16)


def _reorder_w_in(w_in, *, tr=256):
    depth, d, cols = w_in.shape
    assert cols == IN_COLS and d % tr == 0
    return pl.pallas_call(
        _w_in_prep_kernel, grid=(depth, d // tr),
        in_specs=[pl.BlockSpec((None, tr, cols), lambda l, i: (l, i, 0))],
        out_specs=pl.BlockSpec((None, tr, PROJ_COLS), lambda l, i: (l, i, 0)),
        out_shape=jax.ShapeDtypeStruct((depth, d, PROJ_COLS), BF16),
        compiler_params=_params(2), name="w_in_prep")(w_in)


def hybrid_mixer(proj, tables, layer, a_sinks, mlstm_gate_b, nsa_pe_k, nsa_pe_v, nsa_ck_w1, nsa_ck_w2,
                 nsa_cv_w1, nsa_cv_w2, mlstm_conv_w, mlstm_norm_w):
    bsz, t, _ = proj.shape
    o_a = sliding_window_attention(proj, col_q=COL_AQ, col_k=COL_AK, col_v=COL_AV, n_heads=A_HEADS,
                                   n_kv=A_KV_HEADS, window=A_WINDOW, tq=256, sub=128, tables=tables,
                                   sinks=a_sinks[layer])
    chunks = lambda col: proj[:, :, col:col + HEAD_DIM].reshape(bsz, t // CMP_STRIDE, CMP_STRIDE * HEAD_DIM)
    k_cmp = compress_blocks(chunks(COL_BKC), nsa_pe_k, nsa_ck_w1, nsa_ck_w2, layer)
    v_cmp = compress_blocks(chunks(COL_BVC), nsa_pe_v, nsa_cv_w1, nsa_cv_w2, layer)
    o_cmp, bias_t = compressed_attention_select(proj, k_cmp, v_cmp)
    o_sel = selected_attention(proj, bias_t)
    o_b = sliding_window_attention(proj, col_q=COL_BQ, col_k=COL_BKW, col_v=COL_BVW, n_heads=B_HEADS,
                                   n_kv=1, window=B_WINDOW, tq=512, sub=512, gate_with=(o_cmp, o_sel))
    o_c = mlstm_mixer(proj, mlstm_gate_b, mlstm_conv_w, mlstm_norm_w, layer)
    return o_a, o_b, o_c


def kernel(x, c, positions, ada_w, ada_b, norm_mix_w, norm_ffn_w, w_in, mlstm_gate_b, a_sinks, nsa_pe_k,
           nsa_pe_v, nsa_ck_w1, nsa_ck_w2, nsa_cv_w1, nsa_cv_w2, mlstm_conv_w, mlstm_norm_w, w_out, ffn_w1,
           ffn_w3, ffn_w2, moe_router, moe_w1, moe_w3, moe_w2, final_norm_w):
    bsz, t, d = x.shape
    depth = ada_w.shape[0]
    mod = ada_modulation(c, ada_w, ada_b)
    tables = rope_tables(positions)
    w_in_p = _reorder_w_in(w_in)
    w_out_b = w_out.astype(BF16)
    ffn_b = [w.astype(BF16) for w in (ffn_w1, ffn_w3, ffn_w2)]
    moe_b = [w.astype(BF16) for w in (moe_w1, moe_w3, moe_w2)]
    router_p = jnp.pad(moe_router, ((0, 0), (0, 0), (0, 128 - N_EXPERTS)))
    norm_mix = norm_mix_w.reshape(depth, 1, d)
    norm_ffn = norm_ffn_w.reshape(depth, 1, d)
    conv_w = mlstm_conv_w
    norm_c = mlstm_norm_w.reshape(depth, 1, -1)

    xf = x.reshape(bsz * t, d)
    for layer in range(depth):
        proj = norm_proj(xf, norm_mix, mod, w_in_p, layer, t, shift_row=0, scale_row=1)
        o_a, o_b, o_c = hybrid_mixer(proj.reshape(bsz, t, PROJ_COLS), tables, layer, a_sinks, mlstm_gate_b,
                                     nsa_pe_k, nsa_pe_v, nsa_ck_w1, nsa_ck_w2, nsa_cv_w1, nsa_cv_w2,
                                     conv_w, norm_c)
        flat = lambda a: a.reshape(bsz * t, a.shape[-1])
        xf = out_proj_residual(flat(o_a), flat(o_b), flat(o_c), w_out_b, xf, mod, layer, t)
        if layer % 2 == 0:
            xf = ffn_residual(xf, norm_ffn, mod, *ffn_b, layer, layer // 2, t)
        else:
            fused_norm = final_norm_w if layer == depth - 1 else None
            xf = moe_residual(xf, norm_ffn, mod, router_p, *moe_b, layer, layer // 2, t, fused_norm)
    if depth % 2 == 1:
        xf = final_norm(xf, final_norm_w)
    return xf.reshape(bsz, t, d)
```

```python
import functools
import math

import numpy as np
import jax
import jax.numpy as jnp
from jax import lax
from jax.experimental import pallas as pl
from jax.experimental.pallas import tpu as pltpu

F32 = jnp.float32
BF16 = jnp.bfloat16

HEAD_DIM = 128
A_HEADS, A_KV_HEADS, A_WINDOW = 4, 2, 128
ROPE_THETA = 150000.0
B_HEADS = 4
CMP_LEN, CMP_STRIDE = 32, 16
SEL_LEN, SEL_TOPK, B_WINDOW = 64, 16, 512
C_HEADS, C_QK_DIM, C_V_DIM, C_CONV = 4, 128, 256, 4
N_EXPERTS = 8
EPS = 1e-6
BIG = 1e9
NEG = -1e30
SCALE = HEAD_DIM ** -0.5

COL_AQ, COL_AK, COL_AV = 0, 512, 768
COL_CQK, COL_CV, COL_CO = 1024, 2048, 3072
COL_BQ = 4096
COL_BKC, COL_BVC, COL_BKS, COL_BVS, COL_BKW, COL_BVW = 4608, 4736, 4864, 4992, 5120, 5248
COL_SMALL = 5376
LANE_CI, LANE_CF = 12, 16
PROJ_COLS = 5632

VMEM_LIMIT_MB = 56
BF16_SUBLANES = 16


def _params(n_axes, vmem_mb=VMEM_LIMIT_MB):
    return pltpu.CompilerParams(dimension_semantics=("arbitrary",) * n_axes,
                                vmem_limit_bytes=vmem_mb * 2 ** 20)


def _dot(a, b):
    return jnp.dot(a, b, preferred_element_type=F32)


def _dot_nt(a, b):
    return lax.dot_general(a, b, (((1,), (1,)), ((), ())), preferred_element_type=F32)


def _silu(v):
    return v * jax.nn.sigmoid(v)


def _norm_mod(x, nw, shift, scale):
    ms = jnp.mean(x * x, axis=-1, keepdims=True)
    return (x * lax.rsqrt(ms + EPS) * nw) * (1.0 + scale) + shift


def _ada_kernel(c_ref, w_ref, b_ref, o_ref):
    act = _silu(c_ref[...]).astype(BF16)
    o_ref[...] = _dot(act, w_ref[...].astype(BF16)) + b_ref[...]


def ada_modulation(c, ada_w, ada_b):
    depth, d, n6 = ada_w.shape
    bsz = c.shape[0]
    assert bsz <= 8
    cp = jnp.zeros((8, d), F32).at[:bsz].set(c)
    tn = 512
    out = pl.pallas_call(
        _ada_kernel, grid=(depth, n6 // tn),
        in_specs=[pl.BlockSpec((8, d), lambda l, j: (0, 0)),
                  pl.BlockSpec((None, d, tn), lambda l, j: (l, 0, j)),
                  pl.BlockSpec((None, 1, tn), lambda l, j: (l, 0, j))],
        out_specs=pl.BlockSpec((None, 8, tn), lambda l, j: (l, 0, j)),
        out_shape=jax.ShapeDtypeStruct((depth, 8, n6), F32),
        compiler_params=_params(2), name="ada_modulation")(cp, ada_w, ada_b.reshape(depth, 1, n6))
    return out.reshape(depth, 8, 6, d)


def _rope_table_kernel(pos_ref, cos_ref, sin_ref):
    pos = pos_ref[...].astype(F32)
    lane = lax.broadcasted_iota(jnp.int32, (1, HEAD_DIM), 1)
    half = HEAD_DIM // 2
    inv = jnp.exp((lane & (half - 1)).astype(F32) * (-math.log(ROPE_THETA) * 2.0 / HEAD_DIM))
    ang = pos * inv
    cos_ref[...] = jnp.cos(ang)
    sin_ref[...] = jnp.where(lane < half, -1.0, 1.0) * jnp.sin(ang)


def rope_tables(positions):
    bsz, t = positions.shape
    tt = 512
    spec = pl.BlockSpec((None, tt, HEAD_DIM), lambda b, i: (b, i, 0))
    return pl.pallas_call(
        _rope_table_kernel, grid=(bsz, t // tt),
        in_specs=[pl.BlockSpec((None, tt, 1), lambda b, i: (b, i, 0))],
        out_specs=[spec, spec],
        out_shape=[jax.ShapeDtypeStruct((bsz, t, HEAD_DIM), F32)] * 2,
        compiler_params=_params(2), name="rope_tables")(positions.reshape(bsz, t, 1))


def _proj_kernel(x_ref, nw_ref, mod_ref, w_ref, o_ref, h_scr, *, shift_row, scale_row):
    @pl.when(pl.program_id(1) == 0)
    def _():
        h = _norm_mod(x_ref[...], nw_ref[...], mod_ref[shift_row:shift_row + 1, :],
                      mod_ref[scale_row:scale_row + 1, :])
        h_scr[...] = h.astype(BF16)

    o_ref[...] = _dot(h_scr[...], w_ref[...])


def norm_proj(x, norm_w, mod, w, layer, seq, *, shift_row, scale_row, tm=1024, tn=1408):
    n, d = x.shape
    p = w.shape[-1]
    assert seq % tm == 0 and p % tn == 0
    return pl.pallas_call(
        functools.partial(_proj_kernel, shift_row=shift_row, scale_row=scale_row),
        grid=(n // tm, p // tn),
        in_specs=[pl.BlockSpec((tm, d), lambda i, j: (i, 0)),
                  pl.BlockSpec((None, 1, d), lambda i, j: (layer, 0, 0)),
                  pl.BlockSpec((None, None, 6, d), lambda i, j: (layer, (i * tm) // seq, 0, 0)),
                  pl.BlockSpec((None, d, tn), lambda i, j: (layer, 0, j))],
        out_specs=pl.BlockSpec((tm, tn), lambda i, j: (i, j)),
        out_shape=jax.ShapeDtypeStruct((n, p), F32),
        scratch_shapes=[pltpu.VMEM((tm, d), BF16)],
        compiler_params=_params(2), name="norm_proj")(x, norm_w, mod, w)


def _swa_kernel(*refs, tq, sub, window, n_heads, group, rope, sinks, gated):
    it = iter(refs)
    q_ref, kp_ref, kc_ref, vp_ref, vc_ref = (next(it) for _ in range(5))
    if rope:
        cp_ref, sp_ref, cc_ref, sc_ref = (next(it) for _ in range(4))
    if sinks:
        sink_ref = next(it)
    if gated:
        g_ref, cmp_ref, sel_ref = (next(it) for _ in range(3))
        gates = jax.nn.sigmoid(g_ref[...])
    o_ref = next(it)
    i = pl.program_id(1)
    pad = -(-(window - 1) // sub) * sub
    span = pad + sub
    row = lax.broadcasted_iota(jnp.int32, (sub, span), 0)
    col = lax.broadcasted_iota(jnp.int32, (sub, span), 1)
    diff = row + pad - col

    def window_mask(n_prev):
        d = diff if n_prev == 0 else diff + jnp.where(col < n_prev, jnp.where(i > 0, 0, window), 0)
        return jnp.abs(2 * d - (window - 1)) <= (window - 1)
    masks = {}

    def rotate(v, c, s):
        return v * c + pltpu.roll(v, HEAD_DIM // 2, 1) * s

    kv_cache = {}
    for hq in range(n_heads):
        kv = hq // group
        if kv not in kv_cache:
            sl = slice(kv * HEAD_DIM, (kv + 1) * HEAD_DIM)
            kp, kc = kp_ref[:, sl], kc_ref[:, sl]
            if rope:
                kp = rotate(kp, cp_ref[...], sp_ref[...])
                kc = rotate(kc, cc_ref[...], sc_ref[...])
            k = jnp.concatenate([kp, kc], axis=0).astype(BF16)
            v = jnp.concatenate([vp_ref[:, sl], vc_ref[:, sl]], axis=0).astype(BF16)
            kv_cache[kv] = (k, v)
        k, v = kv_cache[kv]
        hs = slice(hq * HEAD_DIM, (hq + 1) * HEAD_DIM)
        q = q_ref[:, hs]
        if rope:
            q = rotate(q, cc_ref[...], sc_ref[...])
        q = (q * SCALE).astype(BF16)
        for j in range(tq // sub):
            rows = slice(j * sub, (j + 1) * sub)
            lo = tq + j * sub - pad
            n_prev = max(tq - lo, 0)
            if n_prev not in masks:
                masks[n_prev] = window_mask(n_prev)
            s = _dot_nt(q[rows], k[lo:lo + span])
            s = jnp.where(masks[n_prev], s, NEG)
            m = jnp.max(s, axis=-1, keepdims=True)
            if sinks:
                m = jnp.maximum(m, sink_ref[hq])
            m = jnp.where(m < 0.5 * NEG, 0.0, m)
            p = jnp.exp(s - m)
            den = jnp.sum(p, axis=-1, keepdims=True)
            if sinks:
                den = den + jnp.exp(sink_ref[hq] - m)
            o = _dot(p.astype(BF16), v[lo:lo + span]) / jnp.maximum(den, 1e-30)
            if gated:
                g = gates[rows]
                o = (g[:, 3 * hq:3 * hq + 1] * cmp_ref[rows, hs] + g[:, 3 * hq + 1:3 * hq + 2] * sel_ref[rows, hs]
                     + g[:, 3 * hq + 2:3 * hq + 3] * o)
            o_ref[rows, hs] = o.astype(o_ref.dtype)


def sliding_window_attention(proj, *, col_q, col_k, col_v, n_heads, n_kv, window, tq, sub,
                             tables=None, sinks=None, gate_with=None):
    bsz, t, _ = proj.shape
    assert window - 1 <= tq and t % tq == 0 and tq % sub == 0 and sub % 128 == 0
    wq, wkv = n_heads * HEAD_DIM, n_kv * HEAD_DIM
    assert col_q % wq == 0 and col_k % wkv == 0 and col_v % wkv == 0
    cur = lambda b, i: (b, i, 0)
    prev = lambda b, i: (b, jnp.maximum(i - 1, 0), 0)
    kvspec = lambda col, im: pl.BlockSpec(
        (None, tq, wkv), lambda b, i: (b, im(b, i)[1], col // wkv))
    in_specs = [pl.BlockSpec((None, tq, wq), lambda b, i: (b, i, col_q // wq)),
                kvspec(col_k, prev), kvspec(col_k, cur), kvspec(col_v, prev), kvspec(col_v, cur)]
    args = [proj] * 5
    if tables is not None:
        cos, sin = tables
        tp = pl.BlockSpec((None, tq, HEAD_DIM), prev)
        tc = pl.BlockSpec((None, tq, HEAD_DIM), cur)
        in_specs += [tp, tp, tc, tc]
        args += [cos, sin, cos, sin]
    if sinks is not None:
        in_specs.append(pl.BlockSpec(memory_space=pltpu.SMEM))
        args.append(sinks)
    if gate_with is not None:
        branch = pl.BlockSpec((None, tq, wq), cur)
        in_specs += [pl.BlockSpec((None, tq, 128), lambda b, i: (b, i, COL_SMALL // 128)), branch, branch]
        args += [proj, *gate_with]
    return pl.pallas_call(
        functools.partial(_swa_kernel, tq=tq, sub=sub, window=window, n_heads=n_heads, group=n_heads // n_kv,
                          rope=tables is not None, sinks=sinks is not None, gated=gate_with is not None),
        grid=(bsz, t // tq), in_specs=in_specs,
        out_specs=pl.BlockSpec((None, tq, wq), cur),
        out_shape=jax.ShapeDtypeStruct((bsz, t, wq), BF16),
        compiler_params=_params(2), name="sliding_window_attention")(*args)


def _compress_kernel(r_ref, pe_ref, w1_ref, w2_ref, o_ref):
    n16, half = r_ref.shape
    r = r_ref[...].astype(BF16)
    w1 = w1_ref[...].astype(BF16)
    first = _dot(r, w1[:half])
    second = _dot(r, w1[half:])
    pe = _dot(pe_ref[...].astype(BF16), w1)[0:1]
    pre = first + pltpu.roll(second, n16 - 1, 0) + pe
    hid = jax.nn.gelu(pre, approximate=True)
    o_ref[...] = _dot(hid.astype(BF16), w2_ref[...].astype(BF16))


def compress_blocks(chunks, pe, w1, w2, layer):
    bsz, n16, half = chunks.shape
    hid = w1.shape[-1]
    pe8 = jnp.broadcast_to(pe[layer].reshape(1, 2 * half), (8, 2 * half))
    return pl.pallas_call(
        _compress_kernel, grid=(bsz,),
        in_specs=[pl.BlockSpec((None, n16, half), lambda b: (b, 0, 0)),
                  pl.BlockSpec((8, 2 * half), lambda b: (0, 0)),
                  pl.BlockSpec((None, 2 * half, hid), lambda b: (layer, 0, 0)),
                  pl.BlockSpec((None, hid, HEAD_DIM), lambda b: (layer, 0, 0))],
        out_specs=pl.BlockSpec((None, n16, HEAD_DIM), lambda b: (b, 0, 0)),
        out_shape=jax.ShapeDtypeStruct((bsz, n16, HEAD_DIM), F32),
        compiler_params=_params(1), name="compress_blocks")(chunks, pe8, w1, w2)


def _heads_transposed(q, tq, scale=SCALE):
    n = q.shape[1] // HEAD_DIM
    return jnp.concatenate([(q[:, g * HEAD_DIM:(g + 1) * HEAD_DIM] * scale).T for g in range(n)],
                           axis=1).astype(BF16)


def _cmp_select_kernel(q_ref, kc_ref, vc_ref, ov_ref, ocmp_ref, bias_ref, *, tq, n_sel, topk):
    i = pl.program_id(1)
    nc = kc_ref.shape[0]
    hq = B_HEADS * tq
    qt = _heads_transposed(q_ref[...], tq)
    st = _dot(kc_ref[...].astype(BF16), qt)
    n_row = lax.broadcasted_iota(jnp.int32, (nc, hq), 0)
    t_col = i * tq + (lax.broadcasted_iota(jnp.int32, (nc, hq), 1) & (tq - 1))
    st = jnp.where(n_row * CMP_STRIDE + (CMP_LEN - 1) <= t_col, st, NEG)
    m = jnp.max(st, axis=0, keepdims=True)
    m = jnp.where(m < 0.5 * NEG, 0.0, m)
    p = jnp.exp(st - m)
    p = p / jnp.maximum(jnp.sum(p, axis=0, keepdims=True), 1e-30)
    pb = p.astype(BF16)
    ot = _dot(vc_ref[...].T.astype(BF16), pb)
    for g in range(B_HEADS):
        ocmp_ref[:, g * HEAD_DIM:(g + 1) * HEAD_DIM] = ot[:, g * tq:(g + 1) * tq].T
    imp4 = _dot(ov_ref[...], pb)
    imp = imp4[:, 0:tq]
    for g in range(1, B_HEADS):
        imp = imp + imp4[:, g * tq:(g + 1) * tq]
    nsp = imp.shape[0]
    j = lax.broadcasted_iota(jnp.int32, (nsp, tq), 0)
    t = i * tq + lax.broadcasted_iota(jnp.int32, (nsp, tq), 1)
    cur = t >> 6
    imp = jnp.where((j << 6) <= t, imp, -BIG)
    for forced in (0, cur, cur - 1):
        imp = jnp.where(j == forced, BIG, imp)
    imp = jnp.where(j < n_sel, imp, -3e38)
    sel = jnp.zeros((nsp, tq), F32)
    for _ in range(topk):
        mx = jnp.max(imp, axis=0, keepdims=True)
        idx = jnp.min(jnp.where(imp == mx, j, nsp), axis=0, keepdims=True)
        hit = j == idx
        sel = jnp.where(hit, 1.0, sel)
        imp = jnp.where(hit, -3e38, imp)
    bias_ref[...] = jnp.where(sel > 0.0, 0.0, jnp.where(j < n_sel, NEG, 0.0))


def _overlap_matrix_t(t):
    n16, n_sel = t // CMP_STRIDE, t // SEL_LEN
    nsp = -(-n_sel // 128) * 128
    starts = np.arange(n16) * CMP_STRIDE
    sel_start = np.arange(n_sel) * SEL_LEN
    ov = np.clip(np.minimum(starts[:, None] + CMP_LEN, sel_start[None, :] + SEL_LEN)
                 - np.maximum(starts[:, None], sel_start[None, :]), 0, None).astype(np.float32) / CMP_LEN
    out = np.zeros((nsp, n16), np.float32)
    out[:n_sel] = ov.T
    return jnp.asarray(out, BF16)


def compressed_attention_select(proj, k_cmp, v_cmp, *, tq=512):
    bsz, t, _ = proj.shape
    n16, n_sel = t // CMP_STRIDE, t // SEL_LEN
    assert SEL_LEN == 64 and tq % 128 == 0 and tq & (tq - 1) == 0 and t % tq == 0
    ov = _overlap_matrix_t(t)
    nsp = ov.shape[0]
    wq = B_HEADS * HEAD_DIM
    return pl.pallas_call(
        functools.partial(_cmp_select_kernel, tq=tq, n_sel=n_sel, topk=min(SEL_TOPK, n_sel)),
        grid=(bsz, t // tq),
        in_specs=[pl.BlockSpec((None, tq, wq), lambda b, i: (b, i, COL_BQ // wq)),
                  pl.BlockSpec((None, n16, HEAD_DIM), lambda b, i: (b, 0, 0)),
                  pl.BlockSpec((None, n16, HEAD_DIM), lambda b, i: (b, 0, 0)),
                  pl.BlockSpec((nsp, n16), lambda b, i: (0, 0))],
        out_specs=[pl.BlockSpec((None, tq, wq), lambda b, i: (b, i, 0)),
                   pl.BlockSpec((None, nsp, tq), lambda b, i: (b, 0, i))],
        out_shape=[jax.ShapeDtypeStruct((bsz, t, wq), F32),
                   jax.ShapeDtypeStruct((bsz, nsp, t), F32)],
        compiler_params=_params(2), name="compressed_attention_select")(proj, k_cmp, v_cmp, ov)


def _selected_kernel(q_ref, bias_ref, ks_ref, vs_ref, o_ref, kaug_scr, vt_scr, qa_scr, s0_scr, s1_scr,
                     m_scr, acc_scr, *, tq, kb):
    i = pl.program_id(1)
    t_all = ks_ref.shape[0]
    nsp = bias_ref.shape[0]
    hq = B_HEADS * tq
    ones_rows = vt_scr.shape[1] - HEAD_DIM

    @pl.when(i == 0)
    def _():
        def prep(c, carry):
            r0 = pl.multiple_of(c * kb, kb)
            kaug_scr[pl.ds(r0, kb), 0:HEAD_DIM] = ks_ref[pl.ds(r0, kb), :].astype(BF16)
            key = r0 + lax.broadcasted_iota(jnp.int32, (kb, nsp), 0)
            blk = lax.broadcasted_iota(jnp.int32, (kb, nsp), 1)
            kaug_scr[pl.ds(r0, kb), HEAD_DIM:] = jnp.where((key >> 6) == blk, 1.0, 0.0).astype(BF16)
            vt_scr[c] = jnp.concatenate([vs_ref[pl.ds(r0, kb), :].T, jnp.ones((ones_rows, kb), F32)],
                                        axis=0).astype(BF16)
            return carry
        lax.fori_loop(0, t_all // kb, prep, 0)

    bias = bias_ref[...]
    qa_scr[...] = jnp.concatenate([_heads_transposed(q_ref[...], tq, SCALE * math.log2(math.e)),
                                   jnp.concatenate([bias] * B_HEADS, axis=1).astype(BF16)], axis=0)
    m_scr[...] = jnp.full(m_scr.shape, NEG, F32)
    acc_scr[...] = jnp.zeros(acc_scr.shape, F32)

    def scores(kt, s_out):
        r0 = pl.multiple_of(kt * kb, kb)
        s_out[...] = _dot(kaug_scr[pl.ds(r0, kb), :], qa_scr[...])

    def consume(kt, s_in, causal):
        st = s_in[...]
        if causal:
            key = kt * kb + lax.broadcasted_iota(jnp.int32, (kb, hq), 0)
            tpos = i * tq + (lax.broadcasted_iota(jnp.int32, (kb, hq), 1) & (tq - 1))
            st = jnp.where(key <= tpos, st, NEG)
        m_old = m_scr[...]
        m_new = jnp.maximum(m_old, jnp.max(st, axis=0, keepdims=True))
        p = jnp.exp2((st - m_new).astype(BF16))
        acc_scr[...] = jnp.exp2(m_old - m_new) * acc_scr[...] + _dot(vt_scr[kt], p)
        m_scr[...] = m_new

    def step(kt, s_in, s_out):
        scores(kt + 1, s_out)
        consume(kt, s_in, False)

    def finish(kt, s_in):
        consume(kt, s_in, True)
        ot = acc_scr[0:HEAD_DIM, :] / jnp.maximum(acc_scr[HEAD_DIM:HEAD_DIM + 1, :], 1e-30)
        for g in range(B_HEADS):
            o_ref[:, g * HEAD_DIM:(g + 1) * HEAD_DIM] = ot[:, g * tq:(g + 1) * tq].T

    n_full = (i * tq) // kb
    scores(0, s0_scr)

    def pair(j, carry):
        step(2 * j, s0_scr, s1_scr)
        step(2 * j + 1, s1_scr, s0_scr)
        return carry
    lax.fori_loop(0, n_full // 2, pair, 0)

    @pl.when(n_full % 2 == 1)
    def _():
        step(n_full - 1, s0_scr, s1_scr)
        finish(n_full, s1_scr)

    @pl.when(n_full % 2 == 0)
    def _():
        finish(n_full, s0_scr)


def selected_attention(proj, bias_t, *, tq=512, kb=512):
    bsz, t, _ = proj.shape
    nsp = bias_t.shape[1]
    wq = B_HEADS * HEAD_DIM
    assert kb % tq == 0 and t % kb == 0
    return pl.pallas_call(
        functools.partial(_selected_kernel, tq=tq, kb=kb),
        grid=(bsz, t // tq),
        in_specs=[pl.BlockSpec((None, tq, wq), lambda b, i: (b, i, COL_BQ // wq)),
                  pl.BlockSpec((None, nsp, tq), lambda b, i: (b, 0, i)),
                  pl.BlockSpec((None, t, HEAD_DIM), lambda b, i: (b, 0, COL_BKS // HEAD_DIM)),
                  pl.BlockSpec((None, t, HEAD_DIM), lambda b, i: (b, 0, COL_BVS // HEAD_DIM))],
        out_specs=pl.BlockSpec((None, tq, wq), lambda b, i: (b, i, 0)),
        out_shape=jax.ShapeDtypeStruct((bsz, t, wq), F32),
        scratch_shapes=[pltpu.VMEM((t, HEAD_DIM + nsp), BF16),
                        pltpu.VMEM((t // kb, HEAD_DIM + BF16_SUBLANES, kb), BF16),
                        pltpu.VMEM((HEAD_DIM + nsp, B_HEADS * tq), BF16),
                        pltpu.VMEM((kb, B_HEADS * tq), F32),
                        pltpu.VMEM((kb, B_HEADS * tq), F32),
                        pltpu.VMEM((1, B_HEADS * tq), F32),
                        pltpu.VMEM((HEAD_DIM + BF16_SUBLANES, B_HEADS * tq), F32)],
        compiler_params=_params(2), name="selected_attention")(proj, bias_t, proj, proj)


def _mlstm_kernel(gb_ref, qk_ref, v_ref, og_ref, small_ref, convw_ref, normw_ref, o_ref,
                  hist_scr, c_scr, n_scr, m_scr, *, chunk):
    L = chunk
    dqk, dv = C_QK_DIM, C_V_DIM

    @pl.when(pl.program_id(1) == 0)
    def _():
        hist_scr[...] = jnp.zeros(hist_scr.shape, F32)
        c_scr[...] = jnp.zeros(c_scr.shape, F32)
        n_scr[...] = jnp.zeros(n_scr.shape, F32)
        m_scr[...] = jnp.zeros(m_scr.shape, F32)

    x = qk_ref[...]
    xe = jnp.concatenate([hist_scr[...], x], axis=0)
    w = convw_ref[...]
    y = w[0:1] * xe[8 - 3:8 - 3 + L]
    for tap in range(1, C_CONV):
        y = y + w[tap:tap + 1] * xe[8 - 3 + tap:8 - 3 + tap + L]
    hist_scr[...] = x[L - 8:L]
    qk = _silu(y)

    lane = lax.broadcasted_iota(jnp.int32, (1, 128), 1)
    gbias = jnp.zeros((1, 128), F32)
    for idx in range(2 * C_HEADS):
        gbias = jnp.where(lane == LANE_CI + idx, gb_ref[idx], gbias)
    pre = small_ref[...] + gbias
    logf = jnp.minimum(pre, 0.0) - jnp.log1p(jnp.exp(-jnp.abs(pre)))
    row = lax.broadcasted_iota(jnp.int32, (L, L), 0)
    col = lax.broadcasted_iota(jnp.int32, (L, L), 1)
    causal = row >= col
    tri = jnp.where(causal, 1.0, 0.0)
    hp = lax.Precision.HIGHEST
    b_cols = jnp.dot(tri, logf, precision=hp, preferred_element_type=F32)
    b_rows = lax.dot_general(logf.T, tri, (((1,), (1,)), ((), ())), precision=hp,
                             preferred_element_type=F32)
    pre_t = pre.T

    for h in range(C_HEADS):
        q = qk[:, h * dqk:(h + 1) * dqk]
        k = qk[:, C_HEADS * dqk + h * dqk:C_HEADS * dqk + (h + 1) * dqk] * (dqk ** -0.5)
        v = v_ref[:, h * dv:(h + 1) * dv].astype(BF16)
        b_col = b_cols[:, LANE_CF + h:LANE_CF + h + 1]
        i_col = pre[:, LANE_CI + h:LANE_CI + h + 1]
        b_row = b_rows[LANE_CF + h:LANE_CF + h + 1, :]
        i_row = pre_t[LANE_CI + h:LANE_CI + h + 1, :]
        m_prev = m_scr[h:h + 1, 0:1]
        ct = c_scr[h]
        n_row = n_scr[h:h + 1, :]

        d = jnp.where(causal, b_col - b_row + i_row, NEG)
        m_inter = b_col + m_prev
        m_t = jnp.maximum(m_inter, jnp.max(d, axis=-1, keepdims=True))
        qb = q.astype(BF16)
        s = _dot_nt(qb, k.astype(BF16)) * jnp.exp(d - m_t)
        inter = jnp.exp(m_inter - m_t)
        num = _dot(s.astype(BF16), v) + inter * _dot(qb, ct.astype(BF16))
        den = jnp.sum(s, axis=-1, keepdims=True) + inter * jnp.sum(q * n_row, axis=-1, keepdims=True)
        hh = num / jnp.maximum(jnp.abs(den), jnp.exp(-m_t))

        b_last = b_col[L - 1:L, :]
        g = b_last - b_col + i_col
        m_new = jnp.maximum(b_last + m_prev, jnp.max(g, axis=0, keepdims=True))
        kd = k * jnp.exp(g - m_new)
        keep = jnp.exp(b_last + m_prev - m_new)
        c_scr[h] = keep * ct + _dot(kd.T.astype(BF16), v)
        n_scr[h:h + 1, :] = keep * n_row + jnp.sum(kd, axis=0, keepdims=True)
        m_scr[h:h + 1, :] = jnp.broadcast_to(m_new, (1, 128))

        vs = slice(h * dv, (h + 1) * dv)
        hn = hh * lax.rsqrt(jnp.mean(hh * hh, axis=-1, keepdims=True) + EPS) * normw_ref[:, vs]
        o_ref[:, vs] = (hn * jax.nn.sigmoid(og_ref[:, vs])).astype(o_ref.dtype)


def mlstm_mixer(proj, gate_b, conv_w, norm_w, layer, *, chunk=128):
    bsz, t, _ = proj.shape
    wqk, wv = 2 * C_HEADS * C_QK_DIM, C_HEADS * C_V_DIM
    assert wqk == wv == 1024 and t % chunk == 0
    blk = lambda col: pl.BlockSpec((None, chunk, 1024), lambda b, i: (b, i, col // 1024))
    return pl.pallas_call(
        functools.partial(_mlstm_kernel, chunk=chunk),
        grid=(bsz, t // chunk),
        in_specs=[pl.BlockSpec(memory_space=pltpu.SMEM),
                  blk(COL_CQK), blk(COL_CV), blk(COL_CO),
                  pl.BlockSpec((None, chunk, 128), lambda b, i: (b, i, COL_SMALL // 128)),
                  pl.BlockSpec((None, C_CONV, wqk), lambda b, i: (layer, 0, 0)),
                  pl.BlockSpec((None, 1, wv), lambda b, i: (layer, 0, 0))],
        out_specs=pl.BlockSpec((None, chunk, wv), lambda b, i: (b, i, 0)),
        out_shape=jax.ShapeDtypeStruct((bsz, t, wv), BF16),
        scratch_shapes=[pltpu.VMEM((8, wqk), F32),
                        pltpu.VMEM((C_HEADS, C_QK_DIM, C_V_DIM), F32),
                        pltpu.VMEM((8, C_QK_DIM), F32),
                        pltpu.VMEM((8, 128), F32)],
        compiler_params=_params(2), name="mlstm_mixer")(
            gate_b[layer], proj, proj, proj, proj, conv_w, norm_w)


def _outproj_kernel(oa_ref, ob_ref, oc_ref, w_ref, x_ref, mod_ref, o_ref, *, gate_row):
    wa, wb = oa_ref.shape[1], ob_ref.shape[1]
    acc = _dot(oa_ref[...], w_ref[0:wa, :])
    acc = acc + _dot(ob_ref[...], w_ref[wa:wa + wb, :])
    acc = acc + _dot(oc_ref[...], w_ref[wa + wb:, :])
    o_ref[...] = x_ref[...] + mod_ref[gate_row:gate_row + 1, :] * acc


def out_proj_residual(o_a, o_b, o_c, w_out, x, mod, layer, seq, *, tm=512, tn=2048):
    n, d = x.shape
    assert seq % tm == 0 and d % tn == 0
    wa, wb, wc = o_a.shape[1], o_b.shape[1], o_c.shape[1]
    return pl.pallas_call(
        functools.partial(_outproj_kernel, gate_row=2),
        grid=(n // tm, d // tn),
        in_specs=[pl.BlockSpec((tm, wa), lambda i, j: (i, 0)),
                  pl.BlockSpec((tm, wb), lambda i, j: (i, 0)),
                  pl.BlockSpec((tm, wc), lambda i, j: (i, 0)),
                  pl.BlockSpec((None, wa + wb + wc, tn), lambda i, j: (layer, 0, j)),
                  pl.BlockSpec((tm, tn), lambda i, j: (i, j)),
                  pl.BlockSpec((None, None, 6, tn), lambda i, j: (layer, (i * tm) // seq, 0, j))],
        out_specs=pl.BlockSpec((tm, tn), lambda i, j: (i, j)),
        out_shape=jax.ShapeDtypeStruct((n, d), F32),
        compiler_params=_params(2), name="out_proj_residual")(o_a, o_b, o_c, w_out, x, mod)


def _ffn_kernel(x_ref, nw_ref, mod_ref, w1_ref, w3_ref, w2_ref, o_ref, h_scr, acc_scr):
    f = pl.program_id(1)

    @pl.when(f == 0)
    def _():
        h = _norm_mod(x_ref[...], nw_ref[...], mod_ref[3:4, :], mod_ref[4:5, :])
        h_scr[...] = h.astype(BF16)
        acc_scr[...] = jnp.zeros(acc_scr.shape, F32)

    h = h_scr[...]
    g = _silu(_dot(h, w1_ref[...])) * _dot(h, w3_ref[...])
    acc_scr[...] += _dot(g.astype(BF16), w2_ref[...])

    @pl.when(f == pl.num_programs(1) - 1)
    def _():
        o_ref[...] = x_ref[...] + mod_ref[5:6, :] * acc_scr[...]


def ffn_residual(x, norm_w, mod, w1, w3, w2, layer, idx, seq, *, tm=512, tf=512):
    n, d = x.shape
    dff = w1.shape[-1]
    return pl.pallas_call(
        _ffn_kernel, grid=(n // tm, dff // tf),
        in_specs=[pl.BlockSpec((tm, d), lambda i, f: (i, 0)),
                  pl.BlockSpec((None, 1, d), lambda i, f: (layer, 0, 0)),
                  pl.BlockSpec((None, None, 6, d), lambda i, f: (layer, (i * tm) // seq, 0, 0)),
                  pl.BlockSpec((None, d, tf), lambda i, f: (idx, 0, f)),
                  pl.BlockSpec((None, d, tf), lambda i, f: (idx, 0, f)),
                  pl.BlockSpec((None, tf, d), lambda i, f: (idx, f, 0))],
        out_specs=pl.BlockSpec((tm, d), lambda i, f: (i, 0)),
        out_shape=jax.ShapeDtypeStruct((n, d), F32),
        scratch_shapes=[pltpu.VMEM((tm, d), BF16), pltpu.VMEM((tm, d), F32)],
        compiler_params=_params(2), name="ffn_residual")(x, norm_w, mod, w1, w3, w2)


MOE_TILE = 512


def _router_kernel(x_ref, nw_ref, mod_ref, r_ref, h_ref, route_ref):
    tm = x_ref.shape[0]
    lane = lax.broadcasted_iota(jnp.int32, (tm, 128), 1)
    h = _norm_mod(x_ref[...], nw_ref[...], mod_ref[3:4, :], mod_ref[4:5, :])
    h_ref[...] = h
    r = r_ref[...]
    h_hi, r_hi = h.astype(BF16), r.astype(BF16)
    h_lo, r_lo = (h - h_hi.astype(F32)).astype(BF16), (r - r_hi.astype(F32)).astype(BF16)
    logits = _dot(h_hi, r_hi) + (_dot(h_hi, r_lo) + _dot(h_lo, r_hi))
    logits = jnp.where(lane < N_EXPERTS, logits, NEG)
    v1 = jnp.max(logits, axis=-1, keepdims=True)
    i1 = jnp.min(jnp.where(logits == v1, lane, 128), axis=-1, keepdims=True)
    rest = jnp.where(lane == i1, NEG, logits)
    v2 = jnp.max(rest, axis=-1, keepdims=True)
    i2 = jnp.min(jnp.where(rest == v2, lane, 128), axis=-1, keepdims=True)
    e2 = jnp.exp(v2 - v1)
    route = jnp.where(lane == 0, i1.astype(F32), jnp.where(lane == 1, i2.astype(F32), 0.0))
    route = jnp.where(lane == 2, 1.0 / (1.0 + e2), jnp.where(lane == 3, e2 / (1.0 + e2), route))
    route_ref[...] = route


def moe_router(x, norm_w, mod, router, layer, idx, seq, *, tm=512):
    n, d = x.shape
    return pl.pallas_call(
        _router_kernel, grid=(n // tm,),
        in_specs=[pl.BlockSpec((tm, d), lambda i: (i, 0)),
                  pl.BlockSpec((None, 1, d), lambda i: (layer, 0, 0)),
                  pl.BlockSpec((None, None, 6, d), lambda i: (layer, (i * tm) // seq, 0, 0)),
                  pl.BlockSpec((None, d, 128), lambda i: (idx, 0, 0))],
        out_specs=[pl.BlockSpec((tm, d), lambda i: (i, 0)), pl.BlockSpec((tm, 128), lambda i: (i, 0))],
        out_shape=[jax.ShapeDtypeStruct((n, d), F32), jax.ShapeDtypeStruct((n, 128), F32)],
        compiler_params=_params(1), name="moe_router")(x, norm_w, mod, router)


def _moe_plan(route, n_exp, tile):
    n = route.shape[0]
    ef = route[:, :2].astype(jnp.int32).reshape(-1)
    onehot = (ef[:, None] == jnp.arange(n_exp, dtype=jnp.int32)[None, :]).astype(jnp.int32)
    csum = jnp.cumsum(onehot, axis=0)
    rank = jnp.sum(onehot * (csum - 1), axis=-1)
    counts = csum[-1]
    padded = ((counts + tile - 1) // tile) * tile
    ends = jnp.cumsum(padded)
    dest = (ends - padded)[ef] + rank
    rows = -(-(2 * n + n_exp * tile) // tile) * tile
    src = jnp.zeros((rows,), jnp.int32).at[dest].set(jnp.arange(2 * n, dtype=jnp.int32) // 2)
    tile_start = jnp.arange(rows // tile, dtype=jnp.int32) * tile
    tile_valid = (tile_start < ends[-1]).astype(jnp.int32)
    tile_expert = jnp.sum((ends[None, :] <= tile_start[:, None]).astype(jnp.int32), axis=1)
    tile_expert = jnp.minimum(tile_expert, n_exp - 1)
    last = jnp.maximum(ends[-1] // tile - 1, 0)
    tile_expert = jnp.where(tile_valid > 0, tile_expert, tile_expert[last])
    return dest, src, tile_expert, tile_valid


def _row_copy(src_hbm, row, dst_vmem, slot, sem):
    return pltpu.make_async_copy(src_hbm.at[pl.ds(row, 1), :], dst_vmem.at[pl.ds(slot, 1), :], sem)


def _experts_kernel(te_ref, tv_ref, src_ref, h_hbm, w1_ref, w3_ref, w2_ref, ys_ref,
                    rows_scr, xb_scr, acc_scr, sems, *, n_tiles, nf):
    i = pl.program_id(0)
    f = pl.program_id(1)
    tm = xb_scr.shape[0]
    valid = tv_ref[i] > 0
    slot = i % 2
    per_step = -(-tm // (8 * nf)) * 8
    last_rows = tm - per_step * (nf - 1)
    assert 0 < last_rows <= per_step
    nxt = jnp.minimum(i + 1, n_tiles - 1)

    def drain(sl):
        def body(r, carry):
            _row_copy(h_hbm, 0, rows_scr.at[sl], r, sems.at[sl]).wait()
            return carry
        lax.fori_loop(0, tm, body, 0, unroll=8)

    @pl.when(f == 0)
    def _():
        @pl.when(i == 0)
        def _():
            def body(r, carry):
                _row_copy(h_hbm, src_ref[r], rows_scr.at[0], r, sems.at[0]).start()
                return carry
            lax.fori_loop(0, tm, body, 0, unroll=8)
        drain(slot)
        xb_scr[...] = rows_scr[slot].astype(BF16)
        acc_scr[...] = jnp.zeros(acc_scr.shape, F32)

    def request_next(count):
        first = f * per_step
        for r in range(count):
            _row_copy(h_hbm, src_ref[nxt * tm + first + r], rows_scr.at[1 - slot], first + r,
                      sems.at[1 - slot]).start()

    def matmuls():
        h = xb_scr[...]
        g = _silu(_dot(h, w1_ref[...])) * _dot(h, w3_ref[...])
        acc_scr[...] += _dot(g.astype(BF16), w2_ref[...])

    for final, count in ((False, per_step), (True, last_rows)):
        step = (f == nf - 1) if final else (f < nf - 1)

        @pl.when(step & valid)
        def _():
            request_next(count)
            matmuls()

        @pl.when(step & jnp.logical_not(valid))
        def _():
            request_next(count)

    @pl.when(f == nf - 1)
    def _():
        ys_ref[...] = acc_scr[...]

        @pl.when(i == n_tiles - 1)
        def _():
            drain(1 - slot)


def moe_experts(h, src, tile_expert, tile_valid, w1, w3, w2, idx, *, tf=256):
    n, d = h.shape
    rows = src.shape[0]
    dff = w1.shape[-1]
    tm = MOE_TILE
    nf = dff // tf
    fcol = lambda i, f, tv: jnp.where(tv[i] > 0, f, nf - 1)
    return pl.pallas_call(
        functools.partial(_experts_kernel, n_tiles=rows // tm, nf=nf),
        grid_spec=pltpu.PrefetchScalarGridSpec(
            num_scalar_prefetch=3, grid=(rows // tm, nf),
            in_specs=[pl.BlockSpec(memory_space=pl.ANY),
                      pl.BlockSpec((None, None, d, tf), lambda i, f, te, tv, src: (idx, te[i], 0, fcol(i, f, tv))),
                      pl.BlockSpec((None, None, d, tf), lambda i, f, te, tv, src: (idx, te[i], 0, fcol(i, f, tv))),
                      pl.BlockSpec((None, None, tf, d), lambda i, f, te, tv, src: (idx, te[i], fcol(i, f, tv), 0))],
            out_specs=pl.BlockSpec((tm, d), lambda i, f, te, tv, src: (i, 0)),
            scratch_shapes=[pltpu.VMEM((2, tm, d), F32), pltpu.VMEM((tm, d), BF16), pltpu.VMEM((tm, d), F32),
                            pltpu.SemaphoreType.DMA((2,))]),
        out_shape=jax.ShapeDtypeStruct((rows, d), F32),
        compiler_params=_params(2), name="moe_experts")(tile_expert, tile_valid, src, h, w1, w3, w2)


def _moe_combine_kernel(dest_ref, ys_hbm, x_ref, route_ref, mod_ref, *rest, out_norm):
    if out_norm:
        fw_ref, o_ref, buf0, buf1, sems = rest
    else:
        o_ref, buf0, buf1, sems = rest
    tc = x_ref.shape[0]
    i = pl.program_id(0)
    slot = i % 2

    def issue(tile, sl):
        def body(t, carry):
            a = 2 * (tile * tc + t)
            _row_copy(ys_hbm, dest_ref[a], buf0.at[sl], t, sems.at[sl]).start()
            _row_copy(ys_hbm, dest_ref[a + 1], buf1.at[sl], t, sems.at[sl]).start()
            return carry
        lax.fori_loop(0, tc, body, 0, unroll=4)

    def drain(sl):
        def body(t, carry):
            _row_copy(ys_hbm, 0, buf0.at[sl], t, sems.at[sl]).wait()
            _row_copy(ys_hbm, 0, buf1.at[sl], t, sems.at[sl]).wait()
            return carry
        lax.fori_loop(0, tc, body, 0, unroll=4)

    @pl.when(i == 0)
    def _():
        issue(0, 0)

    @pl.when(i + 1 < pl.num_programs(0))
    def _():
        issue(i + 1, 1 - slot)
    drain(slot)
    route = route_ref[...]
    y = route[:, 2:3] * buf0[slot] + route[:, 3:4] * buf1[slot]
    out = x_ref[...] + mod_ref[5:6, :] * y
    if out_norm:
        out = out * lax.rsqrt(jnp.mean(out * out, axis=-1, keepdims=True) + EPS) * fw_ref[...]
    o_ref[...] = out


def moe_combine(ys, dest, x, route, mod, layer, seq, out_norm_w=None, *, tc=256):
    n, d = x.shape
    in_specs = [pl.BlockSpec(memory_space=pl.ANY),
                pl.BlockSpec((tc, d), lambda i, dest: (i, 0)),
                pl.BlockSpec((tc, 128), lambda i, dest: (i, 0)),
                pl.BlockSpec((None, None, 6, d), lambda i, dest: (layer, (i * tc) // seq, 0, 0))]
    args = [dest, ys, x, route, mod]
    if out_norm_w is not None:
        in_specs.append(pl.BlockSpec((1, d), lambda i, dest: (0, 0)))
        args.append(out_norm_w.reshape(1, d))
    return pl.pallas_call(
        functools.partial(_moe_combine_kernel, out_norm=out_norm_w is not None),
        grid_spec=pltpu.PrefetchScalarGridSpec(
            num_scalar_prefetch=1, grid=(n // tc,), in_specs=in_specs,
            out_specs=pl.BlockSpec((tc, d), lambda i, dest: (i, 0)),
            scratch_shapes=[pltpu.VMEM((2, tc, d), F32), pltpu.VMEM((2, tc, d), F32),
                            pltpu.SemaphoreType.DMA((2,))]),
        out_shape=jax.ShapeDtypeStruct((n, d), F32),
        compiler_params=_params(1), name="moe_combine")(*args)


def moe_residual(x, norm_w, mod, router, w1, w3, w2, layer, idx, seq, out_norm_w=None):
    h, route = moe_router(x, norm_w, mod, router, layer, idx, seq)
    dest, src, tile_expert, tile_valid = _moe_plan(route, w1.shape[1], MOE_TILE)
    ys = moe_experts(h, src, tile_expert, tile_valid, w1, w3, w2, idx)
    return moe_combine(ys, dest, x, route, mod, layer, seq, out_norm_w)


def _final_norm_kernel(x_ref, w_ref, o_ref):
    x = x_ref[...]
    o_ref[...] = x * lax.rsqrt(jnp.mean(x * x, axis=-1, keepdims=True) + EPS) * w_ref[...]


def final_norm(x, w, *, tm=512):
    n, d = x.shape
    return pl.pallas_call(
        _final_norm_kernel, grid=(n // tm,),
        in_specs=[pl.BlockSpec((tm, d), lambda i: (i, 0)), pl.BlockSpec((1, d), lambda i: (0, 0))],
        out_specs=pl.BlockSpec((tm, d), lambda i: (i, 0)),
        out_shape=jax.ShapeDtypeStruct((n, d), F32),
        compiler_params=_params(1), name="final_norm")(x, w.reshape(1, d))


_SRC_A, _SRC_B, _SRC_BG, _SRC_C, _SRC_CIF, _SRC_CO = 0, 1024, 2304, 2316, 4364, 4372
IN_COLS = 5396


def _w_in_prep_kernel(w_ref, o_ref):
    rows = w_ref.shape[0]

    def put(dst, src, width):
        o_ref[:, dst:dst + width] = w_ref[:, src:src + width].astype(BF16)

    put(COL_AQ, _SRC_A, _SRC_B - _SRC_A)
    put(COL_CQK, _SRC_C, _SRC_CIF - _SRC_C)
    put(COL_CO, _SRC_CO, IN_COLS - _SRC_CO)
    put(COL_BQ, _SRC_B, _SRC_BG - _SRC_B)
    n_small = (_SRC_C - _SRC_BG) + (_SRC_CO - _SRC_CIF)
    small = jnp.concatenate([w_ref[:, _SRC_BG:_SRC_C], w_ref[:, _SRC_CIF:_SRC_CO],
                             jnp.zeros((rows, 128 - n_small), F32)], axis=1)
    o_ref[:, COL_SMALL:COL_SMALL + 128] = small.astype(BF16)
    o_ref[:, COL_SMALL + 128:] = jnp.zeros((rows, PROJ_COLS - COL_SMALL - 128), BF16)


def _reorder_w_in(w_in, *, tr=256):
    depth, d, cols = w_in.shape
    assert cols == IN_COLS and d % tr == 0
    return pl.pallas_call(
        _w_in_prep_kernel, grid=(depth, d // tr),
        in_specs=[pl.BlockSpec((None, tr, cols), lambda l, i: (l, i, 0))],
        out_specs=pl.BlockSpec((None, tr, PROJ_COLS), lambda l, i: (l, i, 0)),
        out_shape=jax.ShapeDtypeStruct((depth, d, PROJ_COLS), BF16),
        compiler_params=_params(2), name="w_in_prep")(w_in)


def hybrid_mixer(proj, tables, layer, a_sinks, mlstm_gate_b, nsa_pe_k, nsa_pe_v, nsa_ck_w1, nsa_ck_w2,
                 nsa_cv_w1, nsa_cv_w2, mlstm_conv_w, mlstm_norm_w):
    bsz, t, _ = proj.shape
    o_a = sliding_window_attention(proj, col_q=COL_AQ, col_k=COL_AK, col_v=COL_AV, n_heads=A_HEADS,
                                   n_kv=A_KV_HEADS, window=A_WINDOW, tq=256, sub=128, tables=tables,
                                   sinks=a_sinks[layer])
    chunks = lambda col: proj[:, :, col:col + HEAD_DIM].reshape(bsz, t // CMP_STRIDE, CMP_STRIDE * HEAD_DIM)
    k_cmp = compress_blocks(chunks(COL_BKC), nsa_pe_k, nsa_ck_w1, nsa_ck_w2, layer)
    v_cmp = compress_blocks(chunks(COL_BVC), nsa_pe_v, nsa_cv_w1, nsa_cv_w2, layer)
    o_cmp, bias_t = compressed_attention_select(proj, k_cmp, v_cmp)
    o_sel = selected_attention(proj, bias_t)
    o_b = sliding_window_attention(proj, col_q=COL_BQ, col_k=COL_BKW, col_v=COL_BVW, n_heads=B_HEADS,
                                   n_kv=1, window=B_WINDOW, tq=512, sub=512, gate_with=(o_cmp, o_sel))
    o_c = mlstm_mixer(proj, mlstm_gate_b, mlstm_conv_w, mlstm_norm_w, layer)
    return o_a, o_b, o_c


def kernel(x, c, positions, ada_w, ada_b, norm_mix_w, norm_ffn_w, w_in, mlstm_gate_b, a_sinks, nsa_pe_k,
           nsa_pe_v, nsa_ck_w1, nsa_ck_w2, nsa_cv_w1, nsa_cv_w2, mlstm_conv_w, mlstm_norm_w, w_out, ffn_w1,
           ffn_w3, ffn_w2, moe_router, moe_w1, moe_w3, moe_w2, final_norm_w):
    bsz, t, d = x.shape
    depth = ada_w.shape[0]
    mod = ada_modulation(c, ada_w, ada_b)
    tables = rope_tables(positions)
    w_in_p = _reorder_w_in(w_in)
    w_out_b = w_out.astype(BF16)
    ffn_b = [w.astype(BF16) for w in (ffn_w1, ffn_w3, ffn_w2)]
    moe_b = [w.astype(BF16) for w in (moe_w1, moe_w3, moe_w2)]
    router_p = jnp.pad(moe_router, ((0, 0), (0, 0), (0, 128 - N_EXPERTS)))
    norm_mix = norm_mix_w.reshape(depth, 1, d)
    norm_ffn = norm_ffn_w.reshape(depth, 1, d)
    conv_w = mlstm_conv_w
    norm_c = mlstm_norm_w.reshape(depth, 1, -1)

    xf = x.reshape(bsz * t, d)
    for layer in range(depth):
        proj = norm_proj(xf, norm_mix, mod, w_in_p, layer, t, shift_row=0, scale_row=1)
        o_a, o_b, o_c = hybrid_mixer(proj.reshape(bsz, t, PROJ_COLS), tables, layer, a_sinks, mlstm_gate_b,
                                     nsa_pe_k, nsa_pe_v, nsa_ck_w1, nsa_ck_w2, nsa_cv_w1, nsa_cv_w2,
                                     conv_w, norm_c)
        flat = lambda a: a.reshape(bsz * t, a.shape[-1])
        xf = out_proj_residual(flat(o_a), flat(o_b), flat(o_c), w_out_b, xf, mod, layer, t)
        if layer % 2 == 0:
            xf = ffn_residual(xf, norm_ffn, mod, *ffn_b, layer, layer // 2, t)
        else:
            fused_norm = final_norm_w if layer == depth - 1 else None
            xf = moe_residual(xf, norm_ffn, mod, router_p, *moe_b, layer, layer // 2, t, fused_norm)
    if depth % 2 == 1:
        xf = final_norm(xf, final_norm_w)
    return xf.reshape(bsz, t, d)
```
